```python
import jax, jax.numpy as jnp
from jax import lax
import numpy as np

D_MODEL = 1024
BATCH = 2
SEQ = 8192
DEPTH = 4

N_EVEN = (DEPTH + 1) // 2
N_ODD = DEPTH // 2
DEEPNORM_ALPHA = (2 * DEPTH) ** 0.25
DEEPNORM_BETA = (8 * DEPTH) ** -0.25
LN_EPS = 1e-5
RMS_EPS = 1e-6

SWA_HEADS = 8
SWA_KV_HEADS = 2
SWA_HEAD_DIM = 64
SWA_WINDOW = 128
SWA_BLOCK = 128
ROPE_DIM = SWA_HEAD_DIM // 4
ROPE_THETA = 500000.0

DN_HEADS = 4
DN_HEAD_DIM = 128
DN_CONV = 4
DN_CHUNK = 64
DN_QKV = 3 * DN_HEADS * DN_HEAD_DIM

EVEN_SPLITS = (SWA_HEADS * SWA_HEAD_DIM, SWA_KV_HEADS * SWA_HEAD_DIM, SWA_KV_HEADS * SWA_HEAD_DIM,
               DN_HEADS * DN_HEAD_DIM, DN_HEADS * DN_HEAD_DIM, DN_HEADS * DN_HEAD_DIM,
               DN_HEADS * DN_HEAD_DIM, DN_HEADS, DN_HEADS)
EVEN_PROJ = sum(EVEN_SPLITS)
MIX_WIDTH = SWA_HEADS * SWA_HEAD_DIM + DN_HEADS * DN_HEAD_DIM

SSM_D_INNER = 2 * D_MODEL
SSM_HEAD_DIM = 64
SSM_HEADS = SSM_D_INNER // SSM_HEAD_DIM
SSM_GROUPS = 4
SSM_STATE = 128
SSM_CONV = 4
SSM_CHUNK = 128
SSM_CONV_DIM = SSM_D_INNER + 2 * SSM_GROUPS * SSM_STATE
SSM_PROJ = SSM_D_INNER + SSM_CONV_DIM + SSM_HEADS

FFN_DIM = 2816
N_EXPERTS = 8
TOP_K = 2
EXPERT_DIM = 3584

kernel_name = "hybrid_swa_deltanet_mamba2_moe_deepnorm"


def layer_norm(x, g, b):
    xf = x.astype(jnp.float32)
    mu = jnp.mean(xf, axis=-1, keepdims=True)
    var = jnp.mean(jnp.square(xf - mu), axis=-1, keepdims=True)
    return ((xf - mu) * lax.rsqrt(var + LN_EPS) * g + b).astype(x.dtype)


def rms_norm(x, w):
    xf = x.astype(jnp.float32)
    return xf * lax.rsqrt(jnp.mean(jnp.square(xf), axis=-1, keepdims=True) + RMS_EPS) * w


def l2_normalize(x):
    xf = x.astype(jnp.float32)
    return xf * lax.rsqrt(jnp.sum(jnp.square(xf), axis=-1, keepdims=True) + RMS_EPS)


def split_last(a, sizes):
    return jnp.split(a, np.cumsum(sizes)[:-1].tolist(), axis=-1)


def causal_depthwise_conv(x, w):
    K, C = w.shape
    return lax.conv_general_dilated(x, w[:, None, :].astype(x.dtype), window_strides=(1,),
                                    padding=[(K - 1, 0)], dimension_numbers=("NWC", "WIO", "NWC"),
                                    feature_group_count=C)


def partial_rope(x, pos):
    half = ROPE_DIM // 2
    inv_freq = jnp.power(ROPE_THETA, -jnp.arange(half, dtype=jnp.float32) / half)
    ang = pos.astype(jnp.float32)[:, None] * inv_freq[None, :]
    cos = jnp.cos(ang)[None, :, None, :]
    sin = jnp.sin(ang)[None, :, None, :]
    xf = x.astype(jnp.float32)
    x1, x2, rest = xf[..., :half], xf[..., half:ROPE_DIM], xf[..., ROPE_DIM:]
    return jnp.concatenate([x1 * cos - x2 * sin, x2 * cos + x1 * sin, rest], axis=-1).astype(x.dtype)


def sliding_window_sink_attention(q, k, v, sinks):
    Bsz, T, Hq, D = q.shape
    Hkv = k.shape[2]
    G = Hq // Hkv
    Q = SWA_BLOCK
    N = T // Q
    qb = q.reshape(Bsz, N, Q, Hkv, G, D)
    kb = k.reshape(Bsz, N, Q, Hkv, D)
    vb = v.reshape(Bsz, N, Q, Hkv, D)
    prev = lambda a: jnp.concatenate([jnp.zeros_like(a[:, :1]), a[:, :-1]], axis=1)
    kw = jnp.concatenate([prev(kb), kb], axis=2)
    vw = jnp.concatenate([prev(vb), vb], axis=2)
    s = jnp.einsum("bnqhgd,bnkhd->bnhgqk", qb, kw, preferred_element_type=jnp.float32) * (D ** -0.5)
    qi = jnp.arange(Q)[:, None] + Q
    kj = jnp.arange(2 * Q)[None, :]
    band = (kj <= qi) & (qi - kj < SWA_WINDOW)
    has_prev = (jnp.arange(N) > 0)[:, None, None] | (kj >= Q)[None]
    valid = band[None] & has_prev
    s = jnp.where(valid[None, :, None, None], s, -jnp.inf)
    sink = sinks.astype(jnp.float32).reshape(Hkv, G)[None, None, :, :, None, None]
    s_all = jnp.concatenate([s, jnp.broadcast_to(sink, s.shape[:-1] + (1,))], axis=-1)
    p = jax.nn.softmax(s_all, axis=-1)[..., :-1]
    o = jnp.einsum("bnhgqk,bnkhd->bnqhgd", p.astype(v.dtype), vw)
    return o.reshape(Bsz, T, Hq * D)


def gated_delta_rule(q, k, v, g, beta):
    Bsz, T, H, Dk = q.shape
    Dv = v.shape[-1]
    C = DN_CHUNK
    N = T // C
    f32 = jnp.float32

    def chunks(a):
        a = a.astype(f32).reshape((Bsz, N, C, H) + a.shape[3:])
        return jnp.moveaxis(a, 3, 1)

    q = chunks(q) * (Dk ** -0.5)
    k, v, g, beta = chunks(k), chunks(v), chunks(g), chunks(beta)
    gc = jnp.cumsum(g, axis=-1)
    causal = jnp.tril(jnp.ones((C, C), bool))
    strict = jnp.tril(jnp.ones((C, C), bool), -1)
    decay = jnp.exp(jnp.where(causal, gc[..., :, None] - gc[..., None, :], -jnp.inf))
    kb = k * beta[..., None]
    a_lower = jnp.where(strict, jnp.einsum("bhncd,bhnsd->bhncs", kb, k) * decay, 0.0)
    rhs = jnp.concatenate([v * beta[..., None], kb * jnp.exp(gc)[..., None]], axis=-1)
    sol = lax.linalg.triangular_solve(a_lower + jnp.eye(C, dtype=f32), rhs, left_side=True,
                                      lower=True, unit_diagonal=True)
    u, w = sol[..., :Dv], sol[..., Dv:]
    attn = jnp.einsum("bhncd,bhnsd->bhncs", q, k) * decay
    q_dec = q * jnp.exp(gc)[..., None]
    k_dec = k * jnp.exp(gc[..., -1:] - gc)[..., None]
    g_last = jnp.exp(gc[..., -1])

    def step(S, inp):
        qd, kd, uc, wc, ac, gl = inp
        v_new = uc - jnp.einsum("bhck,bhkv->bhcv", wc, S)
        o = jnp.einsum("bhck,bhkv->bhcv", qd, S) + jnp.einsum("bhcs,bhsv->bhcv", ac, v_new)
        S = S * gl[..., None, None] + jnp.einsum("bhck,bhcv->bhkv", kd, v_new)
        return S, o

    lead = lambda a: jnp.moveaxis(a, 2, 0)
    S0 = jnp.zeros((Bsz, H, Dk, Dv), f32)
    _, o = lax.scan(step, S0, (lead(q_dec), lead(k_dec), lead(u), lead(w), lead(attn), lead(g_last)))
    return jnp.transpose(o, (1, 0, 3, 2, 4)).reshape(Bsz, T, H, Dv)


def ssd_chunked(x, dt, a, bmat, cmat):
    Bsz, T, H, P = x.shape
    G, Ns = bmat.shape[2], bmat.shape[3]
    R = H // G
    L = SSM_CHUNK
    Nc = T // L
    f32 = jnp.float32
    xdt = (x.astype(f32) * dt[..., None]).reshape(Bsz, Nc, L, G, R, P)
    dA = (dt * a).reshape(Bsz, Nc, L, G, R).transpose(0, 1, 3, 4, 2)
    acs = jnp.cumsum(dA, axis=-1)
    bc = bmat.astype(f32).reshape(Bsz, Nc, L, G, Ns)
    cc = cmat.astype(f32).reshape(Bsz, Nc, L, G, Ns)
    causal = jnp.tril(jnp.ones((L, L), bool))
    seg = jnp.exp(jnp.where(causal, acs[..., :, None] - acs[..., None, :], -jnp.inf))
    cb = jnp.einsum("bclgn,bcsgn->bcgls", cc, bc)
    y_diag = jnp.einsum("bcgrls,bcsgrp->bclgrp", cb[:, :, :, None] * seg, xdt)
    decay_states = jnp.exp(acs[..., -1:] - acs)
    states = jnp.einsum("bclgn,bcgrl,bclgrp->bcgrpn", bc, decay_states, xdt)
    chunk_decay = jnp.exp(acs[..., -1])

    def step(h, inp):
        st, dec = inp
        return h * dec[..., None, None] + st, h

    h0 = jnp.zeros((Bsz, G, R, P, Ns), f32)
    _, h_in = lax.scan(step, h0, (jnp.moveaxis(states, 1, 0), jnp.moveaxis(chunk_decay, 1, 0)))
    h_in = jnp.moveaxis(h_in, 0, 1)
    y_off = jnp.einsum("bclgn,bcgrpn,bcgrl->bclgrp", cc, h_in, jnp.exp(acs))
    return (y_diag + y_off).reshape(Bsz, T, H, P)


def attn_deltanet_mixer(x, pos, w_in, sinks, conv_w, a_log, dt_bias, norm_w, w_out):
    Bsz, T, _ = x.shape
    f32 = jnp.float32
    aq, ak, av, bq, bk, bv, bgate, bbeta, ba = split_last(x @ w_in, EVEN_SPLITS)
    q = partial_rope(aq.reshape(Bsz, T, SWA_HEADS, SWA_HEAD_DIM), pos)
    k = partial_rope(ak.reshape(Bsz, T, SWA_KV_HEADS, SWA_HEAD_DIM), pos)
    v = av.reshape(Bsz, T, SWA_KV_HEADS, SWA_HEAD_DIM)
    out_a = sliding_window_sink_attention(q, k, v, sinks)
    qkv = jax.nn.silu(causal_depthwise_conv(jnp.concatenate([bq, bk, bv], axis=-1), conv_w))
    dq, dk, dv = jnp.split(qkv, 3, axis=-1)
    shp = (Bsz, T, DN_HEADS, DN_HEAD_DIM)
    beta = jax.nn.sigmoid(bbeta.astype(f32))
    g = -jnp.exp(a_log.astype(f32)) * jax.nn.softplus(ba.astype(f32) + dt_bias)
    o = gated_delta_rule(l2_normalize(dq.reshape(shp)), l2_normalize(dk.reshape(shp)), dv.reshape(shp), g, beta)
    o = rms_norm(o, norm_w) * jax.nn.silu(bgate.reshape(shp).astype(f32))
    out_b = o.reshape(Bsz, T, DN_HEADS * DN_HEAD_DIM).astype(x.dtype)
    return jnp.concatenate([out_a, out_b], axis=-1) @ w_out


def mamba2_mixer(x, w_in, conv_w, conv_b, dt_bias, a_log, d_skip, norm_w, w_out):
    Bsz, T, _ = x.shape
    f32 = jnp.float32
    z, xbc, dt = split_last(x @ w_in, (SSM_D_INNER, SSM_CONV_DIM, SSM_HEADS))
    xbc = jax.nn.silu(causal_depthwise_conv(xbc, conv_w) + conv_b)
    xs, bmat, cmat = split_last(xbc, (SSM_D_INNER, SSM_GROUPS * SSM_STATE, SSM_GROUPS * SSM_STATE))
    xs = xs.reshape(Bsz, T, SSM_HEADS, SSM_HEAD_DIM)
    dt = jax.nn.softplus(dt.astype(f32) + dt_bias)
    y = ssd_chunked(xs, dt, -jnp.exp(a_log.astype(f32)),
                    bmat.reshape(Bsz, T, SSM_GROUPS, SSM_STATE), cmat.reshape(Bsz, T, SSM_GROUPS, SSM_STATE))
    y = y + d_skip.astype(f32)[:, None] * xs.astype(f32)
    y = y.reshape(Bsz, T, SSM_D_INNER) * jax.nn.silu(z.astype(f32))
    y = rms_norm(y.reshape(Bsz, T, SSM_GROUPS, -1), norm_w.reshape(SSM_GROUPS, -1)).reshape(Bsz, T, SSM_D_INNER)
    return y.astype(x.dtype) @ w_out


def swiglu(x, w_gate, w_up, w_down):
    return (jax.nn.silu(x @ w_gate) * (x @ w_up)) @ w_down


def moe_swiglu(x, w_router, b_router, w_gate, w_up, w_down):
    Bsz, T, D = x.shape
    xt = x.reshape(-1, D)
    logits = (xt @ w_router).astype(jnp.float32) + b_router.astype(jnp.float32)
    top_val, top_idx = lax.top_k(logits, TOP_K)
    gates = jax.nn.softmax(top_val, axis=-1)
    combine = jnp.sum(jax.nn.one_hot(top_idx, N_EXPERTS, dtype=jnp.float32) * gates[..., None], axis=1)
    out = jnp.zeros(xt.shape, jnp.float32)
    for e in range(N_EXPERTS):
        out = out + combine[:, e:e + 1] * swiglu(xt, w_gate[e], w_up[e], w_down[e]).astype(jnp.float32)
    return out.reshape(Bsz, T, D).astype(x.dtype)


def setup_inputs(seed: int = 0) -> dict:
    key = jax.random.key(seed)
    ks = jax.random.split(key, 32)
    f32 = jnp.float32
    nrm = lambda i, shape, scale: jax.random.normal(ks[i], shape, f32) * scale

    def dt_bias_init(i, shape):
        dt = jnp.exp(jax.random.uniform(ks[i], shape, f32, np.log(1e-3), np.log(1e-1)))
        return dt + jnp.log(-jnp.expm1(-dt))

    def a_log_init(i, shape):
        return jnp.log(jax.random.uniform(ks[i], shape, f32, 1.0, 16.0))

    b = DEEPNORM_BETA
    return {
        "x": nrm(0, (BATCH, SEQ, D_MODEL), 1.0),
        "ln_g": 1.0 + nrm(1, (DEPTH, 2, D_MODEL), 0.02),
        "ln_b": nrm(2, (DEPTH, 2, D_MODEL), 0.02),
        "even_w_in": nrm(3, (N_EVEN, D_MODEL, EVEN_PROJ), D_MODEL ** -0.5),
        "swa_sinks": nrm(4, (N_EVEN, SWA_HEADS), 1.0),
        "dn_conv_w": nrm(5, (N_EVEN, DN_CONV, DN_QKV), DN_CONV ** -0.5),
        "dn_a_log": a_log_init(6, (N_EVEN, DN_HEADS)),
        "dn_dt_bias": dt_bias_init(7, (N_EVEN, DN_HEADS)),
        "dn_norm_w": 1.0 + nrm(8, (N_EVEN, DN_HEAD_DIM), 0.02),
        "even_w_out": nrm(9, (N_EVEN, MIX_WIDTH, D_MODEL), b * MIX_WIDTH ** -0.5),
        "ssm_w_in": nrm(10, (N_ODD, D_MODEL, SSM_PROJ), D_MODEL ** -0.5),
        "ssm_conv_w": nrm(11, (N_ODD, SSM_CONV, SSM_CONV_DIM), SSM_CONV ** -0.5),
        "ssm_conv_b": nrm(12, (N_ODD, SSM_CONV_DIM), 0.02),
        "ssm_dt_bias": dt_bias_init(13, (N_ODD, SSM_HEADS)),
        "ssm_a_log": a_log_init(14, (N_ODD, SSM_HEADS)),
        "ssm_d": 1.0 + nrm(15, (N_ODD, SSM_HEADS), 0.02),
        "ssm_norm_w": 1.0 + nrm(16, (N_ODD, SSM_D_INNER), 0.02),
        "ssm_w_out": nrm(17, (N_ODD, SSM_D_INNER, D_MODEL), b * SSM_D_INNER ** -0.5),
        "ffn_w_gate": nrm(18, (N_EVEN, D_MODEL, FFN_DIM), D_MODEL ** -0.5),
        "ffn_w_up": nrm(19, (N_EVEN, D_MODEL, FFN_DIM), D_MODEL ** -0.5),
        "ffn_w_down": nrm(20, (N_EVEN, FFN_DIM, D_MODEL), b * FFN_DIM ** -0.5),
        "moe_w_router": nrm(21, (N_ODD, D_MODEL, N_EXPERTS), D_MODEL ** -0.5),
        "moe_b_router": nrm(22, (N_ODD, N_EXPERTS), 0.01),
        "moe_w_gate": nrm(23, (N_ODD, N_EXPERTS, D_MODEL, EXPERT_DIM), D_MODEL ** -0.5),
        "moe_w_up": nrm(24, (N_ODD, N_EXPERTS, D_MODEL, EXPERT_DIM), D_MODEL ** -0.5),
        "moe_w_down": nrm(25, (N_ODD, N_EXPERTS, EXPERT_DIM, D_MODEL), b * EXPERT_DIM ** -0.5),
    }


def reference(x, ln_g, ln_b, even_w_in, swa_sinks, dn_conv_w, dn_a_log, dn_dt_bias, dn_norm_w,
              even_w_out, ssm_w_in, ssm_conv_w, ssm_conv_b, ssm_dt_bias, ssm_a_log, ssm_d,
              ssm_norm_w, ssm_w_out, ffn_w_gate, ffn_w_up, ffn_w_down, moe_w_router, moe_b_router,
              moe_w_gate, moe_w_up, moe_w_down):
    pos = jnp.arange(x.shape[1], dtype=jnp.int32)
    for i in range(DEPTH):
        j = i // 2
        if i % 2 == 0:
            mix = attn_deltanet_mixer(x, pos, even_w_in[j], swa_sinks[j], dn_conv_w[j], dn_a_log[j],
                                      dn_dt_bias[j], dn_norm_w[j], even_w_out[j])
        else:
            mix = mamba2_mixer(x, ssm_w_in[j], ssm_conv_w[j], ssm_conv_b[j], ssm_dt_bias[j],
                               ssm_a_log[j], ssm_d[j], ssm_norm_w[j], ssm_w_out[j])
        x = layer_norm(DEEPNORM_ALPHA * x + mix, ln_g[i, 0], ln_b[i, 0])
        if i % 2 == 0:
            ffn = swiglu(x, ffn_w_gate[j], ffn_w_up[j], ffn_w_down[j])
        else:
            ffn = moe_swiglu(x, moe_w_router[j], moe_b_router[j], moe_w_gate[j], moe_w_up[j], moe_w_down[j])
        x = layer_norm(DEEPNORM_ALPHA * x + ffn, ln_g[i, 1], ln_b[i, 1])
    return x
```

```python
import functools

import numpy as np
import jax
import jax.numpy as jnp
from jax import lax
from jax.experimental import pallas as pl
from jax.experimental.pallas import tpu as pltpu

F32 = jnp.float32
BF16 = jnp.bfloat16
HIGHEST = lax.Precision.HIGHEST

D_MODEL = 1024
DEPTH = 4
ALPHA = (2 * DEPTH) ** 0.25
LN_EPS = 1e-5
RMS_EPS = 1e-6

SWA_HEADS = 8
SWA_KV_HEADS = 2
SWA_HEAD_DIM = 64
SWA_BLOCK = 128
ROPE_DIM = SWA_HEAD_DIM // 4
ROPE_THETA = 500000.0
SWA_Q = SWA_HEADS * SWA_HEAD_DIM
SWA_KV = SWA_KV_HEADS * SWA_HEAD_DIM

DN_HEADS = 4
DN_HEAD_DIM = 128
DN_CONV = 4
DN_CHUNK = 64
DN_W = DN_HEADS * DN_HEAD_DIM

SSM_D_INNER = 2 * D_MODEL
SSM_HEAD_DIM = 64
SSM_HEADS = SSM_D_INNER // SSM_HEAD_DIM
SSM_GROUPS = 4
SSM_STATE = 128
SSM_CHUNK = 128
SSM_GN = SSM_GROUPS * SSM_STATE
SSM_CONV_DIM = SSM_D_INNER + 2 * SSM_GN
SSM_GROUP_W = SSM_D_INNER // SSM_GROUPS

FFN_DIM = 2816
N_EXPERTS = 8
EXPERT_DIM = 3584

LANES = 128
HALO = 8
VMEM_LIMIT = 52 * 1024 * 1024

ROW_TILE = 512
MOE_TILE = 512
MOE_FSPLIT = 2
DN_BLOCK = 256
GATHER_TILE = 256


def _cparams(sem):
    return pltpu.CompilerParams(dimension_semantics=sem, vmem_limit_bytes=VMEM_LIMIT)


def _sigmoid(x):
    return 1.0 / (1.0 + jnp.exp(-x))


def _silu(x):
    return x * _sigmoid(x)


def _softplus(x):
    return jnp.maximum(x, 0.0) + jnp.log(1.0 + jnp.exp(-jnp.abs(x)))


def _layer_norm(y, g, b):
    mu = jnp.mean(y, axis=-1, keepdims=True)
    d = y - mu
    var = jnp.mean(d * d, axis=-1, keepdims=True)
    return d * lax.rsqrt(var + LN_EPS) * g + b


def _dot(a, b, precision=None):
    return jnp.dot(a, b, preferred_element_type=F32, precision=precision)


def _dot_nt(a, b, precision=None):
    return lax.dot_general(a, b, (((1,), (1,)), ((), ())), preferred_element_type=F32, precision=precision)


def _dot_tn(a, b, precision=None):
    return lax.dot_general(a, b, (((0,), (0,)), ((), ())), preferred_element_type=F32, precision=precision)


def _const_spec(shape):
    nd = len(shape)
    return pl.BlockSpec(shape, lambda *_: (0,) * nd)


def _proj_kernel(x_ref, w_ref, wt_ref, wtt_ref, o_ref, t_ref, tt_ref, *, chunk):
    x = x_ref[...]
    xb = x.astype(BF16)
    for c in range(0, o_ref.shape[1], chunk):
        o_ref[:, c:c + chunk] = _dot(xb, w_ref[:, c:c + chunk]).astype(o_ref.dtype)
    t_ref[...] = _dot(x, wt_ref[...], HIGHEST)
    tt_ref[...] = _dot_nt(wtt_ref[...], x, HIGHEST)


def _proj(x, w, wt, wtt):
    n = x.shape[0]
    c = w.shape[1]
    r = wtt.shape[0]
    tm = ROW_TILE
    return pl.pallas_call(
        functools.partial(_proj_kernel, chunk=512),
        grid=(n // tm,),
        in_specs=[pl.BlockSpec((tm, D_MODEL), lambda i: (i, 0)),
                  _const_spec(w.shape), _const_spec(wt.shape), _const_spec(wtt.shape)],
        out_specs=[pl.BlockSpec((tm, c), lambda i: (i, 0)),
                   pl.BlockSpec((tm, LANES), lambda i: (i, 0)),
                   pl.BlockSpec((r, tm), lambda i: (0, i))],
        out_shape=[jax.ShapeDtypeStruct((n, c), BF16),
                   jax.ShapeDtypeStruct((n, LANES), F32),
                   jax.ShapeDtypeStruct((r, n), F32)],
        compiler_params=_cparams(("parallel",)),
        name="proj",
    )(x, w, wt, wtt)


def _mix_out_kernel(*refs, n_in):
    a_refs = refs[:n_in]
    w_ref, x_ref, g_ref, b_ref, o_ref = refs[n_in:]
    a = a_refs[0][...] if n_in == 1 else jnp.concatenate([r[...] for r in a_refs], axis=1)
    y = ALPHA * x_ref[...] + _dot(a, w_ref[...])
    o_ref[...] = _layer_norm(y, g_ref[...], b_ref[...])


def _mix_out(a_list, w, x, g, b):
    n = x.shape[0]
    tm = ROW_TILE
    in_specs = [pl.BlockSpec((tm, a.shape[1]), lambda i: (i, 0)) for a in a_list]
    in_specs += [_const_spec(w.shape), pl.BlockSpec((tm, D_MODEL), lambda i: (i, 0)),
                 _const_spec(g.shape), _const_spec(b.shape)]
    return pl.pallas_call(
        functools.partial(_mix_out_kernel, n_in=len(a_list)),
        grid=(n // tm,),
        in_specs=in_specs,
        out_specs=pl.BlockSpec((tm, D_MODEL), lambda i: (i, 0)),
        out_shape=jax.ShapeDtypeStruct((n, D_MODEL), F32),
        compiler_params=_cparams(("parallel",)),
        name="mix_out",
    )(*a_list, w, x, g, b)


def _swiglu_chunks(xb, wg_ref, wu_ref, wd_ref, chunk):
    f = wg_ref.shape[-1]
    acc = None
    for c in range(0, f, chunk):
        w = min(chunk, f - c)
        hg = _dot(xb, wg_ref[:, c:c + w])
        hu = _dot(xb, wu_ref[:, c:c + w])
        h = (_silu(hg) * hu).astype(BF16)
        part = _dot(h, wd_ref[c:c + w, :])
        acc = part if acc is None else acc + part
    return acc


def _ffn_kernel(x_ref, wg_ref, wu_ref, wd_ref, g_ref, b_ref, o_ref):
    x = x_ref[...]
    y = _swiglu_chunks(x.astype(BF16), wg_ref, wu_ref, wd_ref, 512)
    o_ref[...] = _layer_norm(ALPHA * x + y, g_ref[...], b_ref[...])


def _ffn(x, wg, wu, wd, g, b):
    n = x.shape[0]
    tm = ROW_TILE
    single = pl.Buffered(1)
    return pl.pallas_call(
        _ffn_kernel,
        grid=(n // tm,),
        in_specs=[pl.BlockSpec((tm, D_MODEL), lambda i: (i, 0)),
                  pl.BlockSpec(wg.shape, lambda i: (0, 0), pipeline_mode=single),
                  pl.BlockSpec(wu.shape, lambda i: (0, 0), pipeline_mode=single),
                  pl.BlockSpec(wd.shape, lambda i: (0, 0), pipeline_mode=single),
                  _const_spec(g.shape), _const_spec(b.shape)],
        out_specs=pl.BlockSpec((tm, D_MODEL), lambda i: (i, 0)),
        out_shape=jax.ShapeDtypeStruct((n, D_MODEL), F32),
        compiler_params=_cparams(("parallel",)),
        name="ffn",
    )(x, wg, wu, wd, g, b)


def _experts_kernel(te_ref, nu_ref, xs_ref, wg_ref, wu_ref, wd_ref, o_ref):
    i = pl.program_id(0)
    j = pl.program_id(1)

    @pl.when(i < nu_ref[0])
    def _():
        y = _swiglu_chunks(xs_ref[...].astype(BF16), wg_ref, wu_ref, wd_ref, 256)

        @pl.when(j == 0)
        def _():
            o_ref[...] = y

        @pl.when(j > 0)
        def _():
            o_ref[...] += y

    @pl.when(i >= nu_ref[0])
    def _():
        o_ref[...] = jnp.zeros_like(o_ref)


def _experts(tile_expert, n_used, xs, wg, wu, wd):
    npad = xs.shape[0]
    tm = MOE_TILE
    fb = EXPERT_DIM // MOE_FSPLIT
    grid_spec = pltpu.PrefetchScalarGridSpec(
        num_scalar_prefetch=2,
        grid=(npad // tm, MOE_FSPLIT),
        in_specs=[pl.BlockSpec((tm, D_MODEL), lambda i, j, te, nu: (i, 0)),
                  pl.BlockSpec((None, D_MODEL, fb), lambda i, j, te, nu: (te[i], 0, j)),
                  pl.BlockSpec((None, D_MODEL, fb), lambda i, j, te, nu: (te[i], 0, j)),
                  pl.BlockSpec((None, fb, D_MODEL), lambda i, j, te, nu: (te[i], j, 0))],
        out_specs=pl.BlockSpec((tm, D_MODEL), lambda i, j, te, nu: (i, 0)),
    )
    return pl.pallas_call(
        _experts_kernel,
        grid_spec=grid_spec,
        out_shape=jax.ShapeDtypeStruct((npad, D_MODEL), F32),
        compiler_params=_cparams(("arbitrary", "arbitrary")),
        name="experts",
    )(tile_expert, n_used, xs, wg, wu, wd)


def _router_kernel(x_ref, wr_ref, br_ref, ltri_ref, o_ref, cnt_ref, carry_ref):
    @pl.when(pl.program_id(0) == 0)
    def _():
        carry_ref[...] = jnp.zeros_like(carry_ref)

    logits = _dot(x_ref[...], wr_ref[...], HIGHEST) + br_ref[...]
    lane = lax.broadcasted_iota(jnp.int32, logits.shape, 1)
    m1 = jnp.max(logits, axis=-1, keepdims=True)
    i1 = jnp.min(jnp.where(logits == m1, lane, LANES), axis=-1, keepdims=True)
    rest = jnp.where(lane == i1, -jnp.inf, logits)
    m2 = jnp.max(rest, axis=-1, keepdims=True)
    i2 = jnp.min(jnp.where(rest == m2, lane, LANES), axis=-1, keepdims=True)
    e2 = jnp.exp(m2 - m1)
    g1 = 1.0 / (1.0 + e2)
    g2 = e2 / (1.0 + e2)
    sel1 = lane == i1
    sel2 = lane == i2
    onehot = jnp.where(sel1, 1.0, jnp.where(sel2, 1.0, 0.0))
    before = _dot(ltri_ref[...], onehot.astype(BF16)) + carry_ref[0:1, :]
    r1 = jnp.sum(jnp.where(sel1, before, 0.0), axis=-1, keepdims=True)
    r2 = jnp.sum(jnp.where(sel2, before, 0.0), axis=-1, keepdims=True)
    new_carry = carry_ref[0:1, :] + jnp.sum(onehot, axis=0, keepdims=True)
    carry_ref[...] = jnp.broadcast_to(new_carry, carry_ref.shape)
    cnt_ref[...] = jnp.broadcast_to(new_carry, cnt_ref.shape)
    vals = (i1.astype(F32), i2.astype(F32), r1, r2, g1, g2)
    out = jnp.zeros(logits.shape, F32)
    for k, v in enumerate(vals):
        out = jnp.where(lane == k, v, out)
    o_ref[...] = out


def _router(x, wr, br):
    n = x.shape[0]
    tm = ROW_TILE
    ltri = jnp.asarray(np.tril(np.ones((tm, tm), np.float32), -1), BF16)
    return pl.pallas_call(
        _router_kernel,
        grid=(n // tm,),
        in_specs=[pl.BlockSpec((tm, D_MODEL), lambda i: (i, 0)),
                  _const_spec(wr.shape), _const_spec(br.shape), _const_spec(ltri.shape)],
        out_specs=[pl.BlockSpec((tm, LANES), lambda i: (i, 0)),
                   pl.BlockSpec((HALO, LANES), lambda i: (0, 0))],
        out_shape=[jax.ShapeDtypeStruct((n, LANES), F32),
                   jax.ShapeDtypeStruct((HALO, LANES), F32)],
        scratch_shapes=[pltpu.VMEM((HALO, LANES), F32)],
        compiler_params=_cparams(("arbitrary",)),
        name="router",
    )(x, wr, br, ltri)


def _dispatch_kernel(slot_ref, x_ref, xs_in_ref, xs_ref, sem):
    del xs_in_ref
    tg = x_ref.shape[0]

    def issue(r, carry):
        for k in range(2):
            pltpu.make_async_copy(x_ref.at[pl.ds(r, 1)], xs_ref.at[pl.ds(slot_ref[0, 0, 2 * r + k], 1)],
                                  sem).start()
        return carry

    lax.fori_loop(0, tg, issue, 0)
    for k in range(2):
        pltpu.make_async_copy(x_ref, xs_ref.at[pl.ds(0, tg)], sem).wait()


def _dispatch(x, slots, npad):
    n = x.shape[0]
    tg = GATHER_TILE
    slots3 = slots.reshape(n // tg, 1, 2 * tg)
    zeros = jnp.zeros((npad, D_MODEL), F32)
    return pl.pallas_call(
        _dispatch_kernel,
        grid=(n // tg,),
        in_specs=[pl.BlockSpec((1, 1, 2 * tg), lambda i: (i, 0, 0), memory_space=pltpu.SMEM),
                  pl.BlockSpec((tg, D_MODEL), lambda i: (i, 0)),
                  pl.BlockSpec(memory_space=pl.ANY)],
        out_specs=pl.BlockSpec(memory_space=pl.ANY),
        out_shape=jax.ShapeDtypeStruct((npad, D_MODEL), F32),
        scratch_shapes=[pltpu.SemaphoreType.DMA(())],
        input_output_aliases={2: 0},
        compiler_params=_cparams(("arbitrary",)),
        name="moe_dispatch",
    )(slots3, x, zeros)


def _combine_kernel(slot_ref, ys_ref, x_ref, info_ref, g_ref, b_ref, o_ref, buf_ref, sem):
    tg = x_ref.shape[0]

    def issue(r, carry):
        for k in range(2):
            pltpu.make_async_copy(ys_ref.at[pl.ds(slot_ref[0, 0, 2 * r + k], 1)],
                                  buf_ref.at[k, pl.ds(r, 1)], sem).start()
        return carry

    lax.fori_loop(0, tg, issue, 0)
    for k in range(2):
        pltpu.make_async_copy(ys_ref.at[pl.ds(0, tg)], buf_ref.at[k], sem).wait()
    info = info_ref[...]
    y = info[:, 4:5] * buf_ref[0] + info[:, 5:6] * buf_ref[1]
    o_ref[...] = _layer_norm(ALPHA * x_ref[...] + y, g_ref[...], b_ref[...])


def _combine(ys, slots, x, info, g, b):
    n = x.shape[0]
    tg = GATHER_TILE
    slots3 = slots.reshape(n // tg, 1, 2 * tg)
    return pl.pallas_call(
        _combine_kernel,
        grid=(n // tg,),
        in_specs=[pl.BlockSpec((1, 1, 2 * tg), lambda i: (i, 0, 0), memory_space=pltpu.SMEM),
                  pl.BlockSpec(memory_space=pl.ANY),
                  pl.BlockSpec((tg, D_MODEL), lambda i: (i, 0)),
                  pl.BlockSpec((tg, LANES), lambda i: (i, 0)),
                  _const_spec(g.shape), _const_spec(b.shape)],
        out_specs=pl.BlockSpec((tg, D_MODEL), lambda i: (i, 0)),
        out_shape=jax.ShapeDtypeStruct((n, D_MODEL), F32),
        scratch_shapes=[pltpu.VMEM((2, tg, D_MODEL), F32), pltpu.SemaphoreType.DMA(())],
        compiler_params=_cparams(("arbitrary",)),
        name="moe_combine",
    )(slots3, ys, x, info, g, b)


def _moe(x, wr, br, wg, wu, wd, g, b):
    n = x.shape[0]
    tm = MOE_TILE
    info, counts = _router(x, wr, br)
    experts = info[:, 0:2].astype(jnp.int32)
    ranks = info[:, 2:4].astype(jnp.int32)
    count = counts[0, :N_EXPERTS].astype(jnp.int32)
    tiles = (count + tm - 1) // tm
    tile_end = jnp.cumsum(tiles)
    tile_start = tile_end - tiles
    slots = tile_start[experts] * tm + ranks
    n_tiles = (2 * n) // tm + N_EXPERTS
    n_used = tile_end[-1]
    tile_id = jnp.minimum(jnp.arange(n_tiles, dtype=jnp.int32), n_used - 1)
    tile_expert = jnp.sum((tile_id[:, None] >= tile_end[None, :]).astype(jnp.int32), axis=1)
    xs = _dispatch(x, slots, n_tiles * tm)
    ys = _experts(tile_expert.astype(jnp.int32), n_used.reshape(1).astype(jnp.int32), xs, wg, wu, wd)
    return _combine(ys, slots, x, info, g, b)


def _rope(x, c, sa, sb):
    return x * c + pltpu.roll(x, LANES - ROPE_DIM // 2, 1) * sa + pltpu.roll(x, ROPE_DIM // 2, 1) * sb


def _swa_kernel(sink_ref, q_ref, k_ref, v_ref, c_ref, sa_ref, sb_ref, o_ref, kp_ref, vp_ref):
    n = pl.program_id(1)
    blk = SWA_BLOCK

    @pl.when(n == 0)
    def _():
        kp_ref[...] = jnp.zeros_like(kp_ref)
        vp_ref[...] = jnp.zeros_like(vp_ref)

    c = c_ref[...]
    sa = sa_ref[...]
    sb = sb_ref[...]
    lane = lax.broadcasted_iota(jnp.int32, (blk, LANES), 1)
    low = lane < SWA_HEAD_DIM
    row = lax.broadcasted_iota(jnp.int32, (blk, blk), 0)
    col = lax.broadcasted_iota(jnp.int32, (blk, blk), 1)
    mask_prev = (col > row) & (n > 0)
    mask_cur = col <= row

    k_cur = _rope(k_ref[...].astype(F32), c, sa, sb).astype(BF16)
    v_cur = v_ref[...]
    k_prev = kp_ref[...]
    v_prev = vp_ref[...]
    zero = jnp.zeros((blk, LANES), BF16)

    def halves(t, g):
        swapped = jnp.concatenate([t[:, SWA_HEAD_DIM:], t[:, :SWA_HEAD_DIM]], axis=1)
        src_a, src_b = (t, swapped) if g == 0 else (swapped, t)
        return jnp.where(low, src_a, zero), jnp.where(low, zero, src_b)

    for g in range(SWA_KV_HEADS):
        kc = halves(k_cur, g)
        kp = halves(k_prev, g)
        vc = halves(v_cur, g)
        vp = halves(v_prev, g)
        for pair in range(2 * g, 2 * g + 2):
            q = q_ref[:, pair * LANES:(pair + 1) * LANES].astype(F32)
            q = (_rope(q, c, sa, sb) * (SWA_HEAD_DIM ** -0.5)).astype(BF16)
            out = jnp.zeros((blk, LANES), F32)
            for side in range(2):
                sink = sink_ref[2 * pair + side]
                s_prev = jnp.where(mask_prev, _dot_nt(q, kp[side]), -jnp.inf)
                s_cur = jnp.where(mask_cur, _dot_nt(q, kc[side]), -jnp.inf)
                m = jnp.maximum(jnp.maximum(jnp.max(s_prev, axis=-1, keepdims=True),
                                            jnp.max(s_cur, axis=-1, keepdims=True)), sink)
                p_prev = jnp.exp(s_prev - m)
                p_cur = jnp.exp(s_cur - m)
                denom = (jnp.sum(p_prev, axis=-1, keepdims=True) + jnp.sum(p_cur, axis=-1, keepdims=True)
                         + jnp.exp(sink - m))
                inv = 1.0 / denom
                out = out + _dot((p_prev * inv).astype(BF16), vp[side])
                out = out + _dot((p_cur * inv).astype(BF16), vc[side])
            o_ref[:, pair * LANES:(pair + 1) * LANES] = out.astype(o_ref.dtype)

    kp_ref[...] = k_cur
    vp_ref[...] = v_cur


def _swa(p, sinks, tables, bsz, t, q_col, k_col, v_col):
    nb = t // SWA_BLOCK
    c, sa, sb = tables
    row = lambda b, n: b * nb + n
    return pl.pallas_call(
        _swa_kernel,
        grid=(bsz, nb),
        in_specs=[pl.BlockSpec(memory_space=pltpu.SMEM),
                  pl.BlockSpec((SWA_BLOCK, SWA_Q), lambda b, n: (row(b, n), q_col)),
                  pl.BlockSpec((SWA_BLOCK, SWA_KV), lambda b, n: (row(b, n), k_col)),
                  pl.BlockSpec((SWA_BLOCK, SWA_KV), lambda b, n: (row(b, n), v_col)),
                  pl.BlockSpec((SWA_BLOCK, LANES), lambda b, n: (n, 0)),
                  pl.BlockSpec((SWA_BLOCK, LANES), lambda b, n: (n, 0)),
                  pl.BlockSpec((SWA_BLOCK, LANES), lambda b, n: (n, 0))],
        out_specs=pl.BlockSpec((SWA_BLOCK, SWA_Q), lambda b, n: (row(b, n), 0)),
        out_shape=jax.ShapeDtypeStruct((bsz * t, SWA_Q), BF16),
        scratch_shapes=[pltpu.VMEM((SWA_BLOCK, SWA_KV), BF16), pltpu.VMEM((SWA_BLOCK, SWA_KV), BF16)],
        compiler_params=_cparams(("parallel", "arbitrary")),
        name="swa",
    )(sinks, p, p, p, c, sa, sb)


def _rope_tables(t):
    half = ROPE_DIM // 2
    pos = jnp.arange(t, dtype=jnp.int32)
    inv_freq = jnp.power(ROPE_THETA, -jnp.arange(half, dtype=F32) / half)
    ang = pos.astype(F32)[:, None] * inv_freq[None, :]
    cos = jnp.cos(ang)
    sin = jnp.sin(ang)
    ones = jnp.ones((t, SWA_HEAD_DIM - ROPE_DIM), F32)
    zeros = jnp.zeros((t, SWA_HEAD_DIM - ROPE_DIM), F32)
    zhalf = jnp.zeros((t, half), F32)
    c = jnp.concatenate([cos, cos, ones], axis=1)
    sa = jnp.concatenate([-sin, zhalf, zeros], axis=1)
    sb = jnp.concatenate([zhalf, sin, zeros], axis=1)
    two = lambda a: jnp.concatenate([a, a], axis=1)
    return two(c), two(sa), two(sb)


def _unit_lower_inverse(a, eye, blk_mask, merge_masks):
    d0 = jnp.where(blk_mask, a, 0.0)
    x = eye - d0
    p = _dot(d0, d0, HIGHEST)
    x = x + _dot(x, p, HIGHEST)
    p = _dot(p, p, HIGHEST)
    x = x + _dot(x, p, HIGHEST)
    for m in merge_masks:
        lm = jnp.where(m, a, 0.0)
        x = x - _dot(x, _dot(lm, x, HIGHEST), HIGHEST)
    return x


def _dn_kernel(q_ref, k_ref, v_ref, gate_ref, tail_ref, cw_ref, par_ref, nw_ref, o_ref,
               xpad_ref, qkv_ref, bg_ref, s_ref):
    n = pl.program_id(1)
    tb = q_ref.shape[0]
    ch = DN_CHUNK
    hd = DN_HEAD_DIM

    @pl.when(n == 0)
    def _():
        xpad_ref[0:HALO, :] = jnp.zeros((HALO, 3 * DN_W), F32)
        s_ref[...] = jnp.zeros_like(s_ref)

    xpad_ref[HALO:HALO + tb, 0:DN_W] = q_ref[...].astype(F32)
    xpad_ref[HALO:HALO + tb, DN_W:2 * DN_W] = k_ref[...].astype(F32)
    xpad_ref[HALO:HALO + tb, 2 * DN_W:3 * DN_W] = v_ref[...].astype(F32)
    conv = None
    for kk in range(DN_CONV):
        off = HALO - (DN_CONV - 1) + kk
        term = cw_ref[kk:kk + 1, :] * xpad_ref[off:off + tb, :]
        conv = term if conv is None else conv + term
    xpad_ref[0:HALO, :] = xpad_ref[tb:tb + HALO, :]
    qkv = _silu(conv)
    for h in range(2 * DN_HEADS):
        xh = qkv[:, h * hd:(h + 1) * hd]
        scale = lax.rsqrt(jnp.sum(xh * xh, axis=-1, keepdims=True) + RMS_EPS)
        if h < DN_HEADS:
            scale = scale * (hd ** -0.5)
        qkv_ref[:, h * hd:(h + 1) * hd] = xh * scale
    qkv_ref[:, 2 * DN_W:] = qkv[:, 2 * DN_W:]

    tail = tail_ref[...]
    lane = lax.broadcasted_iota(jnp.int32, tail.shape, 1)
    gval = -jnp.exp(par_ref[0:1, :]) * _softplus(tail + par_ref[1:2, :])
    bg_ref[...] = jnp.where(lane < DN_HEADS, _sigmoid(tail), gval)

    ri = lax.broadcasted_iota(jnp.int32, (ch, ch), 0)
    ci = lax.broadcasted_iota(jnp.int32, (ch, ch), 1)
    causal = ri >= ci
    strict = ri > ci
    eye = jnp.where(ri == ci, 1.0, 0.0).astype(F32)
    ltri = jnp.where(causal, 1.0, 0.0).astype(F32)
    blk_mask = strict & ((ri // 8) == (ci // 8))
    merge_masks = [((ri // (2 * s)) == (ci // (2 * s))) & ((ri % (2 * s)) >= s) & ((ci % (2 * s)) < s)
                   for s in (8, 16, 32)]
    er = lax.broadcasted_iota(jnp.int32, (LANES, LANES), 0)
    ec = lax.broadcasted_iota(jnp.int32, (LANES, LANES), 1)
    eye128 = jnp.where(er == ec, 1.0, 0.0).astype(F32)
    lane_c = lax.broadcasted_iota(jnp.int32, (ch, LANES), 1)

    def chunk_body(cidx, carry):
        r0 = pl.multiple_of(cidx * ch, ch)
        rows = pl.ds(r0, ch)
        bg = bg_ref[rows, :]
        gc_all = _dot(ltri, jnp.where(lane_c >= DN_HEADS, bg, 0.0), HIGHEST)
        gc_t = _dot_nt(eye128, gc_all, HIGHEST)
        for h in range(DN_HEADS):
            gcol = gc_all[:, DN_HEADS + h:DN_HEADS + h + 1]
            grow = gc_t[DN_HEADS + h:DN_HEADS + h + 1, :]
            beta = bg[:, h:h + 1]
            decay = jnp.where(causal, jnp.exp(gcol - grow), 0.0)
            q = qkv_ref[rows, h * hd:(h + 1) * hd]
            k = qkv_ref[rows, DN_W + h * hd:DN_W + (h + 1) * hd]
            v = qkv_ref[rows, 2 * DN_W + h * hd:2 * DN_W + (h + 1) * hd]
            k16 = k.astype(BF16)
            a = jnp.where(strict, _dot_nt(k16, k16) * decay * beta, 0.0)
            tinv = _unit_lower_inverse(a, eye, blk_mask, merge_masks)
            egc = jnp.exp(gcol)
            u = _dot(tinv, v * beta, HIGHEST)
            w = _dot(tinv, k * (beta * egc), HIGHEST)
            attn = _dot_nt(q.astype(BF16), k16) * decay
            glast = gcol[ch - 1:ch, :]
            q_dec = (q * egc).astype(BF16)
            k_dec = (k * jnp.exp(glast - gcol)).astype(BF16)
            s = s_ref[h]
            s16 = s.astype(BF16)
            v_new = u - _dot(w.astype(BF16), s16)
            v_new16 = v_new.astype(BF16)
            o = _dot(q_dec, s16) + _dot(attn.astype(BF16), v_new16)
            s_ref[h] = s * jnp.exp(glast) + _dot_tn(k_dec, v_new16)
            gate = gate_ref[rows, h * hd:(h + 1) * hd].astype(F32)
            on = o * lax.rsqrt(jnp.mean(o * o, axis=-1, keepdims=True) + RMS_EPS) * nw_ref[...]
            o_ref[rows, h * hd:(h + 1) * hd] = (on * _silu(gate)).astype(o_ref.dtype)
        return carry

    lax.fori_loop(0, tb // ch, chunk_body, 0)


def _deltanet(p, tail, conv_w, par, norm_w, bsz, t, cols):
    tb = DN_BLOCK
    nb = t // tb
    row = lambda b, n: b * nb + n
    col_spec = lambda cidx: pl.BlockSpec((tb, DN_W), lambda b, n: (row(b, n), cidx))
    return pl.pallas_call(
        _dn_kernel,
        grid=(bsz, nb),
        in_specs=[col_spec(cols[0]), col_spec(cols[1]), col_spec(cols[2]), col_spec(cols[3]),
                  pl.BlockSpec((tb, LANES), lambda b, n: (row(b, n), 0)),
                  _const_spec(conv_w.shape), _const_spec(par.shape), _const_spec(norm_w.shape)],
        out_specs=pl.BlockSpec((tb, DN_W), lambda b, n: (row(b, n), 0)),
        out_shape=jax.ShapeDtypeStruct((bsz * t, DN_W), BF16),
        scratch_shapes=[pltpu.VMEM((tb + HALO, 3 * DN_W), F32),
                        pltpu.VMEM((tb, 3 * DN_W), F32),
                        pltpu.VMEM((tb, LANES), F32),
                        pltpu.VMEM((DN_HEADS, DN_HEAD_DIM, DN_HEAD_DIM), F32)],
        compiler_params=_cparams(("parallel", "arbitrary")),
        name="deltanet",
    )(p, p, p, p, tail, conv_w, par, norm_w)


def _ssd_kernel(z_ref, xs_ref, bc_ref, dt_ref, dtt_ref, cw_ref, cb_ref, pcol_ref, prow_ref,
                dskip_ref, nw_ref, expand_ref, o_ref, xpad_ref, xc_ref, y_ref, h_ref):
    n = pl.program_id(1)
    L = SSM_CHUNK

    @pl.when(n == 0)
    def _():
        xpad_ref[0:HALO, :] = jnp.zeros((HALO, SSM_CONV_DIM), F32)
        h_ref[...] = jnp.zeros_like(h_ref)

    xpad_ref[HALO:HALO + L, 0:SSM_D_INNER] = xs_ref[...].astype(F32)
    xpad_ref[HALO:HALO + L, SSM_D_INNER:] = bc_ref[...].astype(F32)
    conv = cb_ref[...]
    for kk in range(4):
        off = HALO - 3 + kk
        conv = conv + cw_ref[kk:kk + 1, :] * xpad_ref[off:off + L, :]
    xpad_ref[0:HALO, :] = xpad_ref[L:L + HALO, :]
    xc_ref[...] = _silu(conv)

    ri = lax.broadcasted_iota(jnp.int32, (L, L), 0)
    ci = lax.broadcasted_iota(jnp.int32, (L, L), 1)
    causal = ri >= ci
    ltri = jnp.where(causal, 1.0, 0.0).astype(F32)
    utri = jnp.where(ri <= ci, 1.0, 0.0).astype(F32)

    dt = _softplus(dt_ref[...] + pcol_ref[0:1, :])
    acs = _dot(ltri, dt * pcol_ref[1:2, :], HIGHEST)
    tot = acs[L - 1:L, :]
    f1 = dt * jnp.exp(tot - acs)
    ea = jnp.exp(acs)
    dt_r = _softplus(dtt_ref[...] + prow_ref[0])
    acs_r = _dot(dt_r * prow_ref[1], utri, HIGHEST)

    stacked = jnp.concatenate([dt, f1, ea], axis=0).astype(BF16)
    wide = _dot(stacked, expand_ref[...])
    dt_x = wide[0:L]
    f1_x = wide[L:2 * L]
    ea_x = wide[2 * L:3 * L]
    cd = jnp.broadcast_to(jnp.exp(tot), (HALO, LANES))
    cd_hi = cd.astype(BF16)
    cd_lo = (cd - cd_hi.astype(F32)).astype(BF16)
    cd_x = (_dot(cd_hi, expand_ref[...]) + _dot(cd_lo, expand_ref[...]))[0:1, :]

    xs = xc_ref[:, 0:SSM_D_INNER]
    xdt = xs * dt_x
    xdd = (xs * f1_x).astype(BF16)
    lane = lax.broadcasted_iota(jnp.int32, (L, LANES), 1)
    low = lane < SSM_HEAD_DIM
    for g in range(SSM_GROUPS):
        gl = slice(g * SSM_GROUP_W, (g + 1) * SSM_GROUP_W)
        bm = xc_ref[:, SSM_D_INNER + g * SSM_STATE:SSM_D_INNER + (g + 1) * SSM_STATE].astype(BF16)
        cm = xc_ref[:, SSM_D_INNER + SSM_GN + g * SSM_STATE:
                    SSM_D_INNER + SSM_GN + (g + 1) * SSM_STATE].astype(BF16)
        cb = _dot_nt(cm, bm)
        hg = h_ref[g]
        y_off = _dot(cm, hg.astype(BF16)) * ea_x[:, gl]
        h_ref[g] = hg * cd_x[:, gl] + _dot_tn(bm, xdd[:, gl])
        for j in range(SSM_GROUP_W // LANES):
            lanes = slice(g * SSM_GROUP_W + j * LANES, g * SSM_GROUP_W + (j + 1) * LANES)
            xp = xdt[:, lanes]
            yd = jnp.zeros((L, LANES), F32)
            for side in range(2):
                hidx = (g * SSM_GROUP_W + j * LANES) // SSM_HEAD_DIM + side
                seg = jnp.where(causal, jnp.exp(acs[:, hidx:hidx + 1] - acs_r[hidx:hidx + 1, :]), 0.0)
                m = (cb * seg).astype(BF16)
                keep = low if side == 0 else jnp.logical_not(low)
                yd = yd + _dot(m, jnp.where(keep, xp, 0.0).astype(BF16))
            y_ref[:, lanes] = yd + y_off[:, j * LANES:(j + 1) * LANES] + dskip_ref[:, lanes] * xs[:, lanes]

    y = y_ref[...] * _silu(z_ref[...].astype(F32))
    for g in range(SSM_GROUPS):
        gl = slice(g * SSM_GROUP_W, (g + 1) * SSM_GROUP_W)
        yg = y[:, gl]
        yn = yg * lax.rsqrt(jnp.mean(yg * yg, axis=-1, keepdims=True) + RMS_EPS) * nw_ref[:, gl]
        o_ref[:, gl] = yn.astype(o_ref.dtype)


def _ssd(p, dt, dtt, conv_w, conv_b, pcol, prow, dskip, norm_w, expand, bsz, t):
    L = SSM_CHUNK
    nb = t // L
    row = lambda b, n: b * nb + n
    return pl.pallas_call(
        _ssd_kernel,
        grid=(bsz, nb),
        in_specs=[pl.BlockSpec((L, SSM_D_INNER), lambda b, n: (row(b, n), 0)),
                  pl.BlockSpec((L, SSM_D_INNER), lambda b, n: (row(b, n), 1)),
                  pl.BlockSpec((L, 2 * SSM_GN), lambda b, n: (row(b, n), 2 * SSM_D_INNER // (2 * SSM_GN))),
                  pl.BlockSpec((L, LANES), lambda b, n: (row(b, n), 0)),
                  pl.BlockSpec((SSM_HEADS, L), lambda b, n: (0, row(b, n))),
                  _const_spec(conv_w.shape), _const_spec(conv_b.shape), _const_spec(pcol.shape),
                  _const_spec(prow.shape), _const_spec(dskip.shape), _const_spec(norm_w.shape),
                  _const_spec(expand.shape)],
        out_specs=pl.BlockSpec((L, SSM_D_INNER), lambda b, n: (row(b, n), 0)),
        out_shape=jax.ShapeDtypeStruct((bsz * t, SSM_D_INNER), BF16),
        scratch_shapes=[pltpu.VMEM((L + HALO, SSM_CONV_DIM), F32),
                        pltpu.VMEM((L, SSM_CONV_DIM), F32),
                        pltpu.VMEM((L, SSM_D_INNER), F32),
                        pltpu.VMEM((SSM_GROUPS, SSM_STATE, SSM_GROUP_W), F32)],
        compiler_params=_cparams(("parallel", "arbitrary")),
        name="ssd",
    )(p, p, p, dt, dtt, conv_w, conv_b, pcol, prow, dskip, norm_w, expand)


def _pad_lanes(v, offset=0):
    v = v.astype(F32)
    return jnp.zeros((1, LANES), F32).at[0, offset:offset + v.shape[0]].set(v)


def _even_layer(x, bsz, t, w_in, sinks, conv_w, a_log, dt_bias, norm_w, w_out, ln_g, ln_b,
                ffn_wg, ffn_wu, ffn_wd, rope_tables):
    o = np.cumsum((0, SWA_Q, SWA_KV, SWA_KV, DN_W, DN_W, DN_W, DN_W, DN_HEADS, DN_HEADS))
    seg = lambda i: w_in[:, o[i]:o[i + 1]]
    w_main = jnp.concatenate([seg(0), seg(3), seg(4), seg(5), seg(6), seg(1), seg(2)], axis=1).astype(BF16)
    w_tail = w_in[:, o[7]:o[9]]
    wt = jnp.zeros((D_MODEL, LANES), F32).at[:, :2 * DN_HEADS].set(w_tail)
    p, tail, _ = _proj(x, w_main, wt, w_tail.T)
    out_a = _swa(p, sinks.astype(F32), rope_tables, bsz, t,
                 q_col=0, k_col=(SWA_Q + 4 * DN_W) // SWA_KV, v_col=(SWA_Q + 4 * DN_W) // SWA_KV + 1)
    par = jnp.concatenate([_pad_lanes(a_log, DN_HEADS), _pad_lanes(dt_bias, DN_HEADS)], axis=0)
    out_b = _deltanet(p, tail, conv_w.astype(F32), par, norm_w.reshape(1, DN_HEAD_DIM).astype(F32),
                      bsz, t, cols=(1, 2, 3, 4))
    x = _mix_out([out_a, out_b], w_out.astype(BF16), x, ln_g[0:1], ln_b[0:1])
    return _ffn(x, ffn_wg.astype(BF16), ffn_wu.astype(BF16), ffn_wd.astype(BF16), ln_g[1:2], ln_b[1:2])


def _odd_layer(x, bsz, t, w_in, conv_w, conv_b, dt_bias, a_log, d_skip, norm_w, w_out, ln_g, ln_b,
               w_router, b_router, moe_wg, moe_wu, moe_wd):
    main = SSM_D_INNER + SSM_CONV_DIM
    w_tail = w_in[:, main:]
    wt = jnp.zeros((D_MODEL, LANES), F32).at[:, :SSM_HEADS].set(w_tail)
    p, dt, dtt = _proj(x, w_in[:, :main].astype(BF16), wt, w_tail.T)
    a = -jnp.exp(a_log.astype(F32))
    pcol = jnp.concatenate([_pad_lanes(dt_bias), _pad_lanes(a)], axis=0)
    prow = jnp.stack([jnp.broadcast_to(dt_bias.astype(F32)[:, None], (SSM_HEADS, SSM_CHUNK)),
                      jnp.broadcast_to(a[:, None], (SSM_HEADS, SSM_CHUNK))])
    dskip = jnp.repeat(d_skip.astype(F32), SSM_HEAD_DIM)[None, :]
    expand = (jnp.arange(LANES)[:, None] == (jnp.arange(SSM_D_INNER) // SSM_HEAD_DIM)[None, :]).astype(BF16)
    y = _ssd(p, dt, dtt, conv_w.astype(F32), conv_b.astype(F32)[None, :], pcol, prow, dskip,
             norm_w.astype(F32)[None, :], expand, bsz, t)
    x = _mix_out([y], w_out.astype(BF16), x, ln_g[0:1], ln_b[0:1])
    wr = jnp.zeros((D_MODEL, LANES), F32).at[:, :N_EXPERTS].set(w_router)
    br = jnp.full((1, LANES), -1e30, F32).at[0, :N_EXPERTS].set(b_router.astype(F32))
    return _moe(x, wr, br, moe_wg.astype(BF16), moe_wu.astype(BF16), moe_wd.astype(BF16),
                ln_g[1:2], ln_b[1:2])


def kernel(x, ln_g, ln_b, even_w_in, swa_sinks, dn_conv_w, dn_a_log, dn_dt_bias, dn_norm_w, even_w_out,
           ssm_w_in, ssm_conv_w, ssm_conv_b, ssm_dt_bias, ssm_a_log, ssm_d, ssm_norm_w, ssm_w_out,
           ffn_w_gate, ffn_w_up, ffn_w_down, moe_w_router, moe_b_router, moe_w_gate, moe_w_up, moe_w_down):
    bsz, t, d = x.shape
    h = x.reshape(bsz * t, d)
    rope_tables = _rope_tables(t)
    for i in range(DEPTH):
        j = i // 2
        if i % 2 == 0:
            h = _even_layer(h, bsz, t, even_w_in[j], swa_sinks[j], dn_conv_w[j], dn_a_log[j], dn_dt_bias[j],
                            dn_norm_w[j], even_w_out[j], ln_g[i], ln_b[i],
                            ffn_w_gate[j], ffn_w_up[j], ffn_w_down[j], rope_tables)
        else:
            h = _odd_layer(h, bsz, t, ssm_w_in[j], ssm_conv_w[j], ssm_conv_b[j], ssm_dt_bias[j], ssm_a_log[j],
                           ssm_d[j], ssm_norm_w[j], ssm_w_out[j], ln_g[i], ln_b[i],
                           moe_w_router[j], moe_b_router[j], moe_w_gate[j], moe_w_up[j], moe_w_down[j])
    return h.reshape(bsz, t, d)
```

```python
import functools

import numpy as np
import jax
import jax.numpy as jnp
from jax import lax
from jax.experimental import pallas as pl
from jax.experimental.pallas import tpu as pltpu

F32 = jnp.float32
BF16 = jnp.bfloat16
HIGHEST = lax.Precision.HIGHEST

D_MODEL = 1024
DEPTH = 4
ALPHA = (2 * DEPTH) ** 0.25
LN_EPS = 1e-5
RMS_EPS = 1e-6

SWA_HEADS = 8
SWA_KV_HEADS = 2
SWA_HEAD_DIM = 64
SWA_BLOCK = 128
ROPE_DIM = SWA_HEAD_DIM // 4
ROPE_THETA = 500000.0
SWA_Q = SWA_HEADS * SWA_HEAD_DIM
SWA_KV = SWA_KV_HEADS * SWA_HEAD_DIM

DN_HEADS = 4
DN_HEAD_DIM = 128
DN_CONV = 4
DN_CHUNK = 64
DN_W = DN_HEADS * DN_HEAD_DIM

SSM_D_INNER = 2 * D_MODEL
SSM_HEAD_DIM = 64
SSM_HEADS = SSM_D_INNER // SSM_HEAD_DIM
SSM_GROUPS = 4
SSM_STATE = 128
SSM_CHUNK = 128
SSM_GN = SSM_GROUPS * SSM_STATE
SSM_CONV_DIM = SSM_D_INNER + 2 * SSM_GN
SSM_GROUP_W = SSM_D_INNER // SSM_GROUPS

FFN_DIM = 2816
N_EXPERTS = 8
EXPERT_DIM = 3584

LANES = 128
HALO = 8
VMEM_LIMIT = 52 * 1024 * 1024

ROW_TILE = 512
MOE_TILE = 512
MOE_FSPLIT = 4
DN_BLOCK = 256
DN_INV_PASSES = 3
GATHER_TILE = 256


def _cparams(sem):
    return pltpu.CompilerParams(dimension_semantics=sem, vmem_limit_bytes=VMEM_LIMIT)


def _sigmoid(x):
    return 1.0 / (1.0 + jnp.exp(-x))


def _silu(x):
    return x * _sigmoid(x)


def _softplus(x):
    return jnp.maximum(x, 0.0) + jnp.log(1.0 + jnp.exp(-jnp.abs(x)))


def _layer_norm(y, g, b):
    mu = jnp.mean(y, axis=-1, keepdims=True)
    d = y - mu
    var = jnp.mean(d * d, axis=-1, keepdims=True)
    return d * lax.rsqrt(var + LN_EPS) * g + b


def _dot(a, b, precision=None):
    return jnp.dot(a, b, preferred_element_type=F32, precision=precision)


def _dot_nt(a, b, precision=None):
    return lax.dot_general(a, b, (((1,), (1,)), ((), ())), preferred_element_type=F32, precision=precision)


def _dot_tn(a, b, precision=None):
    return lax.dot_general(a, b, (((0,), (0,)), ((), ())), preferred_element_type=F32, precision=precision)


def _const_spec(shape):
    nd = len(shape)
    return pl.BlockSpec(shape, lambda *_: (0,) * nd)


def _proj_kernel(x_ref, w_ref, wt_ref, wtt_ref, o_ref, t_ref, tt_ref, *, chunk):
    x = x_ref[...]
    xb = x.astype(BF16)
    for c in range(0, o_ref.shape[1], chunk):
        o_ref[:, c:c + chunk] = _dot(xb, w_ref[:, c:c + chunk]).astype(o_ref.dtype)
    t_ref[...] = _dot(x, wt_ref[...], HIGHEST)
    tt_ref[...] = _dot_nt(wtt_ref[...], x, HIGHEST)


def _proj(x, w, wt, wtt):
    n = x.shape[0]
    c = w.shape[1]
    r = wtt.shape[0]
    tm = ROW_TILE
    return pl.pallas_call(
        functools.partial(_proj_kernel, chunk=512),
        grid=(n // tm,),
        in_specs=[pl.BlockSpec((tm, D_MODEL), lambda i: (i, 0)),
                  _const_spec(w.shape), _const_spec(wt.shape), _const_spec(wtt.shape)],
        out_specs=[pl.BlockSpec((tm, c), lambda i: (i, 0)),
                   pl.BlockSpec((tm, LANES), lambda i: (i, 0)),
                   pl.BlockSpec((r, tm), lambda i: (0, i))],
        out_shape=[jax.ShapeDtypeStruct((n, c), BF16),
                   jax.ShapeDtypeStruct((n, LANES), F32),
                   jax.ShapeDtypeStruct((r, n), F32)],
        compiler_params=_cparams(("parallel",)),
        name="proj",
    )(x, w, wt, wtt)


def _mix_out_kernel(*refs, n_in):
    a_refs = refs[:n_in]
    w_ref, x_ref, g_ref, b_ref, o_ref = refs[n_in:]
    a = a_refs[0][...] if n_in == 1 else jnp.concatenate([r[...] for r in a_refs], axis=1)
    y = ALPHA * x_ref[...] + _dot(a, w_ref[...])
    o_ref[...] = _layer_norm(y, g_ref[...], b_ref[...])


def _mix_out(a_list, w, x, g, b):
    n = x.shape[0]
    tm = ROW_TILE
    in_specs = [pl.BlockSpec((tm, a.shape[1]), lambda i: (i, 0)) for a in a_list]
    in_specs += [_const_spec(w.shape), pl.BlockSpec((tm, D_MODEL), lambda i: (i, 0)),
                 _const_spec(g.shape), _const_spec(b.shape)]
    return pl.pallas_call(
        functools.partial(_mix_out_kernel, n_in=len(a_list)),
        grid=(n // tm,),
        in_specs=in_specs,
        out_specs=pl.BlockSpec((tm, D_MODEL), lambda i: (i, 0)),
        out_shape=jax.ShapeDtypeStruct((n, D_MODEL), F32),
        compiler_params=_cparams(("parallel",)),
        name="mix_out",
    )(*a_list, w, x, g, b)


def _swiglu_chunks(xb, wg_ref, wu_ref, wd_ref, chunk):
    f = wg_ref.shape[-1]
    acc = None
    for c in range(0, f, chunk):
        w = min(chunk, f - c)
        hg = _dot(xb, wg_ref[:, c:c + w].astype(BF16))
        hu = _dot(xb, wu_ref[:, c:c + w].astype(BF16))
        h = (_silu(hg) * hu).astype(BF16)
        part = _dot(h, wd_ref[c:c + w, :].astype(BF16))
        acc = part if acc is None else acc + part
    return acc


def _ffn_kernel(x_ref, wg_ref, wu_ref, wd_ref, g_ref, b_ref, o_ref):
    x = x_ref[...]
    y = _swiglu_chunks(x.astype(BF16), wg_ref, wu_ref, wd_ref, 512)
    o_ref[...] = _layer_norm(ALPHA * x + y, g_ref[...], b_ref[...])


def _ffn(x, wg, wu, wd, g, b):
    n = x.shape[0]
    tm = ROW_TILE
    single = pl.Buffered(1)
    return pl.pallas_call(
        _ffn_kernel,
        grid=(n // tm,),
        in_specs=[pl.BlockSpec((tm, D_MODEL), lambda i: (i, 0)),
                  pl.BlockSpec(wg.shape, lambda i: (0, 0), pipeline_mode=single),
                  pl.BlockSpec(wu.shape, lambda i: (0, 0), pipeline_mode=single),
                  pl.BlockSpec(wd.shape, lambda i: (0, 0), pipeline_mode=single),
                  _const_spec(g.shape), _const_spec(b.shape)],
        out_specs=pl.BlockSpec((tm, D_MODEL), lambda i: (i, 0)),
        out_shape=jax.ShapeDtypeStruct((n, D_MODEL), F32),
        compiler_params=_cparams(("parallel",)),
        name="ffn",
    )(x, wg, wu, wd, g, b)


def _experts_kernel(te_ref, nu_ref, xs_ref, wg_ref, wu_ref, wd_ref, o_ref):
    i = pl.program_id(0)
    j = pl.program_id(1)

    @pl.when(i < nu_ref[0])
    def _():
        y = _swiglu_chunks(xs_ref[...].astype(BF16), wg_ref, wu_ref, wd_ref, 512)

        @pl.when(j == 0)
        def _():
            o_ref[...] = y

        @pl.when(j > 0)
        def _():
            o_ref[...] += y

    @pl.when(i >= nu_ref[0])
    def _():
        o_ref[...] = jnp.zeros_like(o_ref)


def _experts(tile_expert, n_used, xs, wg, wu, wd, layer):
    npad = xs.shape[0]
    tm = MOE_TILE
    fb = EXPERT_DIM // MOE_FSPLIT
    last = MOE_FSPLIT - 1

    def fblock(i, j, nu):
        iu = jnp.minimum(i, nu[0] - 1)
        ju = jnp.where(i < nu[0], j, last)
        return ju + (iu % 2) * (last - 2 * ju)

    grid_spec = pltpu.PrefetchScalarGridSpec(
        num_scalar_prefetch=2,
        grid=(npad // tm, MOE_FSPLIT),
        in_specs=[pl.BlockSpec((tm, D_MODEL), lambda i, j, te, nu: (jnp.minimum(i, nu[0] - 1), 0)),
                  pl.BlockSpec((None, None, D_MODEL, fb), lambda i, j, te, nu: (layer, te[i], 0, fblock(i, j, nu))),
                  pl.BlockSpec((None, None, D_MODEL, fb), lambda i, j, te, nu: (layer, te[i], 0, fblock(i, j, nu))),
                  pl.BlockSpec((None, None, fb, D_MODEL), lambda i, j, te, nu: (layer, te[i], fblock(i, j, nu), 0))],
        out_specs=pl.BlockSpec((tm, D_MODEL), lambda i, j, te, nu: (i, 0)),
    )
    return pl.pallas_call(
        _experts_kernel,
        grid_spec=grid_spec,
        out_shape=jax.ShapeDtypeStruct((npad, D_MODEL), F32),
        compiler_params=_cparams(("arbitrary", "arbitrary")),
        name="experts",
    )(tile_expert, n_used, xs, wg, wu, wd)


def _router_kernel(x_ref, wr_ref, br_ref, ltri_ref, o_ref, cnt_ref, carry_ref):
    @pl.when(pl.program_id(0) == 0)
    def _():
        carry_ref[...] = jnp.zeros_like(carry_ref)

    logits = _dot(x_ref[...], wr_ref[...], HIGHEST) + br_ref[...]
    lane = lax.broadcasted_iota(jnp.int32, logits.shape, 1)
    m1 = jnp.max(logits, axis=-1, keepdims=True)
    i1 = jnp.min(jnp.where(logits == m1, lane, LANES), axis=-1, keepdims=True)
    rest = jnp.where(lane == i1, -jnp.inf, logits)
    m2 = jnp.max(rest, axis=-1, keepdims=True)
    i2 = jnp.min(jnp.where(rest == m2, lane, LANES), axis=-1, keepdims=True)
    e2 = jnp.exp(m2 - m1)
    g1 = 1.0 / (1.0 + e2)
    g2 = e2 / (1.0 + e2)
    sel1 = lane == i1
    sel2 = lane == i2
    onehot = jnp.where(sel1, 1.0, jnp.where(sel2, 1.0, 0.0))
    before = _dot(ltri_ref[...], onehot.astype(BF16)) + carry_ref[0:1, :]
    r1 = jnp.sum(jnp.where(sel1, before, 0.0), axis=-1, keepdims=True)
    r2 = jnp.sum(jnp.where(sel2, before, 0.0), axis=-1, keepdims=True)
    new_carry = carry_ref[0:1, :] + jnp.sum(onehot, axis=0, keepdims=True)
    carry_ref[...] = jnp.broadcast_to(new_carry, carry_ref.shape)
    cnt_ref[...] = jnp.broadcast_to(new_carry, cnt_ref.shape)
    vals = (i1.astype(F32), i2.astype(F32), r1, r2, g1, g2)
    out = jnp.zeros(logits.shape, F32)
    for k, v in enumerate(vals):
        out = jnp.where(lane == k, v, out)
    o_ref[...] = out


def _router(x, wr, br):
    n = x.shape[0]
    tm = ROW_TILE
    ltri = jnp.asarray(np.tril(np.ones((tm, tm), np.float32), -1), BF16)
    return pl.pallas_call(
        _router_kernel,
        grid=(n // tm,),
        in_specs=[pl.BlockSpec((tm, D_MODEL), lambda i: (i, 0)),
                  _const_spec(wr.shape), _const_spec(br.shape), _const_spec(ltri.shape)],
        out_specs=[pl.BlockSpec((tm, LANES), lambda i: (i, 0)),
                   pl.BlockSpec((HALO, LANES), lambda i: (0, 0))],
        out_shape=[jax.ShapeDtypeStruct((n, LANES), F32),
                   jax.ShapeDtypeStruct((HALO, LANES), F32)],
        scratch_shapes=[pltpu.VMEM((HALO, LANES), F32)],
        compiler_params=_cparams(("arbitrary",)),
        name="router",
    )(x, wr, br, ltri)


def _dispatch_kernel(slot_ref, x_ref, xs_in_ref, xs_ref, sem):
    del xs_in_ref
    tg = x_ref.shape[0]

    def issue(r, carry):
        for k in range(2):
            pltpu.make_async_copy(x_ref.at[pl.ds(r, 1)], xs_ref.at[pl.ds(slot_ref[0, 0, 2 * r + k], 1)],
                                  sem).start()
        return carry

    lax.fori_loop(0, tg, issue, 0)
    for k in range(2):
        pltpu.make_async_copy(x_ref, xs_ref.at[pl.ds(0, tg)], sem).wait()


def _dispatch(x, slots, npad):
    n = x.shape[0]
    tg = GATHER_TILE
    slots3 = slots.reshape(n // tg, 1, 2 * tg)
    zeros = jnp.zeros((npad, D_MODEL), F32)
    return pl.pallas_call(
        _dispatch_kernel,
        grid=(n // tg,),
        in_specs=[pl.BlockSpec((1, 1, 2 * tg), lambda i: (i, 0, 0), memory_space=pltpu.SMEM),
                  pl.BlockSpec((tg, D_MODEL), lambda i: (i, 0)),
                  pl.BlockSpec(memory_space=pl.ANY)],
        out_specs=pl.BlockSpec(memory_space=pl.ANY),
        out_shape=jax.ShapeDtypeStruct((npad, D_MODEL), F32),
        scratch_shapes=[pltpu.SemaphoreType.DMA(())],
        input_output_aliases={2: 0},
        compiler_params=_cparams(("arbitrary",)),
        name="moe_dispatch",
    )(slots3, x, zeros)


def _combine_kernel(slot_ref, ys_ref, x_ref, info_ref, g_ref, b_ref, o_ref, buf_ref, sem):
    tg = x_ref.shape[0]

    def issue(r, carry):
        for k in range(2):
            pltpu.make_async_copy(ys_ref.at[pl.ds(slot_ref[0, 0, 2 * r + k], 1)],
                                  buf_ref.at[k, pl.ds(r, 1)], sem).start()
        return carry

    lax.fori_loop(0, tg, issue, 0)
    for k in range(2):
        pltpu.make_async_copy(ys_ref.at[pl.ds(0, tg)], buf_ref.at[k], sem).wait()
    info = info_ref[...]
    y = info[:, 4:5] * buf_ref[0] + info[:, 5:6] * buf_ref[1]
    o_ref[...] = _layer_norm(ALPHA * x_ref[...] + y, g_ref[...], b_ref[...])


def _combine(ys, slots, x, info, g, b):
    n = x.shape[0]
    tg = GATHER_TILE
    slots3 = slots.reshape(n // tg, 1, 2 * tg)
    return pl.pallas_call(
        _combine_kernel,
        grid=(n // tg,),
        in_specs=[pl.BlockSpec((1, 1, 2 * tg), lambda i: (i, 0, 0), memory_space=pltpu.SMEM),
                  pl.BlockSpec(memory_space=pl.ANY),
                  pl.BlockSpec((tg, D_MODEL), lambda i: (i, 0)),
                  pl.BlockSpec((tg, LANES), lambda i: (i, 0)),
                  _const_spec(g.shape), _const_spec(b.shape)],
        out_specs=pl.BlockSpec((tg, D_MODEL), lambda i: (i, 0)),
        out_shape=jax.ShapeDtypeStruct((n, D_MODEL), F32),
        scratch_shapes=[pltpu.VMEM((2, tg, D_MODEL), F32), pltpu.SemaphoreType.DMA(())],
        compiler_params=_cparams(("arbitrary",)),
        name="moe_combine",
    )(slots3, ys, x, info, g, b)


def _moe(x, wr, br, wg, wu, wd, layer, g, b):
    n = x.shape[0]
    tm = MOE_TILE
    info, counts = _router(x, wr, br)
    experts = info[:, 0:2].astype(jnp.int32)
    ranks = info[:, 2:4].astype(jnp.int32)
    count = counts[0, :N_EXPERTS].astype(jnp.int32)
    tiles = (count + tm - 1) // tm
    tile_end = jnp.cumsum(tiles)
    tile_start = tile_end - tiles
    slots = tile_start[experts] * tm + ranks
    n_tiles = (2 * n) // tm + N_EXPERTS
    n_used = tile_end[-1]
    tile_id = jnp.minimum(jnp.arange(n_tiles, dtype=jnp.int32), n_used - 1)
    tile_expert = jnp.sum((tile_id[:, None] >= tile_end[None, :]).astype(jnp.int32), axis=1)
    xs = _dispatch(x, slots, n_tiles * tm)
    ys = _experts(tile_expert.astype(jnp.int32), n_used.reshape(1).astype(jnp.int32), xs, wg, wu, wd, layer)
    return _combine(ys, slots, x, info, g, b)


def _rope(x, c, sa, sb):
    return x * c + pltpu.roll(x, LANES - ROPE_DIM // 2, 1) * sa + pltpu.roll(x, ROPE_DIM // 2, 1) * sb


def _swa_kernel(sink_ref, q_ref, k_ref, v_ref, c_ref, sa_ref, sb_ref, o_ref, kp_ref, vp_ref):
    n = pl.program_id(1)
    blk = SWA_BLOCK

    @pl.when(n == 0)
    def _():
        kp_ref[...] = jnp.zeros_like(kp_ref)
        vp_ref[...] = jnp.zeros_like(vp_ref)

    c = c_ref[...]
    sa = sa_ref[...]
    sb = sb_ref[...]
    lane = lax.broadcasted_iota(jnp.int32, (blk, LANES), 1)
    low = lane < SWA_HEAD_DIM
    row = lax.broadcasted_iota(jnp.int32, (blk, blk), 0)
    col = lax.broadcasted_iota(jnp.int32, (blk, blk), 1)
    mask_prev = (col > row) & (n > 0)
    mask_cur = col <= row

    k_cur = _rope(k_ref[...].astype(F32), c, sa, sb).astype(BF16)
    v_cur = v_ref[...]
    k_prev = kp_ref[...]
    v_prev = vp_ref[...]
    zero = jnp.zeros((blk, LANES), BF16)

    def halves(t, g):
        swapped = jnp.concatenate([t[:, SWA_HEAD_DIM:], t[:, :SWA_HEAD_DIM]], axis=1)
        src_a, src_b = (t, swapped) if g == 0 else (swapped, t)
        return jnp.where(low, src_a, zero), jnp.where(low, zero, src_b)

    for g in range(SWA_KV_HEADS):
        kc = halves(k_cur, g)
        kp = halves(k_prev, g)
        vc = halves(v_cur, g)
        vp = halves(v_prev, g)
        for pair in range(2 * g, 2 * g + 2):
            q = q_ref[:, pair * LANES:(pair + 1) * LANES].astype(F32)
            q = (_rope(q, c, sa, sb) * (SWA_HEAD_DIM ** -0.5)).astype(BF16)
            out = jnp.zeros((blk, LANES), F32)
            for side in range(2):
                sink = sink_ref[2 * pair + side]
                s_prev = jnp.where(mask_prev, _dot_nt(q, kp[side]), -jnp.inf)
                s_cur = jnp.where(mask_cur, _dot_nt(q, kc[side]), -jnp.inf)
                m = jnp.maximum(jnp.maximum(jnp.max(s_prev, axis=-1, keepdims=True),
                                            jnp.max(s_cur, axis=-1, keepdims=True)), sink)
                p_prev = jnp.exp(s_prev - m)
                p_cur = jnp.exp(s_cur - m)
                denom = (jnp.sum(p_prev, axis=-1, keepdims=True) + jnp.sum(p_cur, axis=-1, keepdims=True)
                         + jnp.exp(sink - m))
                inv = 1.0 / denom
                out = out + _dot((p_prev * inv).astype(BF16), vp[side])
                out = out + _dot((p_cur * inv).astype(BF16), vc[side])
            o_ref[:, pair * LANES:(pair + 1) * LANES] = out.astype(o_ref.dtype)

    kp_ref[...] = k_cur
    vp_ref[...] = v_cur


def _swa(p, sinks, tables, bsz, t, q_col, k_col, v_col):
    nb = t // SWA_BLOCK
    c, sa, sb = tables
    row = lambda b, n: b * nb + n
    return pl.pallas_call(
        _swa_kernel,
        grid=(bsz, nb),
        in_specs=[pl.BlockSpec(memory_space=pltpu.SMEM),
                  pl.BlockSpec((SWA_BLOCK, SWA_Q), lambda b, n: (row(b, n), q_col)),
                  pl.BlockSpec((SWA_BLOCK, SWA_KV), lambda b, n: (row(b, n), k_col)),
                  pl.BlockSpec((SWA_BLOCK, SWA_KV), lambda b, n: (row(b, n), v_col)),
                  pl.BlockSpec((SWA_BLOCK, LANES), lambda b, n: (n, 0)),
                  pl.BlockSpec((SWA_BLOCK, LANES), lambda b, n: (n, 0)),
                  pl.BlockSpec((SWA_BLOCK, LANES), lambda b, n: (n, 0))],
        out_specs=pl.BlockSpec((SWA_BLOCK, SWA_Q), lambda b, n: (row(b, n), 0)),
        out_shape=jax.ShapeDtypeStruct((bsz * t, SWA_Q), BF16),
        scratch_shapes=[pltpu.VMEM((SWA_BLOCK, SWA_KV), BF16), pltpu.VMEM((SWA_BLOCK, SWA_KV), BF16)],
        compiler_params=_cparams(("parallel", "arbitrary")),
        name="swa",
    )(sinks, p, p, p, c, sa, sb)


def _rope_tables(t):
    half = ROPE_DIM // 2
    pos = jnp.arange(t, dtype=jnp.int32)
    inv_freq = jnp.power(ROPE_THETA, -jnp.arange(half, dtype=F32) / half)
    ang = pos.astype(F32)[:, None] * inv_freq[None, :]
    cos = jnp.cos(ang)
    sin = jnp.sin(ang)
    ones = jnp.ones((t, SWA_HEAD_DIM - ROPE_DIM), F32)
    zeros = jnp.zeros((t, SWA_HEAD_DIM - ROPE_DIM), F32)
    zhalf = jnp.zeros((t, half), F32)
    c = jnp.concatenate([cos, cos, ones], axis=1)
    sa = jnp.concatenate([-sin, zhalf, zeros], axis=1)
    sb = jnp.concatenate([zhalf, sin, zeros], axis=1)
    two = lambda a: jnp.concatenate([a, a], axis=1)
    return two(c), two(sa), two(sb)


def _split(a):
    hi = a.astype(BF16)
    return hi, (a - hi.astype(F32)).astype(BF16)


def _bdot(a, b):
    return lax.dot_general(a, b, (((2,), (1,)), ((0,), (0,))), preferred_element_type=F32)


def _bdot_nt(a, b):
    return lax.dot_general(a, b, (((2,), (2,)), ((0,), (0,))), preferred_element_type=F32)


def _mm(a, b):
    if DN_INV_PASSES == 1:
        return _bdot(a.astype(BF16), b.astype(BF16))
    ah, al = _split(a)
    bh, bl = _split(b)
    return _bdot(ah, bh) + (_bdot(ah, bl) + _bdot(al, bh))


def _unit_lower_inverse(a, eye, blk_mask, merge_masks):
    d0 = jnp.where(blk_mask, a, 0.0)
    x = eye - d0
    p = _mm(d0, d0)
    x = x + _mm(x, p)
    p = _mm(p, p)
    x = x + _mm(x, p)
    for m in merge_masks:
        lm = jnp.where(m, a, 0.0)
        x = x - _mm(x, _mm(lm, x))
    return x


def _dn_kernel(q_ref, k_ref, v_ref, gate_ref, tail_ref, cw_ref, par_ref, nw_ref, o_ref,
               xpad_ref, qkv_ref, bg_ref, s_ref, gc_ref, u_ref, oi_ref, kd_ref, wq_ref):
    n = pl.program_id(1)
    tb = q_ref.shape[0]
    ch = DN_CHUNK
    hd = DN_HEAD_DIM

    @pl.when(n == 0)
    def _():
        xpad_ref[0:HALO, :] = jnp.zeros((HALO, 3 * DN_W), F32)
        s_ref[...] = jnp.zeros_like(s_ref)

    xpad_ref[HALO:HALO + tb, 0:DN_W] = q_ref[...].astype(F32)
    xpad_ref[HALO:HALO + tb, DN_W:2 * DN_W] = k_ref[...].astype(F32)
    xpad_ref[HALO:HALO + tb, 2 * DN_W:3 * DN_W] = v_ref[...].astype(F32)
    conv = None
    for kk in range(DN_CONV):
        off = HALO - (DN_CONV - 1) + kk
        term = cw_ref[kk:kk + 1, :] * xpad_ref[off:off + tb, :]
        conv = term if conv is None else conv + term
    xpad_ref[0:HALO, :] = xpad_ref[tb:tb + HALO, :]
    qkv = _silu(conv)
    for h in range(2 * DN_HEADS):
        xh = qkv[:, h * hd:(h + 1) * hd]
        scale = lax.rsqrt(jnp.sum(xh * xh, axis=-1, keepdims=True) + RMS_EPS)
        if h < DN_HEADS:
            scale = scale * (hd ** -0.5)
        qkv_ref[:, h * hd:(h + 1) * hd] = xh * scale
    qkv_ref[:, 2 * DN_W:] = qkv[:, 2 * DN_W:]

    tail = tail_ref[...]
    lane = lax.broadcasted_iota(jnp.int32, tail.shape, 1)
    gval = -jnp.exp(par_ref[0:1, :]) * _softplus(tail + par_ref[1:2, :])
    bg_ref[...] = jnp.where(lane < DN_HEADS, _sigmoid(tail), gval)

    rt = lax.broadcasted_iota(jnp.int32, (tb, tb), 0)
    ct = lax.broadcasted_iota(jnp.int32, (tb, tb), 1)
    ltri = jnp.where((rt >= ct) & ((rt // ch) == (ct // ch)), 1.0, 0.0).astype(F32)
    er = lax.broadcasted_iota(jnp.int32, (LANES, LANES), 0)
    ec = lax.broadcasted_iota(jnp.int32, (LANES, LANES), 1)
    eye = jnp.where(er == ec, 1.0, 0.0).astype(F32)
    bg = bg_ref[...]
    gc_all = _dot(ltri, jnp.where(lane >= DN_HEADS, bg, 0.0), HIGHEST)
    gc_t = _dot_nt(eye, gc_all, HIGHEST)
    gc_ref[...] = gc_all

    sb = 2 * ch
    same = (er // ch) == (ec // ch)
    causal = same & (er >= ec)
    strict = same & (er > ec)
    blk_mask = (er > ec) & ((er // 8) == (ec // 8))
    merge_masks = [((er // (2 * s)) == (ec // (2 * s))) & ((er % (2 * s)) >= s) & ((ec % (2 * s)) < s)
                   for s in (8, 16, 32)]
    first = lax.broadcasted_iota(jnp.int32, (sb, 1), 0) < ch
    probs = [(h, b2) for h in range(DN_HEADS) for b2 in range(tb // sb)]
    rows_of = lambda b2: slice(b2 * sb, (b2 + 1) * sb)
    stack = lambda f: jnp.stack([f(h, b2) for h, b2 in probs])
    gcol = stack(lambda h, b2: gc_all[rows_of(b2), DN_HEADS + h:DN_HEADS + h + 1])
    grow = stack(lambda h, b2: gc_t[DN_HEADS + h:DN_HEADS + h + 1, rows_of(b2)])
    beta = stack(lambda h, b2: bg[rows_of(b2), h:h + 1])
    q = stack(lambda h, b2: qkv_ref[rows_of(b2), h * hd:(h + 1) * hd])
    k = stack(lambda h, b2: qkv_ref[rows_of(b2), DN_W + h * hd:DN_W + (h + 1) * hd])
    v = stack(lambda h, b2: qkv_ref[rows_of(b2), 2 * DN_W + h * hd:2 * DN_W + (h + 1) * hd])
    decay = jnp.where(causal, jnp.exp(gcol - grow), 0.0)
    k16 = k.astype(BF16)
    a = jnp.where(strict, _bdot_nt(k16, k16) * decay * beta, 0.0)
    tinv = _unit_lower_inverse(a, eye, blk_mask, merge_masks)
    egc = jnp.exp(gcol)
    u = _mm(tinv, v * beta)
    w16 = _mm(tinv, k * (beta * egc)).astype(BF16)
    attn16 = (_bdot_nt(q.astype(BF16), k16) * decay).astype(BF16)
    q_eff16 = (q * egc - _bdot(attn16, w16)).astype(BF16)
    o_intra = _bdot(attn16, u.astype(BF16))
    glast = jnp.where(first, gcol[:, ch - 1:ch, :], gcol[:, sb - 1:sb, :])
    k_dec16 = (k * jnp.exp(glast - gcol)).astype(BF16)
    for g, (h, b2) in enumerate(probs):
        u_ref[h, rows_of(b2), :] = u[g]
        oi_ref[h, rows_of(b2), :] = o_intra[g]
        kd_ref[h, rows_of(b2), :] = k_dec16[g]
        for c in range(2):
            wq_ref[h, 2 * b2 + c, 0:ch, :] = w16[g, c * ch:(c + 1) * ch]
            wq_ref[h, 2 * b2 + c, ch:sb, :] = q_eff16[g, c * ch:(c + 1) * ch]

    for c in range(tb // ch):
        rows = slice(c * ch, (c + 1) * ch)
        s = s_ref[...]
        r = _bdot(wq_ref[:, c], s.astype(BF16))
        v_new16 = (u_ref[:, rows, :] - r[:, 0:ch]).astype(BF16)
        for h in range(DN_HEADS):
            glast = gc_ref[(c + 1) * ch - 1:(c + 1) * ch, DN_HEADS + h:DN_HEADS + h + 1]
            s_ref[h] = s[h] * jnp.exp(glast) + _dot_tn(kd_ref[h, rows, :], v_new16[h])
        for h in range(DN_HEADS):
            o = r[h, ch:sb] + oi_ref[h, rows, :]
            gate = gate_ref[rows, h * hd:(h + 1) * hd].astype(F32)
            on = o * lax.rsqrt(jnp.mean(o * o, axis=-1, keepdims=True) + RMS_EPS) * nw_ref[...]
            o_ref[rows, h * hd:(h + 1) * hd] = (on * _silu(gate)).astype(o_ref.dtype)


def _deltanet(p, tail, conv_w, par, norm_w, bsz, t, cols):
    tb = DN_BLOCK
    nb = t // tb
    row = lambda b, n: b * nb + n
    col_spec = lambda cidx: pl.BlockSpec((tb, DN_W), lambda b, n: (row(b, n), cidx))
    return pl.pallas_call(
        _dn_kernel,
        grid=(bsz, nb),
        in_specs=[col_spec(cols[0]), col_spec(cols[1]), col_spec(cols[2]), col_spec(cols[3]),
                  pl.BlockSpec((tb, LANES), lambda b, n: (row(b, n), 0)),
                  _const_spec(conv_w.shape), _const_spec(par.shape), _const_spec(norm_w.shape)],
        out_specs=pl.BlockSpec((tb, DN_W), lambda b, n: (row(b, n), 0)),
        out_shape=jax.ShapeDtypeStruct((bsz * t, DN_W), BF16),
        scratch_shapes=[pltpu.VMEM((tb + HALO, 3 * DN_W), F32),
                        pltpu.VMEM((tb, 3 * DN_W), F32),
                        pltpu.VMEM((tb, LANES), F32),
                        pltpu.VMEM((DN_HEADS, DN_HEAD_DIM, DN_HEAD_DIM), F32),
                        pltpu.VMEM((tb, LANES), F32),
                        pltpu.VMEM((DN_HEADS, tb, DN_HEAD_DIM), F32),
                        pltpu.VMEM((DN_HEADS, tb, DN_HEAD_DIM), F32),
                        pltpu.VMEM((DN_HEADS, tb, DN_HEAD_DIM), BF16),
                        pltpu.VMEM((DN_HEADS, tb // DN_CHUNK, 2 * DN_CHUNK, DN_HEAD_DIM), BF16)],
        compiler_params=_cparams(("parallel", "arbitrary")),
        name="deltanet",
    )(p, p, p, p, tail, conv_w, par, norm_w)


def _ssd_kernel(z_ref, xs_ref, bc_ref, dt_ref, dtt_ref, cw_ref, cb_ref, pcol_ref, prow_ref,
                dskip_ref, nw_ref, expand_ref, o_ref, xpad_ref, xc_ref, y_ref, h_ref):
    n = pl.program_id(1)
    L = SSM_CHUNK

    @pl.when(n == 0)
    def _():
        xpad_ref[0:HALO, :] = jnp.zeros((HALO, SSM_CONV_DIM), F32)
        h_ref[...] = jnp.zeros_like(h_ref)

    xpad_ref[HALO:HALO + L, 0:SSM_D_INNER] = xs_ref[...].astype(F32)
    xpad_ref[HALO:HALO + L, SSM_D_INNER:] = bc_ref[...].astype(F32)
    conv = cb_ref[...]
    for kk in range(4):
        off = HALO - 3 + kk
        conv = conv + cw_ref[kk:kk + 1, :] * xpad_ref[off:off + L, :]
    xpad_ref[0:HALO, :] = xpad_ref[L:L + HALO, :]
    xc_ref[...] = _silu(conv)

    ri = lax.broadcasted_iota(jnp.int32, (L, L), 0)
    ci = lax.broadcasted_iota(jnp.int32, (L, L), 1)
    causal = ri >= ci
    ltri = jnp.where(causal, 1.0, 0.0).astype(F32)
    utri = jnp.where(ri <= ci, 1.0, 0.0).astype(F32)

    dt = _softplus(dt_ref[...] + pcol_ref[0:1, :])
    acs = _dot(ltri, dt * pcol_ref[1:2, :], HIGHEST)
    tot = acs[L - 1:L, :]
    f1 = dt * jnp.exp(tot - acs)
    ea = jnp.exp(acs)
    dt_r = _softplus(dtt_ref[...] + prow_ref[0])
    acs_r = _dot(dt_r * prow_ref[1], utri, HIGHEST)

    stacked = jnp.concatenate([dt, f1, ea], axis=0).astype(BF16)
    wide = _dot(stacked, expand_ref[...])
    dt_x = wide[0:L]
    f1_x = wide[L:2 * L]
    ea_x = wide[2 * L:3 * L]
    cd = jnp.broadcast_to(jnp.exp(tot), (HALO, LANES))
    cd_hi = cd.astype(BF16)
    cd_lo = (cd - cd_hi.astype(F32)).astype(BF16)
    cd_x = (_dot(cd_hi, expand_ref[...]) + _dot(cd_lo, expand_ref[...]))[0:1, :]

    xs = xc_ref[:, 0:SSM_D_INNER]
    xdt = xs * dt_x
    xdd = (xs * f1_x).astype(BF16)
    lane = lax.broadcasted_iota(jnp.int32, (L, LANES), 1)
    low = lane < SSM_HEAD_DIM
    for g in range(SSM_GROUPS):
        gl = slice(g * SSM_GROUP_W, (g + 1) * SSM_GROUP_W)
        bm = xc_ref[:, SSM_D_INNER + g * SSM_STATE:SSM_D_INNER + (g + 1) * SSM_STATE].astype(BF16)
        cm = xc_ref[:, SSM_D_INNER + SSM_GN + g * SSM_STATE:
                    SSM_D_INNER + SSM_GN + (g + 1) * SSM_STATE].astype(BF16)
        cb = _dot_nt(cm, bm)
        hg = h_ref[g]
        y_off = _dot(cm, hg.astype(BF16)) * ea_x[:, gl]
        h_ref[g] = hg * cd_x[:, gl] + _dot_tn(bm, xdd[:, gl])
        for j in range(SSM_GROUP_W // LANES):
            lanes = slice(g * SSM_GROUP_W + j * LANES, g * SSM_GROUP_W + (j + 1) * LANES)
            xp = xdt[:, lanes]
            yd = jnp.zeros((L, LANES), F32)
            for side in range(2):
                hidx = (g * SSM_GROUP_W + j * LANES) // SSM_HEAD_DIM + side
                seg = jnp.where(causal, jnp.exp(acs[:, hidx:hidx + 1] - acs_r[hidx:hidx + 1, :]), 0.0)
                m = (cb * seg).astype(BF16)
                keep = low if side == 0 else jnp.logical_not(low)
                yd = yd + _dot(m, jnp.where(keep, xp, 0.0).astype(BF16))
            y_ref[:, lanes] = yd + y_off[:, j * LANES:(j + 1) * LANES] + dskip_ref[:, lanes] * xs[:, lanes]

    y = y_ref[...] * _silu(z_ref[...].astype(F32))
    for g in range(SSM_GROUPS):
        gl = slice(g * SSM_GROUP_W, (g + 1) * SSM_GROUP_W)
        yg = y[:, gl]
        yn = yg * lax.rsqrt(jnp.mean(yg * yg, axis=-1, keepdims=True) + RMS_EPS) * nw_ref[:, gl]
        o_ref[:, gl] = yn.astype(o_ref.dtype)


def _ssd(p, dt, dtt, conv_w, conv_b, pcol, prow, dskip, norm_w, expand, bsz, t):
    L = SSM_CHUNK
    nb = t // L
    row = lambda b, n: b * nb + n
    return pl.pallas_call(
        _ssd_kernel,
        grid=(bsz, nb),
        in_specs=[pl.BlockSpec((L, SSM_D_INNER), lambda b, n: (row(b, n), 0)),
                  pl.BlockSpec((L, SSM_D_INNER), lambda b, n: (row(b, n), 1)),
                  pl.BlockSpec((L, 2 * SSM_GN), lambda b, n: (row(b, n), 2 * SSM_D_INNER // (2 * SSM_GN))),
                  pl.BlockSpec((L, LANES), lambda b, n: (row(b, n), 0)),
                  pl.BlockSpec((SSM_HEADS, L), lambda b, n: (0, row(b, n))),
                  _const_spec(conv_w.shape), _const_spec(conv_b.shape), _const_spec(pcol.shape),
                  _const_spec(prow.shape), _const_spec(dskip.shape), _const_spec(norm_w.shape),
                  _const_spec(expand.shape)],
        out_specs=pl.BlockSpec((L, SSM_D_INNER), lambda b, n: (row(b, n), 0)),
        out_shape=jax.ShapeDtypeStruct((bsz * t, SSM_D_INNER), BF16),
        scratch_shapes=[pltpu.VMEM((L + HALO, SSM_CONV_DIM), F32),
                        pltpu.VMEM((L, SSM_CONV_DIM), F32),
                        pltpu.VMEM((L, SSM_D_INNER), F32),
                        pltpu.VMEM((SSM_GROUPS, SSM_STATE, SSM_GROUP_W), F32)],
        compiler_params=_cparams(("parallel", "arbitrary")),
        name="ssd",
    )(p, p, p, dt, dtt, conv_w, conv_b, pcol, prow, dskip, norm_w, expand)


def _pad_lanes(v, offset=0):
    v = v.astype(F32)
    return jnp.zeros((1, LANES), F32).at[0, offset:offset + v.shape[0]].set(v)


def _even_layer(x, bsz, t, w_in, sinks, conv_w, a_log, dt_bias, norm_w, w_out, ln_g, ln_b,
                ffn_wg, ffn_wu, ffn_wd, rope_tables):
    o = np.cumsum((0, SWA_Q, SWA_KV, SWA_KV, DN_W, DN_W, DN_W, DN_W, DN_HEADS, DN_HEADS))
    seg = lambda i: w_in[:, o[i]:o[i + 1]]
    w_main = jnp.concatenate([seg(0), seg(3), seg(4), seg(5), seg(6), seg(1), seg(2)], axis=1).astype(BF16)
    w_tail = w_in[:, o[7]:o[9]]
    wt = jnp.zeros((D_MODEL, LANES), F32).at[:, :2 * DN_HEADS].set(w_tail)
    p, tail, _ = _proj(x, w_main, wt, w_tail.T)
    out_a = _swa(p, sinks.astype(F32), rope_tables, bsz, t,
                 q_col=0, k_col=(SWA_Q + 4 * DN_W) // SWA_KV, v_col=(SWA_Q + 4 * DN_W) // SWA_KV + 1)
    par = jnp.concatenate([_pad_lanes(a_log, DN_HEADS), _pad_lanes(dt_bias, DN_HEADS)], axis=0)
    out_b = _deltanet(p, tail, conv_w.astype(F32), par, norm_w.reshape(1, DN_HEAD_DIM).astype(F32),
                      bsz, t, cols=(1, 2, 3, 4))
    x = _mix_out([out_a, out_b], w_out.astype(BF16), x, ln_g[0:1], ln_b[0:1])
    return _ffn(x, ffn_wg.astype(BF16), ffn_wu.astype(BF16), ffn_wd.astype(BF16), ln_g[1:2], ln_b[1:2])


def _odd_layer(x, bsz, t, w_in, conv_w, conv_b, dt_bias, a_log, d_skip, norm_w, w_out, ln_g, ln_b,
               w_router, b_router, moe_wg, moe_wu, moe_wd, moe_layer):
    main = SSM_D_INNER + SSM_CONV_DIM
    w_tail = w_in[:, main:]
    wt = jnp.zeros((D_MODEL, LANES), F32).at[:, :SSM_HEADS].set(w_tail)
    p, dt, dtt = _proj(x, w_in[:, :main].astype(BF16), wt, w_tail.T)
    a = -jnp.exp(a_log.astype(F32))
    pcol = jnp.concatenate([_pad_lanes(dt_bias), _pad_lanes(a)], axis=0)
    prow = jnp.stack([jnp.broadcast_to(dt_bias.astype(F32)[:, None], (SSM_HEADS, SSM_CHUNK)),
                      jnp.broadcast_to(a[:, None], (SSM_HEADS, SSM_CHUNK))])
    dskip = jnp.repeat(d_skip.astype(F32), SSM_HEAD_DIM)[None, :]
    expand = (jnp.arange(LANES)[:, None] == (jnp.arange(SSM_D_INNER) // SSM_HEAD_DIM)[None, :]).astype(BF16)
    y = _ssd(p, dt, dtt, conv_w.astype(F32), conv_b.astype(F32)[None, :], pcol, prow, dskip,
             norm_w.astype(F32)[None, :], expand, bsz, t)
    x = _mix_out([y], w_out.astype(BF16), x, ln_g[0:1], ln_b[0:1])
    wr = jnp.zeros((D_MODEL, LANES), F32).at[:, :N_EXPERTS].set(w_router)
    br = jnp.full((1, LANES), -1e30, F32).at[0, :N_EXPERTS].set(b_router.astype(F32))
    return _moe(x, wr, br, moe_wg, moe_wu, moe_wd, moe_layer, ln_g[1:2], ln_b[1:2])


def kernel(x, ln_g, ln_b, even_w_in, swa_sinks, dn_conv_w, dn_a_log, dn_dt_bias, dn_norm_w, even_w_out,
           ssm_w_in, ssm_conv_w, ssm_conv_b, ssm_dt_bias, ssm_a_log, ssm_d, ssm_norm_w, ssm_w_out,
           ffn_w_gate, ffn_w_up, ffn_w_down, moe_w_router, moe_b_router, moe_w_gate, moe_w_up, moe_w_down):
    bsz, t, d = x.shape
    h = x.reshape(bsz * t, d)
    rope_tables = _rope_tables(t)
    for i in range(DEPTH):
        j = i // 2
        if i % 2 == 0:
            h = _even_layer(h, bsz, t, even_w_in[j], swa_sinks[j], dn_conv_w[j], dn_a_log[j], dn_dt_bias[j],
                            dn_norm_w[j], even_w_out[j], ln_g[i], ln_b[i],
                            ffn_w_gate[j], ffn_w_up[j], ffn_w_down[j], rope_tables)
        else:
            h = _odd_layer(h, bsz, t, ssm_w_in[j], ssm_conv_w[j], ssm_conv_b[j], ssm_dt_bias[j], ssm_a_log[j],
                           ssm_d[j], ssm_norm_w[j], ssm_w_out[j], ln_g[i], ln_b[i],
                           moe_w_router[j], moe_b_router[j], moe_w_gate, moe_w_up, moe_w_down, j)
    return h.reshape(bsz, t, d)
```

```python
import functools

import numpy as np
import jax
import jax.numpy as jnp
from jax import lax
from jax.experimental import pallas as pl
from jax.experimental.pallas import tpu as pltpu

F32 = jnp.float32
BF16 = jnp.bfloat16
HIGHEST = lax.Precision.HIGHEST

D_MODEL = 1024
DEPTH = 4
ALPHA = (2 * DEPTH) ** 0.25
LN_EPS = 1e-5
RMS_EPS = 1e-6

SWA_HEADS = 8
SWA_KV_HEADS = 2
SWA_HEAD_DIM = 64
SWA_BLOCK = 128
ROPE_DIM = SWA_HEAD_DIM // 4
ROPE_THETA = 500000.0
SWA_Q = SWA_HEADS * SWA_HEAD_DIM
SWA_KV = SWA_KV_HEADS * SWA_HEAD_DIM

DN_HEADS = 4
DN_HEAD_DIM = 128
DN_CONV = 4
DN_CHUNK = 64
DN_W = DN_HEADS * DN_HEAD_DIM

SSM_D_INNER = 2 * D_MODEL
SSM_HEAD_DIM = 64
SSM_HEADS = SSM_D_INNER // SSM_HEAD_DIM
SSM_GROUPS = 4
SSM_STATE = 128
SSM_CHUNK = 128
SSM_GN = SSM_GROUPS * SSM_STATE
SSM_CONV_DIM = SSM_D_INNER + 2 * SSM_GN
SSM_GROUP_W = SSM_D_INNER // SSM_GROUPS

FFN_DIM = 2816
N_EXPERTS = 8
EXPERT_DIM = 3584

LANES = 128
HALO = 8
VMEM_LIMIT = 52 * 1024 * 1024

ROW_TILE = 512
MOE_TILE = 1024
MOE_SUBTILE = 512
MOE_FSPLIT = 7
DN_BLOCK = 256
DN_INV_PASSES = 1
GATHER_TILE = 256


def _cparams(sem):
    return pltpu.CompilerParams(dimension_semantics=sem, vmem_limit_bytes=VMEM_LIMIT)


def _sigmoid(x):
    return 1.0 / (1.0 + jnp.exp(-x))


def _silu(x):
    return x * _sigmoid(x)


def _softplus(x):
    return jnp.maximum(x, 0.0) + jnp.log(1.0 + jnp.exp(-jnp.abs(x)))


def _layer_norm(y, g, b):
    mu = jnp.mean(y, axis=-1, keepdims=True)
    d = y - mu
    var = jnp.mean(d * d, axis=-1, keepdims=True)
    return d * lax.rsqrt(var + LN_EPS) * g + b


def _dot(a, b, precision=None):
    return jnp.dot(a, b, preferred_element_type=F32, precision=precision)


def _dot_nt(a, b, precision=None):
    return lax.dot_general(a, b, (((1,), (1,)), ((), ())), preferred_element_type=F32, precision=precision)


def _dot_tn(a, b, precision=None):
    return lax.dot_general(a, b, (((0,), (0,)), ((), ())), preferred_element_type=F32, precision=precision)


def _const_spec(shape):
    nd = len(shape)
    return pl.BlockSpec(shape, lambda *_: (0,) * nd)


def _proj_kernel(x_ref, w_ref, wt_ref, o_ref, t_ref, *, chunk):
    x = x_ref[...]
    xb, xl = _split(x)
    for c in range(0, o_ref.shape[1], chunk):
        o_ref[:, c:c + chunk] = _dot(xb, w_ref[:, c:c + chunk]).astype(o_ref.dtype)
    wth, wtl = _split(wt_ref[...])
    t_ref[...] = _dot(xb, wth) + (_dot(xb, wtl) + _dot(xl, wth))


def _proj(x, w, wt):
    n = x.shape[0]
    c = w.shape[1]
    tm = ROW_TILE
    return pl.pallas_call(
        functools.partial(_proj_kernel, chunk=512),
        grid=(n // tm,),
        in_specs=[pl.BlockSpec((tm, D_MODEL), lambda i: (i, 0)),
                  _const_spec(w.shape), _const_spec(wt.shape)],
        out_specs=[pl.BlockSpec((tm, c), lambda i: (i, 0)),
                   pl.BlockSpec((tm, LANES), lambda i: (i, 0))],
        out_shape=[jax.ShapeDtypeStruct((n, c), BF16),
                   jax.ShapeDtypeStruct((n, LANES), F32)],
        compiler_params=_cparams(("parallel",)),
        name="proj",
    )(x, w, wt)


def _mix_out_kernel(*refs, n_in):
    a_refs = refs[:n_in]
    w_ref, x_ref, g_ref, b_ref, o_ref = refs[n_in:]
    a = a_refs[0][...] if n_in == 1 else jnp.concatenate([r[...] for r in a_refs], axis=1)
    y = ALPHA * x_ref[...] + _dot(a, w_ref[...])
    o_ref[...] = _layer_norm(y, g_ref[...], b_ref[...])


def _mix_out(a_list, w, x, g, b):
    n = x.shape[0]
    tm = ROW_TILE
    in_specs = [pl.BlockSpec((tm, a.shape[1]), lambda i: (i, 0)) for a in a_list]
    in_specs += [_const_spec(w.shape), pl.BlockSpec((tm, D_MODEL), lambda i: (i, 0)),
                 _const_spec(g.shape), _const_spec(b.shape)]
    return pl.pallas_call(
        functools.partial(_mix_out_kernel, n_in=len(a_list)),
        grid=(n // tm,),
        in_specs=in_specs,
        out_specs=pl.BlockSpec((tm, D_MODEL), lambda i: (i, 0)),
        out_shape=jax.ShapeDtypeStruct((n, D_MODEL), F32),
        compiler_params=_cparams(("parallel",)),
        name="mix_out",
    )(*a_list, w, x, g, b)


def _swiglu_chunks(xb, wg_ref, wu_ref, wd_ref, chunk):
    f = wg_ref.shape[-1]
    acc = None
    for c in range(0, f, chunk):
        w = min(chunk, f - c)
        hg = _dot(xb, wg_ref[:, c:c + w].astype(BF16))
        hu = _dot(xb, wu_ref[:, c:c + w].astype(BF16))
        h = (_silu(hg) * hu).astype(BF16)
        part = _dot(h, wd_ref[c:c + w, :].astype(BF16))
        acc = part if acc is None else acc + part
    return acc


def _ffn_kernel(x_ref, wg_ref, wu_ref, wd_ref, g_ref, b_ref, o_ref):
    x = x_ref[...]
    y = _swiglu_chunks(x.astype(BF16), wg_ref, wu_ref, wd_ref, 512)
    o_ref[...] = _layer_norm(ALPHA * x + y, g_ref[...], b_ref[...])


def _ffn(x, wg, wu, wd, g, b):
    n = x.shape[0]
    tm = ROW_TILE
    single = pl.Buffered(1)
    return pl.pallas_call(
        _ffn_kernel,
        grid=(n // tm,),
        in_specs=[pl.BlockSpec((tm, D_MODEL), lambda i: (i, 0)),
                  pl.BlockSpec(wg.shape, lambda i: (0, 0), pipeline_mode=single),
                  pl.BlockSpec(wu.shape, lambda i: (0, 0), pipeline_mode=single),
                  pl.BlockSpec(wd.shape, lambda i: (0, 0), pipeline_mode=single),
                  _const_spec(g.shape), _const_spec(b.shape)],
        out_specs=pl.BlockSpec((tm, D_MODEL), lambda i: (i, 0)),
        out_shape=jax.ShapeDtypeStruct((n, D_MODEL), F32),
        compiler_params=_cparams(("parallel",)),
        name="ffn",
    )(x, wg, wu, wd, g, b)


def _experts_kernel(te_ref, nv_ref, nu_ref, xs_ref, wg_ref, wu_ref, wd_ref, o_ref, wg16_ref, wu16_ref, wd16_ref):
    i = pl.program_id(0)
    j = pl.program_id(1)
    valid = nv_ref[i]

    @pl.when(j == 0)
    def _():
        o_ref[...] = jnp.zeros_like(o_ref)

    @pl.when(valid > 0)
    def _():
        wg16_ref[...] = wg_ref[...].astype(BF16)
        wu16_ref[...] = wu_ref[...].astype(BF16)
        wd16_ref[...] = wd_ref[...].astype(BF16)

    def accumulate(n_sub):
        for s in range(n_sub):
            rows = slice(s * MOE_SUBTILE, (s + 1) * MOE_SUBTILE)
            o_ref[rows, :] += _swiglu_chunks(xs_ref[rows, :].astype(BF16), wg16_ref, wu16_ref, wd16_ref, 256)

    n_sub_max = MOE_TILE // MOE_SUBTILE
    for n_sub in range(1, n_sub_max + 1):
        lo = (n_sub - 1) * MOE_SUBTILE
        hi = n_sub * MOE_SUBTILE
        pl.when((valid > lo) & (valid <= hi))(functools.partial(accumulate, n_sub))


def _experts(tile_expert, tile_valid, n_used, xs, wg, wu, wd, layer):
    npad = xs.shape[0]
    tm = MOE_TILE
    fb = EXPERT_DIM // MOE_FSPLIT
    last = MOE_FSPLIT - 1

    def tile(i, nu):
        return jnp.minimum(i, jnp.maximum(nu[0] - 1, 0))

    def fblock(i, j, nu):
        ju = jnp.where(i < nu[0], j, last)
        return ju + (tile(i, nu) % 2) * (last - 2 * ju)

    grid_spec = pltpu.PrefetchScalarGridSpec(
        num_scalar_prefetch=3,
        grid=(npad // tm, MOE_FSPLIT),
        in_specs=[pl.BlockSpec((tm, D_MODEL), lambda i, j, te, nv, nu: (tile(i, nu), 0)),
                  pl.BlockSpec((None, None, D_MODEL, fb),
                               lambda i, j, te, nv, nu: (layer, te[i], 0, fblock(i, j, nu))),
                  pl.BlockSpec((None, None, D_MODEL, fb),
                               lambda i, j, te, nv, nu: (layer, te[i], 0, fblock(i, j, nu))),
                  pl.BlockSpec((None, None, fb, D_MODEL),
                               lambda i, j, te, nv, nu: (layer, te[i], fblock(i, j, nu), 0))],
        out_specs=pl.BlockSpec((tm, D_MODEL), lambda i, j, te, nv, nu: (i, 0)),
        scratch_shapes=[pltpu.VMEM((D_MODEL, fb), BF16), pltpu.VMEM((D_MODEL, fb), BF16),
                        pltpu.VMEM((fb, D_MODEL), BF16)],
    )
    return pl.pallas_call(
        _experts_kernel,
        grid_spec=grid_spec,
        out_shape=jax.ShapeDtypeStruct((npad, D_MODEL), F32),
        compiler_params=_cparams(("arbitrary", "arbitrary")),
        name="experts",
    )(tile_expert, tile_valid, n_used, xs, wg, wu, wd)


def _router_kernel(x_ref, wr_ref, br_ref, ltri_ref, o_ref, cnt_ref, carry_ref):
    @pl.when(pl.program_id(0) == 0)
    def _():
        carry_ref[...] = jnp.zeros_like(carry_ref)

    logits = _dot(x_ref[...], wr_ref[...], HIGHEST) + br_ref[...]
    lane = lax.broadcasted_iota(jnp.int32, logits.shape, 1)
    m1 = jnp.max(logits, axis=-1, keepdims=True)
    i1 = jnp.min(jnp.where(logits == m1, lane, LANES), axis=-1, keepdims=True)
    rest = jnp.where(lane == i1, -jnp.inf, logits)
    m2 = jnp.max(rest, axis=-1, keepdims=True)
    i2 = jnp.min(jnp.where(rest == m2, lane, LANES), axis=-1, keepdims=True)
    e2 = jnp.exp(m2 - m1)
    g1 = 1.0 / (1.0 + e2)
    g2 = e2 / (1.0 + e2)
    sel1 = lane == i1
    sel2 = lane == i2
    onehot = jnp.where(sel1, 1.0, jnp.where(sel2, 1.0, 0.0))
    before = _dot(ltri_ref[...], onehot.astype(BF16)) + carry_ref[0:1, :]
    r1 = jnp.sum(jnp.where(sel1, before, 0.0), axis=-1, keepdims=True)
    r2 = jnp.sum(jnp.where(sel2, before, 0.0), axis=-1, keepdims=True)
    new_carry = carry_ref[0:1, :] + jnp.sum(onehot, axis=0, keepdims=True)
    carry_ref[...] = jnp.broadcast_to(new_carry, carry_ref.shape)
    cnt_ref[...] = jnp.broadcast_to(new_carry, cnt_ref.shape)
    vals = (i1.astype(F32), i2.astype(F32), r1, r2, g1, g2)
    out = jnp.zeros(logits.shape, F32)
    for k, v in enumerate(vals):
        out = jnp.where(lane == k, v, out)
    o_ref[...] = out


def _router(x, wr, br):
    n = x.shape[0]
    tm = ROW_TILE
    ltri = jnp.asarray(np.tril(np.ones((tm, tm), np.float32), -1), BF16)
    return pl.pallas_call(
        _router_kernel,
        grid=(n // tm,),
        in_specs=[pl.BlockSpec((tm, D_MODEL), lambda i: (i, 0)),
                  _const_spec(wr.shape), _const_spec(br.shape), _const_spec(ltri.shape)],
        out_specs=[pl.BlockSpec((tm, LANES), lambda i: (i, 0)),
                   pl.BlockSpec((HALO, LANES), lambda i: (0, 0))],
        out_shape=[jax.ShapeDtypeStruct((n, LANES), F32),
                   jax.ShapeDtypeStruct((HALO, LANES), F32)],
        scratch_shapes=[pltpu.VMEM((HALO, LANES), F32)],
        compiler_params=_cparams(("arbitrary",)),
        name="router",
    )(x, wr, br, ltri)


def _dispatch_kernel(slot_ref, x_ref, xs_in_ref, xs_ref, sem):
    del xs_in_ref
    tg = x_ref.shape[0]

    def issue(r, carry):
        for k in range(2):
            pltpu.make_async_copy(x_ref.at[pl.ds(r, 1)], xs_ref.at[pl.ds(slot_ref[0, 0, 2 * r + k], 1)],
                                  sem).start()
        return carry

    lax.fori_loop(0, tg, issue, 0, unroll=8)
    for k in range(2):
        pltpu.make_async_copy(x_ref, xs_ref.at[pl.ds(0, tg)], sem).wait()


def _dispatch(x, slots, npad):
    n = x.shape[0]
    tg = GATHER_TILE
    slots3 = slots.reshape(n // tg, 1, 2 * tg)
    zeros = jnp.zeros((npad, D_MODEL), F32)
    return pl.pallas_call(
        _dispatch_kernel,
        grid=(n // tg,),
        in_specs=[pl.BlockSpec((1, 1, 2 * tg), lambda i: (i, 0, 0), memory_space=pltpu.SMEM),
                  pl.BlockSpec((tg, D_MODEL), lambda i: (i, 0)),
                  pl.BlockSpec(memory_space=pl.ANY)],
        out_specs=pl.BlockSpec(memory_space=pl.ANY),
        out_shape=jax.ShapeDtypeStruct((npad, D_MODEL), F32),
        scratch_shapes=[pltpu.SemaphoreType.DMA(())],
        input_output_aliases={2: 0},
        compiler_params=_cparams(("arbitrary",)),
        name="moe_dispatch",
    )(slots3, x, zeros)


def _combine_kernel(slot_ref, ys_ref, x_ref, info_ref, g_ref, b_ref, o_ref, buf_ref, sem):
    tg = x_ref.shape[0]

    def issue(r, carry):
        for k in range(2):
            pltpu.make_async_copy(ys_ref.at[pl.ds(slot_ref[0, 0, 2 * r + k], 1)],
                                  buf_ref.at[k, pl.ds(r, 1)], sem).start()
        return carry

    lax.fori_loop(0, tg, issue, 0, unroll=8)
    for k in range(2):
        pltpu.make_async_copy(ys_ref.at[pl.ds(0, tg)], buf_ref.at[k], sem).wait()
    info = info_ref[...]
    y = info[:, 4:5] * buf_ref[0] + info[:, 5:6] * buf_ref[1]
    o_ref[...] = _layer_norm(ALPHA * x_ref[...] + y, g_ref[...], b_ref[...])


def _combine(ys, slots, x, info, g, b):
    n = x.shape[0]
    tg = GATHER_TILE
    slots3 = slots.reshape(n // tg, 1, 2 * tg)
    return pl.pallas_call(
        _combine_kernel,
        grid=(n // tg,),
        in_specs=[pl.BlockSpec((1, 1, 2 * tg), lambda i: (i, 0, 0), memory_space=pltpu.SMEM),
                  pl.BlockSpec(memory_space=pl.ANY),
                  pl.BlockSpec((tg, D_MODEL), lambda i: (i, 0)),
                  pl.BlockSpec((tg, LANES), lambda i: (i, 0)),
                  _const_spec(g.shape), _const_spec(b.shape)],
        out_specs=pl.BlockSpec((tg, D_MODEL), lambda i: (i, 0)),
        out_shape=jax.ShapeDtypeStruct((n, D_MODEL), F32),
        scratch_shapes=[pltpu.VMEM((2, tg, D_MODEL), F32), pltpu.SemaphoreType.DMA(())],
        compiler_params=_cparams(("arbitrary",)),
        name="moe_combine",
    )(slots3, ys, x, info, g, b)


def _moe(x, wr, br, wg, wu, wd, layer, g, b):
    n = x.shape[0]
    tm = MOE_TILE
    info, counts = _router(x, wr, br)
    experts = info[:, 0:2].astype(jnp.int32)
    ranks = info[:, 2:4].astype(jnp.int32)
    count = counts[0, :N_EXPERTS].astype(jnp.int32)
    tiles = (count + tm - 1) // tm
    tile_end = jnp.cumsum(tiles)
    tile_start = tile_end - tiles
    slots = tile_start[experts] * tm + ranks
    n_tiles = (2 * n) // tm + N_EXPERTS
    n_used = tile_end[-1]
    tile_ids = jnp.arange(n_tiles, dtype=jnp.int32)
    tile_id = jnp.minimum(tile_ids, n_used - 1)
    tile_expert = jnp.sum((tile_id[:, None] >= tile_end[None, :]).astype(jnp.int32), axis=1)
    tile_valid = jnp.clip(count[tile_expert] - (tile_ids - tile_start[tile_expert]) * tm, 0, tm)
    tile_valid = jnp.where(tile_ids < n_used, tile_valid, 0)
    xs = _dispatch(x, slots, n_tiles * tm)
    ys = _experts(tile_expert.astype(jnp.int32), tile_valid.astype(jnp.int32),
                  n_used.reshape(1).astype(jnp.int32), xs, wg, wu, wd, layer)
    return _combine(ys, slots, x, info, g, b)


def _rope(x, c, sa, sb):
    return x * c + pltpu.roll(x, LANES - ROPE_DIM // 2, 1) * sa + pltpu.roll(x, ROPE_DIM // 2, 1) * sb


def _swa_kernel(sink_ref, q_ref, k_ref, v_ref, c_ref, sa_ref, sb_ref, o_ref, kp_ref, vp_ref):
    n = pl.program_id(1)
    blk = SWA_BLOCK

    @pl.when(n == 0)
    def _():
        kp_ref[...] = jnp.zeros_like(kp_ref)
        vp_ref[...] = jnp.zeros_like(vp_ref)

    c = c_ref[...]
    sa = sa_ref[...]
    sb = sb_ref[...]
    lane = lax.broadcasted_iota(jnp.int32, (blk, LANES), 1)
    low = lane < SWA_HEAD_DIM
    row = lax.broadcasted_iota(jnp.int32, (blk, blk), 0)
    col = lax.broadcasted_iota(jnp.int32, (blk, blk), 1)
    mask_prev = (col > row) & (n > 0)
    mask_cur = col <= row

    k_cur = _rope(k_ref[...].astype(F32), c, sa, sb).astype(BF16)
    v_cur = v_ref[...]
    k_prev = kp_ref[...]
    v_prev = vp_ref[...]
    zero = jnp.zeros((blk, LANES), BF16)

    def halves(t, g):
        swapped = jnp.concatenate([t[:, SWA_HEAD_DIM:], t[:, :SWA_HEAD_DIM]], axis=1)
        src_a, src_b = (t, swapped) if g == 0 else (swapped, t)
        return jnp.where(low, src_a, zero), jnp.where(low, zero, src_b)

    kc = [halves(k_cur, g) for g in range(SWA_KV_HEADS)]
    kp = [halves(k_prev, g) for g in range(SWA_KV_HEADS)]
    vc = [halves(v_cur, g) for g in range(SWA_KV_HEADS)]
    vp = [halves(v_prev, g) for g in range(SWA_KV_HEADS)]
    group = SWA_HEADS // SWA_KV_HEADS
    per_head = lambda t: jnp.stack([t[h // group][h % 2] for h in range(SWA_HEADS)])
    q_pairs = [(_rope(q_ref[:, pr * LANES:(pr + 1) * LANES].astype(F32), c, sa, sb)
                * (SWA_HEAD_DIM ** -0.5)).astype(BF16) for pr in range(SWA_HEADS // 2)]
    q8 = jnp.stack([q_pairs[h // 2] for h in range(SWA_HEADS)])
    sink = jnp.stack([jnp.full((1, 1), sink_ref[h], F32) for h in range(SWA_HEADS)])
    s_prev = jnp.where(mask_prev, _bdot_nt(q8, per_head(kp)), -jnp.inf)
    s_cur = jnp.where(mask_cur, _bdot_nt(q8, per_head(kc)), -jnp.inf)
    m = jnp.maximum(jnp.maximum(jnp.max(s_prev, axis=-1, keepdims=True),
                                jnp.max(s_cur, axis=-1, keepdims=True)), sink)
    p_prev = jnp.exp(s_prev - m)
    p_cur = jnp.exp(s_cur - m)
    denom = (jnp.sum(p_prev, axis=-1, keepdims=True) + jnp.sum(p_cur, axis=-1, keepdims=True)
             + jnp.exp(sink - m))
    inv = 1.0 / denom
    out = (_bdot((p_prev * inv).astype(BF16), per_head(vp))
           + _bdot((p_cur * inv).astype(BF16), per_head(vc)))
    for pr in range(SWA_HEADS // 2):
        o_ref[:, pr * LANES:(pr + 1) * LANES] = (out[2 * pr] + out[2 * pr + 1]).astype(o_ref.dtype)

    kp_ref[...] = k_cur
    vp_ref[...] = v_cur


def _swa(p, sinks, tables, bsz, t, q_col, k_col, v_col):
    nb = t // SWA_BLOCK
    c, sa, sb = tables
    row = lambda b, n: b * nb + n
    return pl.pallas_call(
        _swa_kernel,
        grid=(bsz, nb),
        in_specs=[pl.BlockSpec(memory_space=pltpu.SMEM),
                  pl.BlockSpec((SWA_BLOCK, SWA_Q), lambda b, n: (row(b, n), q_col)),
                  pl.BlockSpec((SWA_BLOCK, SWA_KV), lambda b, n: (row(b, n), k_col)),
                  pl.BlockSpec((SWA_BLOCK, SWA_KV), lambda b, n: (row(b, n), v_col)),
                  pl.BlockSpec((SWA_BLOCK, LANES), lambda b, n: (n, 0)),
                  pl.BlockSpec((SWA_BLOCK, LANES), lambda b, n: (n, 0)),
                  pl.BlockSpec((SWA_BLOCK, LANES), lambda b, n: (n, 0))],
        out_specs=pl.BlockSpec((SWA_BLOCK, SWA_Q), lambda b, n: (row(b, n), 0)),
        out_shape=jax.ShapeDtypeStruct((bsz * t, SWA_Q), BF16),
        scratch_shapes=[pltpu.VMEM((SWA_BLOCK, SWA_KV), BF16), pltpu.VMEM((SWA_BLOCK, SWA_KV), BF16)],
        compiler_params=_cparams(("parallel", "arbitrary")),
        name="swa",
    )(sinks, p, p, p, c, sa, sb)


def _rope_tables(t):
    half = ROPE_DIM // 2
    pos = jnp.arange(t, dtype=jnp.int32)
    inv_freq = jnp.power(ROPE_THETA, -jnp.arange(half, dtype=F32) / half)
    ang = pos.astype(F32)[:, None] * inv_freq[None, :]
    cos = jnp.cos(ang)
    sin = jnp.sin(ang)
    ones = jnp.ones((t, SWA_HEAD_DIM - ROPE_DIM), F32)
    zeros = jnp.zeros((t, SWA_HEAD_DIM - ROPE_DIM), F32)
    zhalf = jnp.zeros((t, half), F32)
    c = jnp.concatenate([cos, cos, ones], axis=1)
    sa = jnp.concatenate([-sin, zhalf, zeros], axis=1)
    sb = jnp.concatenate([zhalf, sin, zeros], axis=1)
    two = lambda a: jnp.concatenate([a, a], axis=1)
    return two(c), two(sa), two(sb)


def _split(a):
    hi = a.astype(BF16)
    return hi, (a - hi.astype(F32)).astype(BF16)


def _bdot(a, b):
    return lax.dot_general(a, b, (((2,), (1,)), ((0,), (0,))), preferred_element_type=F32)


def _bdot_nt(a, b):
    return lax.dot_general(a, b, (((2,), (2,)), ((0,), (0,))), preferred_element_type=F32)


def _mm(a, b):
    if DN_INV_PASSES == 1:
        return _bdot(a.astype(BF16), b.astype(BF16))
    ah, al = _split(a)
    bh, bl = _split(b)
    return _bdot(ah, bh) + (_bdot(ah, bl) + _bdot(al, bh))


def _unit_lower_inverse(a, eye, blk_mask, merge_masks):
    d0 = jnp.where(blk_mask, a, 0.0)
    x = eye - d0
    p = _mm(d0, d0)
    x = x + _mm(x, p)
    p = _mm(p, p)
    x = x + _mm(x, p)
    for m in merge_masks:
        lm = jnp.where(m, a, 0.0)
        x = x - _mm(x, _mm(lm, x))
    return x


def _dn_kernel(q_ref, k_ref, v_ref, gate_ref, tail_ref, cw_ref, par_ref, nw_ref, o_ref,
               xpad_ref, qkv_ref, bg_ref, s_ref, gc_ref, u_ref, oi_ref, kd_ref, wq_ref):
    n = pl.program_id(1)
    tb = q_ref.shape[0]
    ch = DN_CHUNK
    hd = DN_HEAD_DIM

    @pl.when(n == 0)
    def _():
        xpad_ref[0:HALO, :] = jnp.zeros((HALO, 3 * DN_W), F32)
        s_ref[...] = jnp.zeros_like(s_ref)

    xpad_ref[HALO:HALO + tb, 0:DN_W] = q_ref[...].astype(F32)
    xpad_ref[HALO:HALO + tb, DN_W:2 * DN_W] = k_ref[...].astype(F32)
    xpad_ref[HALO:HALO + tb, 2 * DN_W:3 * DN_W] = v_ref[...].astype(F32)
    conv = None
    for kk in range(DN_CONV):
        off = HALO - (DN_CONV - 1) + kk
        term = cw_ref[kk:kk + 1, :] * xpad_ref[off:off + tb, :]
        conv = term if conv is None else conv + term
    xpad_ref[0:HALO, :] = xpad_ref[tb:tb + HALO, :]
    qkv = _silu(conv)
    for h in range(2 * DN_HEADS):
        xh = qkv[:, h * hd:(h + 1) * hd]
        scale = lax.rsqrt(jnp.sum(xh * xh, axis=-1, keepdims=True) + RMS_EPS)
        if h < DN_HEADS:
            scale = scale * (hd ** -0.5)
        qkv_ref[:, h * hd:(h + 1) * hd] = xh * scale
    qkv_ref[:, 2 * DN_W:] = qkv[:, 2 * DN_W:]

    tail = tail_ref[...]
    lane = lax.broadcasted_iota(jnp.int32, tail.shape, 1)
    gval = -jnp.exp(par_ref[0:1, :]) * _softplus(tail + par_ref[1:2, :])
    bg_ref[...] = jnp.where(lane < DN_HEADS, _sigmoid(tail), gval)

    rt = lax.broadcasted_iota(jnp.int32, (tb, tb), 0)
    ct = lax.broadcasted_iota(jnp.int32, (tb, tb), 1)
    ltri = jnp.where((rt >= ct) & ((rt // ch) == (ct // ch)), 1.0, 0.0).astype(F32)
    er = lax.broadcasted_iota(jnp.int32, (LANES, LANES), 0)
    ec = lax.broadcasted_iota(jnp.int32, (LANES, LANES), 1)
    eye = jnp.where(er == ec, 1.0, 0.0).astype(F32)
    bg = bg_ref[...]
    gc_all = _dot(ltri, jnp.where(lane >= DN_HEADS, bg, 0.0), HIGHEST)
    gc_t = _dot_nt(eye, gc_all, HIGHEST)
    gc_ref[...] = gc_all

    sb = 2 * ch
    same = (er // ch) == (ec // ch)
    causal = same & (er >= ec)
    strict = same & (er > ec)
    blk_mask = (er > ec) & ((er // 8) == (ec // 8))
    merge_masks = [((er // (2 * s)) == (ec // (2 * s))) & ((er % (2 * s)) >= s) & ((ec % (2 * s)) < s)
                   for s in (8, 16, 32)]
    first = lax.broadcasted_iota(jnp.int32, (sb, 1), 0) < ch
    probs = [(h, b2) for h in range(DN_HEADS) for b2 in range(tb // sb)]
    rows_of = lambda b2: slice(b2 * sb, (b2 + 1) * sb)
    stack = lambda f: jnp.stack([f(h, b2) for h, b2 in probs])
    gcol = stack(lambda h, b2: gc_all[rows_of(b2), DN_HEADS + h:DN_HEADS + h + 1])
    grow = stack(lambda h, b2: gc_t[DN_HEADS + h:DN_HEADS + h + 1, rows_of(b2)])
    beta = stack(lambda h, b2: bg[rows_of(b2), h:h + 1])
    q = stack(lambda h, b2: qkv_ref[rows_of(b2), h * hd:(h + 1) * hd])
    k = stack(lambda h, b2: qkv_ref[rows_of(b2), DN_W + h * hd:DN_W + (h + 1) * hd])
    v = stack(lambda h, b2: qkv_ref[rows_of(b2), 2 * DN_W + h * hd:2 * DN_W + (h + 1) * hd])
    decay = jnp.where(causal, jnp.exp(gcol - grow), 0.0)
    k16 = k.astype(BF16)
    a = jnp.where(strict, _bdot_nt(k16, k16) * decay * beta, 0.0)
    tinv = _unit_lower_inverse(a, eye, blk_mask, merge_masks)
    egc = jnp.exp(gcol)
    u = _mm(tinv, v * beta)
    w16 = _mm(tinv, k * (beta * egc)).astype(BF16)
    attn16 = (_bdot_nt(q.astype(BF16), k16) * decay).astype(BF16)
    q_eff16 = (q * egc - _bdot(attn16, w16)).astype(BF16)
    o_intra = _bdot(attn16, u.astype(BF16))
    glast = jnp.where(first, gcol[:, ch - 1:ch, :], gcol[:, sb - 1:sb, :])
    k_dec16 = (k * jnp.exp(glast - gcol)).astype(BF16)
    for g, (h, b2) in enumerate(probs):
        u_ref[h, rows_of(b2), :] = u[g]
        oi_ref[h, rows_of(b2), :] = o_intra[g]
        kd_ref[h, rows_of(b2), :] = k_dec16[g]
        for c in range(2):
            wq_ref[h, 2 * b2 + c, 0:ch, :] = w16[g, c * ch:(c + 1) * ch]
            wq_ref[h, 2 * b2 + c, ch:sb, :] = q_eff16[g, c * ch:(c + 1) * ch]

    for c in range(tb // ch):
        rows = slice(c * ch, (c + 1) * ch)
        s = s_ref[...]
        r = _bdot(wq_ref[:, c], s.astype(BF16))
        v_new16 = (u_ref[:, rows, :] - r[:, 0:ch]).astype(BF16)
        for h in range(DN_HEADS):
            glast = gc_ref[(c + 1) * ch - 1:(c + 1) * ch, DN_HEADS + h:DN_HEADS + h + 1]
            s_ref[h] = s[h] * jnp.exp(glast) + _dot_tn(kd_ref[h, rows, :], v_new16[h])
        for h in range(DN_HEADS):
            o = r[h, ch:sb] + oi_ref[h, rows, :]
            gate = gate_ref[rows, h * hd:(h + 1) * hd].astype(F32)
            on = o * lax.rsqrt(jnp.mean(o * o, axis=-1, keepdims=True) + RMS_EPS) * nw_ref[...]
            o_ref[rows, h * hd:(h + 1) * hd] = (on * _silu(gate)).astype(o_ref.dtype)


def _deltanet(p, tail, conv_w, par, norm_w, bsz, t, cols):
    tb = DN_BLOCK
    nb = t // tb
    row = lambda b, n: b * nb + n
    col_spec = lambda cidx: pl.BlockSpec((tb, DN_W), lambda b, n: (row(b, n), cidx))
    return pl.pallas_call(
        _dn_kernel,
        grid=(bsz, nb),
        in_specs=[col_spec(cols[0]), col_spec(cols[1]), col_spec(cols[2]), col_spec(cols[3]),
                  pl.BlockSpec((tb, LANES), lambda b, n: (row(b, n), 0)),
                  _const_spec(conv_w.shape), _const_spec(par.shape), _const_spec(norm_w.shape)],
        out_specs=pl.BlockSpec((tb, DN_W), lambda b, n: (row(b, n), 0)),
        out_shape=jax.ShapeDtypeStruct((bsz * t, DN_W), BF16),
        scratch_shapes=[pltpu.VMEM((tb + HALO, 3 * DN_W), F32),
                        pltpu.VMEM((tb, 3 * DN_W), F32),
                        pltpu.VMEM((tb, LANES), F32),
                        pltpu.VMEM((DN_HEADS, DN_HEAD_DIM, DN_HEAD_DIM), F32),
                        pltpu.VMEM((tb, LANES), F32),
                        pltpu.VMEM((DN_HEADS, tb, DN_HEAD_DIM), F32),
                        pltpu.VMEM((DN_HEADS, tb, DN_HEAD_DIM), F32),
                        pltpu.VMEM((DN_HEADS, tb, DN_HEAD_DIM), BF16),
                        pltpu.VMEM((DN_HEADS, tb // DN_CHUNK, 2 * DN_CHUNK, DN_HEAD_DIM), BF16)],
        compiler_params=_cparams(("parallel", "arbitrary")),
        name="deltanet",
    )(p, p, p, p, tail, conv_w, par, norm_w)


def _ssd_kernel(z_ref, xs_ref, bc_ref, dt_ref, cw_ref, cb_ref, pcol_ref, prow_ref,
                dskip_ref, nw_ref, expand_ref, o_ref, xpad_ref, xc_ref, h_ref):
    n = pl.program_id(1)
    L = SSM_CHUNK

    @pl.when(n == 0)
    def _():
        xpad_ref[0:L, :] = jnp.zeros((L, SSM_CONV_DIM), BF16)
        h_ref[...] = jnp.zeros_like(h_ref)

    xpad_ref[L:2 * L, 0:SSM_D_INNER] = xs_ref[...]
    xpad_ref[L:2 * L, SSM_D_INNER:] = bc_ref[...]
    sr = lax.broadcasted_iota(jnp.int32, (L, 2 * L), 0)
    sc = lax.broadcasted_iota(jnp.int32, (L, 2 * L), 1)
    shifts = [jnp.where(sc == sr + (L - 3 + kk), 1.0, 0.0).astype(BF16) for kk in range(3)]
    cw_blk = 512
    for c0 in range(0, SSM_CONV_DIM, cw_blk):
        cols = slice(c0, c0 + cw_blk)
        xe = xpad_ref[:, cols]
        conv = cb_ref[:, cols] + cw_ref[3:4, cols] * xe[L:2 * L].astype(F32)
        for kk in range(3):
            conv = conv + cw_ref[kk:kk + 1, cols] * _dot(shifts[kk], xe)
        xc_ref[:, cols] = _silu(conv)
    xpad_ref[0:L, :] = xpad_ref[L:2 * L, :]

    ri = lax.broadcasted_iota(jnp.int32, (L, L), 0)
    ci = lax.broadcasted_iota(jnp.int32, (L, L), 1)
    causal = ri >= ci
    ltri = jnp.where(causal, 1.0, 0.0).astype(F32)
    utri = jnp.where(ri <= ci, 1.0, 0.0).astype(F32)
    eye = jnp.where(ri == ci, 1.0, 0.0).astype(F32)

    dt_raw = dt_ref[...]
    dt = _softplus(dt_raw + pcol_ref[0:1, :])
    acs = _dot(ltri, dt * pcol_ref[1:2, :], HIGHEST)
    tot = acs[L - 1:L, :]
    f1 = dt * jnp.exp(tot - acs)
    ea = jnp.exp(acs)
    dt_r = _softplus(_dot_nt(eye, dt_raw, HIGHEST)[0:SSM_HEADS] + prow_ref[0])
    acs_r = _dot(dt_r * prow_ref[1], utri, HIGHEST)

    stacked = jnp.concatenate([dt, f1, ea], axis=0).astype(BF16)
    cd = jnp.broadcast_to(jnp.exp(tot), (HALO, LANES))
    cd_hi, cd_lo = _split(cd)
    lane = lax.broadcasted_iota(jnp.int32, (L, LANES), 1)
    low = lane < SSM_HEAD_DIM
    zero16 = jnp.zeros((L, LANES), BF16)
    for g in range(SSM_GROUPS):
        gl = slice(g * SSM_GROUP_W, (g + 1) * SSM_GROUP_W)
        ex = expand_ref[:, gl]
        wide = _dot(stacked, ex)
        cd_x = (_dot(cd_hi, ex) + _dot(cd_lo, ex))[0:1, :]
        xs = xc_ref[:, gl]
        xdt16 = (xs * wide[0:L]).astype(BF16)
        xdd16 = (xs * wide[L:2 * L]).astype(BF16)
        bm = xc_ref[:, SSM_D_INNER + g * SSM_STATE:SSM_D_INNER + (g + 1) * SSM_STATE].astype(BF16)
        cm = xc_ref[:, SSM_D_INNER + SSM_GN + g * SSM_STATE:
                    SSM_D_INNER + SSM_GN + (g + 1) * SSM_STATE].astype(BF16)
        cb = _dot_nt(cm, bm)
        hg = h_ref[g]
        y_off = _dot(cm, hg.astype(BF16)) * wide[2 * L:3 * L]
        h_ref[g] = hg * cd_x + _dot_tn(bm, xdd16)
        parts = []
        for j in range(SSM_GROUP_W // LANES):
            xp16 = xdt16[:, j * LANES:(j + 1) * LANES]
            yd = None
            for side in range(2):
                hidx = (g * SSM_GROUP_W + j * LANES) // SSM_HEAD_DIM + side
                seg = jnp.where(causal, jnp.exp(acs[:, hidx:hidx + 1] - acs_r[hidx:hidx + 1, :]), 0.0)
                m = (cb * seg).astype(BF16)
                xm = jnp.where(low, xp16, zero16) if side == 0 else jnp.where(low, zero16, xp16)
                part = _dot(m, xm)
                yd = part if yd is None else yd + part
            parts.append(yd)
        y = jnp.concatenate(parts, axis=1) + y_off + dskip_ref[:, gl] * xs
        y = y * _silu(z_ref[:, gl].astype(F32))
        yn = y * lax.rsqrt(jnp.mean(y * y, axis=-1, keepdims=True) + RMS_EPS) * nw_ref[:, gl]
        o_ref[:, gl] = yn.astype(o_ref.dtype)


def _ssd(p, dt, conv_w, conv_b, pcol, prow, dskip, norm_w, expand, bsz, t):
    L = SSM_CHUNK
    nb = t // L
    row = lambda b, n: b * nb + n
    return pl.pallas_call(
        _ssd_kernel,
        grid=(bsz, nb),
        in_specs=[pl.BlockSpec((L, SSM_D_INNER), lambda b, n: (row(b, n), 0)),
                  pl.BlockSpec((L, SSM_D_INNER), lambda b, n: (row(b, n), 1)),
                  pl.BlockSpec((L, 2 * SSM_GN), lambda b, n: (row(b, n), 2 * SSM_D_INNER // (2 * SSM_GN))),
                  pl.BlockSpec((L, LANES), lambda b, n: (row(b, n), 0)),
                  _const_spec(conv_w.shape), _const_spec(conv_b.shape), _const_spec(pcol.shape),
                  _const_spec(prow.shape), _const_spec(dskip.shape), _const_spec(norm_w.shape),
                  _const_spec(expand.shape)],
        out_specs=pl.BlockSpec((L, SSM_D_INNER), lambda b, n: (row(b, n), 0)),
        out_shape=jax.ShapeDtypeStruct((bsz * t, SSM_D_INNER), BF16),
        scratch_shapes=[pltpu.VMEM((2 * L, SSM_CONV_DIM), BF16),
                        pltpu.VMEM((L, SSM_CONV_DIM), F32),
                        pltpu.VMEM((SSM_GROUPS, SSM_STATE, SSM_GROUP_W), F32)],
        compiler_params=_cparams(("parallel", "arbitrary")),
        name="ssd",
    )(p, p, p, dt, conv_w, conv_b, pcol, prow, dskip, norm_w, expand)


def _pad_lanes(v, offset=0):
    v = v.astype(F32)
    return jnp.zeros((1, LANES), F32).at[0, offset:offset + v.shape[0]].set(v)


def _even_layer(x, bsz, t, w_in, sinks, conv_w, a_log, dt_bias, norm_w, w_out, ln_g, ln_b,
                ffn_wg, ffn_wu, ffn_wd, rope_tables):
    o = np.cumsum((0, SWA_Q, SWA_KV, SWA_KV, DN_W, DN_W, DN_W, DN_W, DN_HEADS, DN_HEADS))
    seg = lambda i: w_in[:, o[i]:o[i + 1]]
    w_main = jnp.concatenate([seg(0), seg(3), seg(4), seg(5), seg(6), seg(1), seg(2)], axis=1).astype(BF16)
    w_tail = w_in[:, o[7]:o[9]]
    wt = jnp.zeros((D_MODEL, LANES), F32).at[:, :2 * DN_HEADS].set(w_tail)
    p, tail = _proj(x, w_main, wt)
    out_a = _swa(p, sinks.astype(F32), rope_tables, bsz, t,
                 q_col=0, k_col=(SWA_Q + 4 * DN_W) // SWA_KV, v_col=(SWA_Q + 4 * DN_W) // SWA_KV + 1)
    par = jnp.concatenate([_pad_lanes(a_log, DN_HEADS), _pad_lanes(dt_bias, DN_HEADS)], axis=0)
    out_b = _deltanet(p, tail, conv_w.astype(F32), par, norm_w.reshape(1, DN_HEAD_DIM).astype(F32),
                      bsz, t, cols=(1, 2, 3, 4))
    x = _mix_out([out_a, out_b], w_out.astype(BF16), x, ln_g[0:1], ln_b[0:1])
    return _ffn(x, ffn_wg.astype(BF16), ffn_wu.astype(BF16), ffn_wd.astype(BF16), ln_g[1:2], ln_b[1:2])


def _odd_layer(x, bsz, t, w_in, conv_w, conv_b, dt_bias, a_log, d_skip, norm_w, w_out, ln_g, ln_b,
               w_router, b_router, moe_wg, moe_wu, moe_wd, moe_layer):
    main = SSM_D_INNER + SSM_CONV_DIM
    w_tail = w_in[:, main:]
    wt = jnp.zeros((D_MODEL, LANES), F32).at[:, :SSM_HEADS].set(w_tail)
    p, dt = _proj(x, w_in[:, :main].astype(BF16), wt)
    a = -jnp.exp(a_log.astype(F32))
    pcol = jnp.concatenate([_pad_lanes(dt_bias), _pad_lanes(a)], axis=0)
    prow = jnp.stack([jnp.broadcast_to(dt_bias.astype(F32)[:, None], (SSM_HEADS, SSM_CHUNK)),
                      jnp.broadcast_to(a[:, None], (SSM_HEADS, SSM_CHUNK))])
    dskip = jnp.repeat(d_skip.astype(F32), SSM_HEAD_DIM)[None, :]
    expand = (jnp.arange(LANES)[:, None] == (jnp.arange(SSM_D_INNER) // SSM_HEAD_DIM)[None, :]).astype(BF16)
    y = _ssd(p, dt, conv_w.astype(F32), conv_b.astype(F32)[None, :], pcol, prow, dskip,
             norm_w.astype(F32)[None, :], expand, bsz, t)
    x = _mix_out([y], w_out.astype(BF16), x, ln_g[0:1], ln_b[0:1])
    wr = jnp.zeros((D_MODEL, LANES), F32).at[:, :N_EXPERTS].set(w_router)
    br = jnp.full((1, LANES), -1e30, F32).at[0, :N_EXPERTS].set(b_router.astype(F32))
    return _moe(x, wr, br, moe_wg, moe_wu, moe_wd, moe_layer, ln_g[1:2], ln_b[1:2])


def kernel(x, ln_g, ln_b, even_w_in, swa_sinks, dn_conv_w, dn_a_log, dn_dt_bias, dn_norm_w, even_w_out,
           ssm_w_in, ssm_conv_w, ssm_conv_b, ssm_dt_bias, ssm_a_log, ssm_d, ssm_norm_w, ssm_w_out,
           ffn_w_gate, ffn_w_up, ffn_w_down, moe_w_router, moe_b_router, moe_w_gate, moe_w_up, moe_w_down):
    bsz, t, d = x.shape
    h = x.reshape(bsz * t, d)
    rope_tables = _rope_tables(t)
    for i in range(DEPTH):
        j = i // 2
        if i % 2 == 0:
            h = _even_layer(h, bsz, t, even_w_in[j], swa_sinks[j], dn_conv_w[j], dn_a_log[j], dn_dt_bias[j],
                            dn_norm_w[j], even_w_out[j], ln_g[i], ln_b[i],
                            ffn_w_gate[j], ffn_w_up[j], ffn_w_down[j], rope_tables)
        else:
            h = _odd_layer(h, bsz, t, ssm_w_in[j], ssm_conv_w[j], ssm_conv_b[j], ssm_dt_bias[j], ssm_a_log[j],
                           ssm_d[j], ssm_norm_w[j], ssm_w_out[j], ln_g[i], ln_b[i],
                           moe_w_router[j], moe_b_router[j], moe_w_gate, moe_w_up, moe_w_down, j)
    return h.reshape(bsz, t, d)
```

```python
import functools

import numpy as np
import jax
import jax.numpy as jnp
from jax import lax
from jax.experimental import pallas as pl
from jax.experimental.pallas import tpu as pltpu

F32 = jnp.float32
BF16 = jnp.bfloat16
HIGHEST = lax.Precision.HIGHEST

D_MODEL = 1024
DEPTH = 4
ALPHA = (2 * DEPTH) ** 0.25
LN_EPS = 1e-5
RMS_EPS = 1e-6

SWA_HEADS = 8
SWA_KV_HEADS = 2
SWA_HEAD_DIM = 64
SWA_BLOCK = 128
ROPE_DIM = SWA_HEAD_DIM // 4
ROPE_THETA = 500000.0
SWA_Q = SWA_HEADS * SWA_HEAD_DIM
SWA_KV = SWA_KV_HEADS * SWA_HEAD_DIM

DN_HEADS = 4
DN_HEAD_DIM = 128
DN_CONV = 4
DN_CHUNK = 64
DN_W = DN_HEADS * DN_HEAD_DIM

SSM_D_INNER = 2 * D_MODEL
SSM_HEAD_DIM = 64
SSM_HEADS = SSM_D_INNER // SSM_HEAD_DIM
SSM_GROUPS = 4
SSM_STATE = 128
SSM_CHUNK = 128
SSM_GN = SSM_GROUPS * SSM_STATE
SSM_CONV_DIM = SSM_D_INNER + 2 * SSM_GN
SSM_GROUP_W = SSM_D_INNER // SSM_GROUPS

FFN_DIM = 2816
N_EXPERTS = 8
EXPERT_DIM = 3584

LANES = 128
HALO = 8
VMEM_LIMIT = 52 * 1024 * 1024

ROW_TILE = 512
MOE_TILE = 1024
MOE_SUBTILE = 512
MOE_FSPLIT = 7
SWA_STEP_BLOCKS = 2
DN_BLOCK = 256
DN_INV_PASSES = 1
GATHER_TILE = 256


def _cparams(sem):
    return pltpu.CompilerParams(dimension_semantics=sem, vmem_limit_bytes=VMEM_LIMIT)


def _sigmoid(x):
    return 1.0 / (1.0 + jnp.exp(-x))


def _silu(x):
    hx = 0.5 * x
    return hx + hx * jnp.tanh(hx)


def _softplus(x):
    return jnp.maximum(x, 0.0) + jnp.log(1.0 + jnp.exp(-jnp.abs(x)))


def _layer_norm(y, g, b):
    mu = jnp.mean(y, axis=-1, keepdims=True)
    d = y - mu
    var = jnp.mean(d * d, axis=-1, keepdims=True)
    return d * lax.rsqrt(var + LN_EPS) * g + b


def _dot(a, b, precision=None):
    return jnp.dot(a, b, preferred_element_type=F32, precision=precision)


def _dot_nt(a, b, precision=None):
    return lax.dot_general(a, b, (((1,), (1,)), ((), ())), preferred_element_type=F32, precision=precision)


def _dot_tn(a, b, precision=None):
    return lax.dot_general(a, b, (((0,), (0,)), ((), ())), preferred_element_type=F32, precision=precision)


def _const_spec(shape):
    nd = len(shape)
    return pl.BlockSpec(shape, lambda *_: (0,) * nd)


def _proj_kernel(x_ref, w_ref, wt_ref, o_ref, t_ref, *, chunk):
    x = x_ref[...]
    xb, xl = _split(x)
    for c in range(0, o_ref.shape[1], chunk):
        o_ref[:, c:c + chunk] = _dot(xb, w_ref[:, c:c + chunk]).astype(o_ref.dtype)
    wth, wtl = _split(wt_ref[...])
    t_ref[...] = _dot(xb, wth) + (_dot(xb, wtl) + _dot(xl, wth))


def _proj(x, w, wt):
    n = x.shape[0]
    c = w.shape[1]
    tm = ROW_TILE
    return pl.pallas_call(
        functools.partial(_proj_kernel, chunk=512),
        grid=(n // tm,),
        in_specs=[pl.BlockSpec((tm, D_MODEL), lambda i: (i, 0)),
                  _const_spec(w.shape), _const_spec(wt.shape)],
        out_specs=[pl.BlockSpec((tm, c), lambda i: (i, 0)),
                   pl.BlockSpec((tm, LANES), lambda i: (i, 0))],
        out_shape=[jax.ShapeDtypeStruct((n, c), BF16),
                   jax.ShapeDtypeStruct((n, LANES), F32)],
        compiler_params=_cparams(("parallel",)),
        name="proj",
    )(x, w, wt)


def _mix_out_kernel(*refs, n_in):
    a_refs = refs[:n_in]
    w_ref, x_ref, g_ref, b_ref, o_ref = refs[n_in:]
    a = a_refs[0][...] if n_in == 1 else jnp.concatenate([r[...] for r in a_refs], axis=1)
    y = ALPHA * x_ref[...] + _dot(a, w_ref[...])
    o_ref[...] = _layer_norm(y, g_ref[...], b_ref[...])


def _mix_out(a_list, w, x, g, b):
    n = x.shape[0]
    tm = ROW_TILE
    in_specs = [pl.BlockSpec((tm, a.shape[1]), lambda i: (i, 0)) for a in a_list]
    in_specs += [_const_spec(w.shape), pl.BlockSpec((tm, D_MODEL), lambda i: (i, 0)),
                 _const_spec(g.shape), _const_spec(b.shape)]
    return pl.pallas_call(
        functools.partial(_mix_out_kernel, n_in=len(a_list)),
        grid=(n // tm,),
        in_specs=in_specs,
        out_specs=pl.BlockSpec((tm, D_MODEL), lambda i: (i, 0)),
        out_shape=jax.ShapeDtypeStruct((n, D_MODEL), F32),
        compiler_params=_cparams(("parallel",)),
        name="mix_out",
    )(*a_list, w, x, g, b)


def _swiglu_chunks(xb, wg_ref, wu_ref, wd_ref, chunk):
    f = wg_ref.shape[-1]
    acc = None
    for c in range(0, f, chunk):
        w = min(chunk, f - c)
        hg = _dot(xb, wg_ref[:, c:c + w].astype(BF16))
        hu = _dot(xb, wu_ref[:, c:c + w].astype(BF16))
        h = (_silu(hg) * hu).astype(BF16)
        part = _dot(h, wd_ref[c:c + w, :].astype(BF16))
        acc = part if acc is None else acc + part
    return acc


def _ffn_kernel(x_ref, wg_ref, wu_ref, wd_ref, g_ref, b_ref, o_ref):
    x = x_ref[...]
    y = _swiglu_chunks(x.astype(BF16), wg_ref, wu_ref, wd_ref, 512)
    o_ref[...] = _layer_norm(ALPHA * x + y, g_ref[...], b_ref[...])


def _ffn(x, wg, wu, wd, g, b):
    n = x.shape[0]
    tm = ROW_TILE
    single = pl.Buffered(1)
    return pl.pallas_call(
        _ffn_kernel,
        grid=(n // tm,),
        in_specs=[pl.BlockSpec((tm, D_MODEL), lambda i: (i, 0)),
                  pl.BlockSpec(wg.shape, lambda i: (0, 0), pipeline_mode=single),
                  pl.BlockSpec(wu.shape, lambda i: (0, 0), pipeline_mode=single),
                  pl.BlockSpec(wd.shape, lambda i: (0, 0), pipeline_mode=single),
                  _const_spec(g.shape), _const_spec(b.shape)],
        out_specs=pl.BlockSpec((tm, D_MODEL), lambda i: (i, 0)),
        out_shape=jax.ShapeDtypeStruct((n, D_MODEL), F32),
        compiler_params=_cparams(("parallel",)),
        name="ffn",
    )(x, wg, wu, wd, g, b)


def _experts_kernel(te_ref, nv_ref, nu_ref, xs_ref, wg_ref, wu_ref, wd_ref, o_ref, wg16_ref, wu16_ref, wd16_ref):
    i = pl.program_id(0)
    j = pl.program_id(1)
    valid = nv_ref[i]

    @pl.when(j == 0)
    def _():
        o_ref[...] = jnp.zeros_like(o_ref)

    @pl.when(valid > 0)
    def _():
        wg16_ref[...] = wg_ref[...].astype(BF16)
        wu16_ref[...] = wu_ref[...].astype(BF16)
        wd16_ref[...] = wd_ref[...].astype(BF16)

    def accumulate(n_sub):
        for s in range(n_sub):
            rows = slice(s * MOE_SUBTILE, (s + 1) * MOE_SUBTILE)
            o_ref[rows, :] += _swiglu_chunks(xs_ref[rows, :].astype(BF16), wg16_ref, wu16_ref, wd16_ref, 256)

    n_sub_max = MOE_TILE // MOE_SUBTILE
    for n_sub in range(1, n_sub_max + 1):
        lo = (n_sub - 1) * MOE_SUBTILE
        hi = n_sub * MOE_SUBTILE
        pl.when((valid > lo) & (valid <= hi))(functools.partial(accumulate, n_sub))


def _experts(tile_expert, tile_valid, n_used, xs, wg, wu, wd, layer):
    npad = xs.shape[0]
    tm = MOE_TILE
    fb = EXPERT_DIM // MOE_FSPLIT
    last = MOE_FSPLIT - 1

    def tile(i, nu):
        return jnp.minimum(i, jnp.maximum(nu[0] - 1, 0))

    def fblock(i, j, nu):
        ju = jnp.where(i < nu[0], j, last)
        return ju + (tile(i, nu) % 2) * (last - 2 * ju)

    grid_spec = pltpu.PrefetchScalarGridSpec(
        num_scalar_prefetch=3,
        grid=(npad // tm, MOE_FSPLIT),
        in_specs=[pl.BlockSpec((tm, D_MODEL), lambda i, j, te, nv, nu: (tile(i, nu), 0)),
                  pl.BlockSpec((None, None, D_MODEL, fb),
                               lambda i, j, te, nv, nu: (layer, te[i], 0, fblock(i, j, nu))),
                  pl.BlockSpec((None, None, D_MODEL, fb),
                               lambda i, j, te, nv, nu: (layer, te[i], 0, fblock(i, j, nu))),
                  pl.BlockSpec((None, None, fb, D_MODEL),
                               lambda i, j, te, nv, nu: (layer, te[i], fblock(i, j, nu), 0))],
        out_specs=pl.BlockSpec((tm, D_MODEL), lambda i, j, te, nv, nu: (i, 0)),
        scratch_shapes=[pltpu.VMEM((D_MODEL, fb), BF16), pltpu.VMEM((D_MODEL, fb), BF16),
                        pltpu.VMEM((fb, D_MODEL), BF16)],
    )
    return pl.pallas_call(
        _experts_kernel,
        grid_spec=grid_spec,
        out_shape=jax.ShapeDtypeStruct((npad, D_MODEL), F32),
        compiler_params=_cparams(("arbitrary", "arbitrary")),
        name="experts",
    )(tile_expert, tile_valid, n_used, xs, wg, wu, wd)


def _router_kernel(x_ref, wr_ref, br_ref, ltri_ref, o_ref, cnt_ref, carry_ref):
    @pl.when(pl.program_id(0) == 0)
    def _():
        carry_ref[...] = jnp.zeros_like(carry_ref)

    xh, xl = _split(x_ref[...])
    wh, wl = _split(wr_ref[...])
    logits = _dot(xh, wh) + (_dot(xh, wl) + _dot(xl, wh)) + br_ref[...]
    lane = lax.broadcasted_iota(jnp.int32, logits.shape, 1)
    m1 = jnp.max(logits, axis=-1, keepdims=True)
    i1 = jnp.min(jnp.where(logits == m1, lane, LANES), axis=-1, keepdims=True)
    rest = jnp.where(lane == i1, -jnp.inf, logits)
    m2 = jnp.max(rest, axis=-1, keepdims=True)
    i2 = jnp.min(jnp.where(rest == m2, lane, LANES), axis=-1, keepdims=True)
    e2 = jnp.exp(m2 - m1)
    g1 = 1.0 / (1.0 + e2)
    g2 = e2 / (1.0 + e2)
    sel1 = lane == i1
    sel2 = lane == i2
    onehot = jnp.where(sel1, 1.0, jnp.where(sel2, 1.0, 0.0))
    before = _dot(ltri_ref[...], onehot.astype(BF16)) + carry_ref[0:1, :]
    r1 = jnp.sum(jnp.where(sel1, before, 0.0), axis=-1, keepdims=True)
    r2 = jnp.sum(jnp.where(sel2, before, 0.0), axis=-1, keepdims=True)
    new_carry = carry_ref[0:1, :] + jnp.sum(onehot, axis=0, keepdims=True)
    carry_ref[...] = jnp.broadcast_to(new_carry, carry_ref.shape)
    cnt_ref[...] = jnp.broadcast_to(new_carry, cnt_ref.shape)
    vals = (i1.astype(F32), i2.astype(F32), r1, r2, g1, g2)
    out = jnp.zeros(logits.shape, F32)
    for k, v in enumerate(vals):
        out = jnp.where(lane == k, v, out)
    o_ref[...] = out


def _router(x, wr, br):
    n = x.shape[0]
    tm = ROW_TILE
    ltri = jnp.asarray(np.tril(np.ones((tm, tm), np.float32), -1), BF16)
    return pl.pallas_call(
        _router_kernel,
        grid=(n // tm,),
        in_specs=[pl.BlockSpec((tm, D_MODEL), lambda i: (i, 0)),
                  _const_spec(wr.shape), _const_spec(br.shape), _const_spec(ltri.shape)],
        out_specs=[pl.BlockSpec((tm, LANES), lambda i: (i, 0)),
                   pl.BlockSpec((HALO, LANES), lambda i: (0, 0))],
        out_shape=[jax.ShapeDtypeStruct((n, LANES), F32),
                   jax.ShapeDtypeStruct((HALO, LANES), F32)],
        scratch_shapes=[pltpu.VMEM((HALO, LANES), F32)],
        compiler_params=_cparams(("arbitrary",)),
        name="router",
    )(x, wr, br, ltri)


def _dispatch_kernel(slot_ref, x_ref, xs_in_ref, xs_ref, sem):
    del xs_in_ref
    tg = x_ref.shape[0]

    def issue(r, carry):
        for k in range(2):
            pltpu.make_async_copy(x_ref.at[pl.ds(r, 1)], xs_ref.at[pl.ds(slot_ref[0, 0, 2 * r + k], 1)],
                                  sem).start()
        return carry

    lax.fori_loop(0, tg, issue, 0, unroll=8)
    for k in range(2):
        pltpu.make_async_copy(x_ref, xs_ref.at[pl.ds(0, tg)], sem).wait()


def _dispatch(x, slots, npad):
    n = x.shape[0]
    tg = GATHER_TILE
    slots3 = slots.reshape(n // tg, 1, 2 * tg)
    zeros = jnp.zeros((npad, D_MODEL), F32)
    return pl.pallas_call(
        _dispatch_kernel,
        grid=(n // tg,),
        in_specs=[pl.BlockSpec((1, 1, 2 * tg), lambda i: (i, 0, 0), memory_space=pltpu.SMEM),
                  pl.BlockSpec((tg, D_MODEL), lambda i: (i, 0)),
                  pl.BlockSpec(memory_space=pl.ANY)],
        out_specs=pl.BlockSpec(memory_space=pl.ANY),
        out_shape=jax.ShapeDtypeStruct((npad, D_MODEL), F32),
        scratch_shapes=[pltpu.SemaphoreType.DMA(())],
        input_output_aliases={2: 0},
        compiler_params=_cparams(("arbitrary",)),
        name="moe_dispatch",
    )(slots3, x, zeros)


def _combine_kernel(slot_ref, ys_ref, x_ref, info_ref, g_ref, b_ref, o_ref, buf_ref, sem):
    tg = x_ref.shape[0]

    def issue(r, carry):
        for k in range(2):
            pltpu.make_async_copy(ys_ref.at[pl.ds(slot_ref[0, 0, 2 * r + k], 1)],
                                  buf_ref.at[k, pl.ds(r, 1)], sem).start()
        return carry

    lax.fori_loop(0, tg, issue, 0, unroll=8)
    for k in range(2):
        pltpu.make_async_copy(ys_ref.at[pl.ds(0, tg)], buf_ref.at[k], sem).wait()
    info = info_ref[...]
    y = info[:, 4:5] * buf_ref[0] + info[:, 5:6] * buf_ref[1]
    o_ref[...] = _layer_norm(ALPHA * x_ref[...] + y, g_ref[...], b_ref[...])


def _combine(ys, slots, x, info, g, b):
    n = x.shape[0]
    tg = GATHER_TILE
    slots3 = slots.reshape(n // tg, 1, 2 * tg)
    return pl.pallas_call(
        _combine_kernel,
        grid=(n // tg,),
        in_specs=[pl.BlockSpec((1, 1, 2 * tg), lambda i: (i, 0, 0), memory_space=pltpu.SMEM),
                  pl.BlockSpec(memory_space=pl.ANY),
                  pl.BlockSpec((tg, D_MODEL), lambda i: (i, 0)),
                  pl.BlockSpec((tg, LANES), lambda i: (i, 0)),
                  _const_spec(g.shape), _const_spec(b.shape)],
        out_specs=pl.BlockSpec((tg, D_MODEL), lambda i: (i, 0)),
        out_shape=jax.ShapeDtypeStruct((n, D_MODEL), F32),
        scratch_shapes=[pltpu.VMEM((2, tg, D_MODEL), F32), pltpu.SemaphoreType.DMA(())],
        compiler_params=_cparams(("arbitrary",)),
        name="moe_combine",
    )(slots3, ys, x, info, g, b)


def _moe(x, wr, br, wg, wu, wd, layer, g, b):
    n = x.shape[0]
    tm = MOE_TILE
    info, counts = _router(x, wr, br)
    experts = info[:, 0:2].astype(jnp.int32)
    ranks = info[:, 2:4].astype(jnp.int32)
    count = counts[0, :N_EXPERTS].astype(jnp.int32)
    tiles = (count + tm - 1) // tm
    tile_end = jnp.cumsum(tiles)
    tile_start = tile_end - tiles
    slots = tile_start[experts] * tm + ranks
    n_tiles = (2 * n) // tm + N_EXPERTS
    n_used = tile_end[-1]
    tile_ids = jnp.arange(n_tiles, dtype=jnp.int32)
    tile_id = jnp.minimum(tile_ids, n_used - 1)
    tile_expert = jnp.sum((tile_id[:, None] >= tile_end[None, :]).astype(jnp.int32), axis=1)
    tile_valid = jnp.clip(count[tile_expert] - (tile_ids - tile_start[tile_expert]) * tm, 0, tm)
    tile_valid = jnp.where(tile_ids < n_used, tile_valid, 0)
    xs = _dispatch(x, slots, n_tiles * tm)
    ys = _experts(tile_expert.astype(jnp.int32), tile_valid.astype(jnp.int32),
                  n_used.reshape(1).astype(jnp.int32), xs, wg, wu, wd, layer)
    return _combine(ys, slots, x, info, g, b)


def _rope(x, c, sa, sb):
    return x * c + pltpu.roll(x, LANES - ROPE_DIM // 2, 1) * sa + pltpu.roll(x, ROPE_DIM // 2, 1) * sb


def _swa_kernel(sink_ref, q_ref, k_ref, v_ref, c_ref, sa_ref, sb_ref, o_ref, kp_ref, vp_ref):
    n = pl.program_id(1)
    blk = SWA_BLOCK

    @pl.when(n == 0)
    def _():
        kp_ref[...] = jnp.zeros_like(kp_ref)
        vp_ref[...] = jnp.zeros_like(vp_ref)

    c = c_ref[...]
    sa = sa_ref[...]
    sb = sb_ref[...]
    lane = lax.broadcasted_iota(jnp.int32, (blk, LANES), 1)
    low = lane < SWA_HEAD_DIM
    row = lax.broadcasted_iota(jnp.int32, (blk, blk), 0)
    col = lax.broadcasted_iota(jnp.int32, (blk, blk), 1)
    mask_cur = col <= row
    zero = jnp.zeros((blk, LANES), BF16)
    nq = q_ref.shape[0] // blk

    def halves(t, g):
        swapped = jnp.concatenate([t[:, SWA_HEAD_DIM:], t[:, :SWA_HEAD_DIM]], axis=1)
        src_a, src_b = (t, swapped) if g == 0 else (swapped, t)
        return jnp.where(low, src_a, zero), jnp.where(low, zero, src_b)

    both = lambda t: [halves(t, g) for g in range(SWA_KV_HEADS)]
    k_blocks = [kp_ref[...]] + [_rope(k_ref[i * blk:(i + 1) * blk, :].astype(F32), c[i * blk:(i + 1) * blk],
                                      sa[i * blk:(i + 1) * blk], sb[i * blk:(i + 1) * blk]).astype(BF16)
                                for i in range(nq)]
    v_blocks = [vp_ref[...]] + [v_ref[i * blk:(i + 1) * blk, :] for i in range(nq)]
    k_halves = [both(t) for t in k_blocks]
    v_halves = [both(t) for t in v_blocks]

    group = SWA_HEADS // SWA_KV_HEADS
    entries = [(i, h) for i in range(nq) for h in range(SWA_HEADS)]
    stack = lambda f: jnp.stack([f(i, h) for i, h in entries])
    q_pairs = [[(_rope(q_ref[i * blk:(i + 1) * blk, pr * LANES:(pr + 1) * LANES].astype(F32),
                       c[i * blk:(i + 1) * blk], sa[i * blk:(i + 1) * blk], sb[i * blk:(i + 1) * blk])
                 * (SWA_HEAD_DIM ** -0.5)).astype(BF16) for pr in range(SWA_HEADS // 2)] for i in range(nq)]
    q_all = stack(lambda i, h: q_pairs[i][h // 2])
    sink = stack(lambda i, h: jnp.full((1, 1), sink_ref[h], F32))
    s_prev = jnp.where(col > row, _bdot_nt(q_all, stack(lambda i, h: k_halves[i][h // group][h % 2])), -jnp.inf)
    s_first = jnp.where(n > 0, s_prev[:SWA_HEADS], -jnp.inf)
    s_prev = s_first if nq == 1 else jnp.concatenate([s_first, s_prev[SWA_HEADS:]], axis=0)
    s_cur = jnp.where(mask_cur, _bdot_nt(q_all, stack(lambda i, h: k_halves[i + 1][h // group][h % 2])), -jnp.inf)
    m = jnp.maximum(jnp.maximum(jnp.max(s_prev, axis=-1, keepdims=True),
                                jnp.max(s_cur, axis=-1, keepdims=True)), sink)
    p_prev = jnp.exp(s_prev - m)
    p_cur = jnp.exp(s_cur - m)
    denom = (jnp.sum(p_prev, axis=-1, keepdims=True) + jnp.sum(p_cur, axis=-1, keepdims=True)
             + jnp.exp(sink - m))
    inv = 1.0 / denom
    out = (_bdot((p_prev * inv).astype(BF16), stack(lambda i, h: v_halves[i][h // group][h % 2]))
           + _bdot((p_cur * inv).astype(BF16), stack(lambda i, h: v_halves[i + 1][h // group][h % 2])))
    for i in range(nq):
        for pr in range(SWA_HEADS // 2):
            e = i * SWA_HEADS + 2 * pr
            o_ref[i * blk:(i + 1) * blk, pr * LANES:(pr + 1) * LANES] = (out[e] + out[e + 1]).astype(o_ref.dtype)

    kp_ref[...] = k_blocks[nq]
    vp_ref[...] = v_blocks[nq]


def _swa(p, sinks, tables, bsz, t, q_col, k_col, v_col):
    rows = SWA_STEP_BLOCKS * SWA_BLOCK
    nb = t // rows
    c, sa, sb = tables
    row = lambda b, n: b * nb + n
    return pl.pallas_call(
        _swa_kernel,
        grid=(bsz, nb),
        in_specs=[pl.BlockSpec(memory_space=pltpu.SMEM),
                  pl.BlockSpec((rows, SWA_Q), lambda b, n: (row(b, n), q_col)),
                  pl.BlockSpec((rows, SWA_KV), lambda b, n: (row(b, n), k_col)),
                  pl.BlockSpec((rows, SWA_KV), lambda b, n: (row(b, n), v_col)),
                  pl.BlockSpec((rows, LANES), lambda b, n: (n, 0)),
                  pl.BlockSpec((rows, LANES), lambda b, n: (n, 0)),
                  pl.BlockSpec((rows, LANES), lambda b, n: (n, 0))],
        out_specs=pl.BlockSpec((rows, SWA_Q), lambda b, n: (row(b, n), 0)),
        out_shape=jax.ShapeDtypeStruct((bsz * t, SWA_Q), BF16),
        scratch_shapes=[pltpu.VMEM((SWA_BLOCK, SWA_KV), BF16), pltpu.VMEM((SWA_BLOCK, SWA_KV), BF16)],
        compiler_params=_cparams(("parallel", "arbitrary")),
        name="swa",
    )(sinks, p, p, p, c, sa, sb)


def _rope_tables(t):
    half = ROPE_DIM // 2
    pos = jnp.arange(t, dtype=jnp.int32)
    inv_freq = jnp.power(ROPE_THETA, -jnp.arange(half, dtype=F32) / half)
    ang = pos.astype(F32)[:, None] * inv_freq[None, :]
    cos = jnp.cos(ang)
    sin = jnp.sin(ang)
    ones = jnp.ones((t, SWA_HEAD_DIM - ROPE_DIM), F32)
    zeros = jnp.zeros((t, SWA_HEAD_DIM - ROPE_DIM), F32)
    zhalf = jnp.zeros((t, half), F32)
    c = jnp.concatenate([cos, cos, ones], axis=1)
    sa = jnp.concatenate([-sin, zhalf, zeros], axis=1)
    sb = jnp.concatenate([zhalf, sin, zeros], axis=1)
    two = lambda a: jnp.concatenate([a, a], axis=1)
    return two(c), two(sa), two(sb)


def _split(a):
    hi = a.astype(BF16)
    return hi, (a - hi.astype(F32)).astype(BF16)


def _bdot(a, b):
    return lax.dot_general(a, b, (((2,), (1,)), ((0,), (0,))), preferred_element_type=F32)


def _bdot_nt(a, b):
    return lax.dot_general(a, b, (((2,), (2,)), ((0,), (0,))), preferred_element_type=F32)


def _mm(a, b):
    if DN_INV_PASSES == 1:
        return _bdot(a.astype(BF16), b.astype(BF16))
    ah, al = _split(a)
    bh, bl = _split(b)
    return _bdot(ah, bh) + (_bdot(ah, bl) + _bdot(al, bh))


def _unit_lower_inverse(a, eye, blk_mask, merge_masks):
    d0 = jnp.where(blk_mask, a, 0.0)
    x = eye - d0
    p = _mm(d0, d0)
    x = x + _mm(x, p)
    p = _mm(p, p)
    x = x + _mm(x, p)
    for m in merge_masks:
        lm = jnp.where(m, a, 0.0)
        x = x - _mm(x, _mm(lm, x))
    return x


def _dn_kernel(q_ref, k_ref, v_ref, gate_ref, tail_ref, cw_ref, par_ref, nw_ref, o_ref,
               xpad_ref, qkv_ref, bg_ref, s_ref, gc_ref, u_ref, oi_ref, kd_ref, wq_ref):
    n = pl.program_id(1)
    tb = q_ref.shape[0]
    ch = DN_CHUNK
    hd = DN_HEAD_DIM

    @pl.when(n == 0)
    def _():
        xpad_ref[0:HALO, :] = jnp.zeros((HALO, 3 * DN_W), F32)
        s_ref[...] = jnp.zeros_like(s_ref)

    xpad_ref[HALO:HALO + tb, 0:DN_W] = q_ref[...].astype(F32)
    xpad_ref[HALO:HALO + tb, DN_W:2 * DN_W] = k_ref[...].astype(F32)
    xpad_ref[HALO:HALO + tb, 2 * DN_W:3 * DN_W] = v_ref[...].astype(F32)
    conv = None
    for kk in range(DN_CONV):
        off = HALO - (DN_CONV - 1) + kk
        term = cw_ref[kk:kk + 1, :] * xpad_ref[off:off + tb, :]
        conv = term if conv is None else conv + term
    xpad_ref[0:HALO, :] = xpad_ref[tb:tb + HALO, :]
    qkv = _silu(conv)
    for h in range(2 * DN_HEADS):
        xh = qkv[:, h * hd:(h + 1) * hd]
        scale = lax.rsqrt(jnp.sum(xh * xh, axis=-1, keepdims=True) + RMS_EPS)
        if h < DN_HEADS:
            scale = scale * (hd ** -0.5)
        qkv_ref[:, h * hd:(h + 1) * hd] = xh * scale
    qkv_ref[:, 2 * DN_W:] = qkv[:, 2 * DN_W:]

    tail = tail_ref[...]
    lane = lax.broadcasted_iota(jnp.int32, tail.shape, 1)
    gval = -jnp.exp(par_ref[0:1, :]) * _softplus(tail + par_ref[1:2, :])
    bg_ref[...] = jnp.where(lane < DN_HEADS, _sigmoid(tail), gval)

    rt = lax.broadcasted_iota(jnp.int32, (tb, tb), 0)
    ct = lax.broadcasted_iota(jnp.int32, (tb, tb), 1)
    ltri = jnp.where((rt >= ct) & ((rt // ch) == (ct // ch)), 1.0, 0.0).astype(F32)
    er = lax.broadcasted_iota(jnp.int32, (LANES, LANES), 0)
    ec = lax.broadcasted_iota(jnp.int32, (LANES, LANES), 1)
    eye = jnp.where(er == ec, 1.0, 0.0).astype(F32)
    bg = bg_ref[...]
    gc_all = _dot(ltri, jnp.where(lane >= DN_HEADS, bg, 0.0), HIGHEST)
    gc_t = _dot_nt(eye, gc_all, HIGHEST)
    gc_ref[...] = gc_all

    sb = 2 * ch
    same = (er // ch) == (ec // ch)
    causal = same & (er >= ec)
    strict = same & (er > ec)
    blk_mask = (er > ec) & ((er // 8) == (ec // 8))
    merge_masks = [((er // (2 * s)) == (ec // (2 * s))) & ((er % (2 * s)) >= s) & ((ec % (2 * s)) < s)
                   for s in (8, 16, 32)]
    first = lax.broadcasted_iota(jnp.int32, (sb, 1), 0) < ch
    probs = [(h, b2) for h in range(DN_HEADS) for b2 in range(tb // sb)]
    rows_of = lambda b2: slice(b2 * sb, (b2 + 1) * sb)
    stack = lambda f: jnp.stack([f(h, b2) for h, b2 in probs])
    gcol = stack(lambda h, b2: gc_all[rows_of(b2), DN_HEADS + h:DN_HEADS + h + 1])
    grow = stack(lambda h, b2: gc_t[DN_HEADS + h:DN_HEADS + h + 1, rows_of(b2)])
    beta = stack(lambda h, b2: bg[rows_of(b2), h:h + 1])
    q = stack(lambda h, b2: qkv_ref[rows_of(b2), h * hd:(h + 1) * hd])
    k = stack(lambda h, b2: qkv_ref[rows_of(b2), DN_W + h * hd:DN_W + (h + 1) * hd])
    v = stack(lambda h, b2: qkv_ref[rows_of(b2), 2 * DN_W + h * hd:2 * DN_W + (h + 1) * hd])
    decay = jnp.where(causal, jnp.exp(gcol - grow), 0.0)
    k16 = k.astype(BF16)
    a = jnp.where(strict, _bdot_nt(k16, k16) * decay * beta, 0.0)
    tinv = _unit_lower_inverse(a, eye, blk_mask, merge_masks)
    egc = jnp.exp(gcol)
    u = _mm(tinv, v * beta)
    w16 = _mm(tinv, k * (beta * egc)).astype(BF16)
    attn16 = (_bdot_nt(q.astype(BF16), k16) * decay).astype(BF16)
    q_eff16 = (q * egc - _bdot(attn16, w16)).astype(BF16)
    o_intra = _bdot(attn16, u.astype(BF16))
    glast = jnp.where(first, gcol[:, ch - 1:ch, :], gcol[:, sb - 1:sb, :])
    k_dec16 = (k * jnp.exp(glast - gcol)).astype(BF16)
    for g, (h, b2) in enumerate(probs):
        u_ref[h, rows_of(b2), :] = u[g]
        oi_ref[h, rows_of(b2), :] = o_intra[g]
        kd_ref[h, rows_of(b2), :] = k_dec16[g]
        for c in range(2):
            wq_ref[h, 2 * b2 + c, 0:ch, :] = w16[g, c * ch:(c + 1) * ch]
            wq_ref[h, 2 * b2 + c, ch:sb, :] = q_eff16[g, c * ch:(c + 1) * ch]

    for c in range(tb // ch):
        rows = slice(c * ch, (c + 1) * ch)
        s = s_ref[...]
        r = _bdot(wq_ref[:, c], s.astype(BF16))
        v_new16 = (u_ref[:, rows, :] - r[:, 0:ch]).astype(BF16)
        for h in range(DN_HEADS):
            glast = gc_ref[(c + 1) * ch - 1:(c + 1) * ch, DN_HEADS + h:DN_HEADS + h + 1]
            s_ref[h] = s[h] * jnp.exp(glast) + _dot_tn(kd_ref[h, rows, :], v_new16[h])
        for h in range(DN_HEADS):
            o = r[h, ch:sb] + oi_ref[h, rows, :]
            gate = gate_ref[rows, h * hd:(h + 1) * hd].astype(F32)
            on = o * lax.rsqrt(jnp.mean(o * o, axis=-1, keepdims=True) + RMS_EPS) * nw_ref[...]
            o_ref[rows, h * hd:(h + 1) * hd] = (on * _silu(gate)).astype(o_ref.dtype)


def _deltanet(p, tail, conv_w, par, norm_w, bsz, t, cols):
    tb = DN_BLOCK
    nb = t // tb
    row = lambda b, n: b * nb + n
    col_spec = lambda cidx: pl.BlockSpec((tb, DN_W), lambda b, n: (row(b, n), cidx))
    return pl.pallas_call(
        _dn_kernel,
        grid=(bsz, nb),
        in_specs=[col_spec(cols[0]), col_spec(cols[1]), col_spec(cols[2]), col_spec(cols[3]),
                  pl.BlockSpec((tb, LANES), lambda b, n: (row(b, n), 0)),
                  _const_spec(conv_w.shape), _const_spec(par.shape), _const_spec(norm_w.shape)],
        out_specs=pl.BlockSpec((tb, DN_W), lambda b, n: (row(b, n), 0)),
        out_shape=jax.ShapeDtypeStruct((bsz * t, DN_W), BF16),
        scratch_shapes=[pltpu.VMEM((tb + HALO, 3 * DN_W), F32),
                        pltpu.VMEM((tb, 3 * DN_W), F32),
                        pltpu.VMEM((tb, LANES), F32),
                        pltpu.VMEM((DN_HEADS, DN_HEAD_DIM, DN_HEAD_DIM), F32),
                        pltpu.VMEM((tb, LANES), F32),
                        pltpu.VMEM((DN_HEADS, tb, DN_HEAD_DIM), F32),
                        pltpu.VMEM((DN_HEADS, tb, DN_HEAD_DIM), F32),
                        pltpu.VMEM((DN_HEADS, tb, DN_HEAD_DIM), BF16),
                        pltpu.VMEM((DN_HEADS, tb // DN_CHUNK, 2 * DN_CHUNK, DN_HEAD_DIM), BF16)],
        compiler_params=_cparams(("parallel", "arbitrary")),
        name="deltanet",
    )(p, p, p, p, tail, conv_w, par, norm_w)


def _ssd_kernel(z_ref, xs_ref, bc_ref, dt_ref, cw_ref, cb_ref, pcol_ref, prow_ref,
                dskip_ref, nw_ref, expand_ref, o_ref, xpad_ref, xc_ref, h_ref):
    n = pl.program_id(1)
    L = SSM_CHUNK

    @pl.when(n == 0)
    def _():
        xpad_ref[0:L, :] = jnp.zeros((L, SSM_CONV_DIM), BF16)
        h_ref[...] = jnp.zeros_like(h_ref)

    xpad_ref[L:2 * L, 0:SSM_D_INNER] = xs_ref[...]
    xpad_ref[L:2 * L, SSM_D_INNER:] = bc_ref[...]
    sr = lax.broadcasted_iota(jnp.int32, (L, 2 * L), 0)
    sc = lax.broadcasted_iota(jnp.int32, (L, 2 * L), 1)
    shifts = [jnp.where(sc == sr + (L - 3 + kk), 1.0, 0.0).astype(BF16) for kk in range(3)]
    cw_blk = 512
    for c0 in range(0, SSM_CONV_DIM, cw_blk):
        cols = slice(c0, c0 + cw_blk)
        xe = xpad_ref[:, cols]
        conv = cb_ref[:, cols] + cw_ref[3:4, cols] * xe[L:2 * L].astype(F32)
        for kk in range(3):
            conv = conv + cw_ref[kk:kk + 1, cols] * _dot(shifts[kk], xe)
        xc_ref[:, cols] = _silu(conv)
    xpad_ref[0:L, :] = xpad_ref[L:2 * L, :]

    ri = lax.broadcasted_iota(jnp.int32, (L, L), 0)
    ci = lax.broadcasted_iota(jnp.int32, (L, L), 1)
    causal = ri >= ci
    ltri = jnp.where(causal, 1.0, 0.0).astype(F32)
    utri = jnp.where(ri <= ci, 1.0, 0.0).astype(F32)
    eye = jnp.where(ri == ci, 1.0, 0.0).astype(F32)

    dt_raw = dt_ref[...]
    dt = _softplus(dt_raw + pcol_ref[0:1, :])
    acs = _dot(ltri, dt * pcol_ref[1:2, :], HIGHEST)
    tot = acs[L - 1:L, :]
    f1 = dt * jnp.exp(tot - acs)
    ea = jnp.exp(acs)
    dt_r = _softplus(_dot_nt(eye, dt_raw, HIGHEST)[0:SSM_HEADS] + prow_ref[0])
    acs_r = _dot(dt_r * prow_ref[1], utri, HIGHEST)

    stacked = jnp.concatenate([dt, f1, ea], axis=0).astype(BF16)
    cd = jnp.broadcast_to(jnp.exp(tot), (HALO, LANES))
    cd_hi, cd_lo = _split(cd)
    lane = lax.broadcasted_iota(jnp.int32, (L, LANES), 1)
    low = lane < SSM_HEAD_DIM
    zero16 = jnp.zeros((L, LANES), BF16)
    for g in range(SSM_GROUPS):
        gl = slice(g * SSM_GROUP_W, (g + 1) * SSM_GROUP_W)
        ex = expand_ref[:, gl]
        wide = _dot(stacked, ex)
        cd_x = (_dot(cd_hi, ex) + _dot(cd_lo, ex))[0:1, :]
        xs = xc_ref[:, gl]
        xdt16 = (xs * wide[0:L]).astype(BF16)
        xdd16 = (xs * wide[L:2 * L]).astype(BF16)
        bm = xc_ref[:, SSM_D_INNER + g * SSM_STATE:SSM_D_INNER + (g + 1) * SSM_STATE].astype(BF16)
        cm = xc_ref[:, SSM_D_INNER + SSM_GN + g * SSM_STATE:
                    SSM_D_INNER + SSM_GN + (g + 1) * SSM_STATE].astype(BF16)
        cb = _dot_nt(cm, bm)
        hg = h_ref[g]
        y_off = _dot(cm, hg.astype(BF16)) * wide[2 * L:3 * L]
        h_ref[g] = hg * cd_x + _dot_tn(bm, xdd16)
        parts = []
        for j in range(SSM_GROUP_W // LANES):
            xp16 = xdt16[:, j * LANES:(j + 1) * LANES]
            yd = None
            for side in range(2):
                hidx = (g * SSM_GROUP_W + j * LANES) // SSM_HEAD_DIM + side
                seg = jnp.where(causal, jnp.exp(acs[:, hidx:hidx + 1] - acs_r[hidx:hidx + 1, :]), 0.0)
                m = (cb * seg).astype(BF16)
                xm = jnp.where(low, xp16, zero16) if side == 0 else jnp.where(low, zero16, xp16)
                part = _dot(m, xm)
                yd = part if yd is None else yd + part
            parts.append(yd)
        y = jnp.concatenate(parts, axis=1) + y_off + dskip_ref[:, gl] * xs
        y = y * _silu(z_ref[:, gl].astype(F32))
        yn = y * lax.rsqrt(jnp.mean(y * y, axis=-1, keepdims=True) + RMS_EPS) * nw_ref[:, gl]
        o_ref[:, gl] = yn.astype(o_ref.dtype)


def _ssd(p, dt, conv_w, conv_b, pcol, prow, dskip, norm_w, expand, bsz, t):
    L = SSM_CHUNK
    nb = t // L
    row = lambda b, n: b * nb + n
    return pl.pallas_call(
        _ssd_kernel,
        grid=(bsz, nb),
        in_specs=[pl.BlockSpec((L, SSM_D_INNER), lambda b, n: (row(b, n), 0)),
                  pl.BlockSpec((L, SSM_D_INNER), lambda b, n: (row(b, n), 1)),
                  pl.BlockSpec((L, 2 * SSM_GN), lambda b, n: (row(b, n), 2 * SSM_D_INNER // (2 * SSM_GN))),
                  pl.BlockSpec((L, LANES), lambda b, n: (row(b, n), 0)),
                  _const_spec(conv_w.shape), _const_spec(conv_b.shape), _const_spec(pcol.shape),
                  _const_spec(prow.shape), _const_spec(dskip.shape), _const_spec(norm_w.shape),
                  _const_spec(expand.shape)],
        out_specs=pl.BlockSpec((L, SSM_D_INNER), lambda b, n: (row(b, n), 0)),
        out_shape=jax.ShapeDtypeStruct((bsz * t, SSM_D_INNER), BF16),
        scratch_shapes=[pltpu.VMEM((2 * L, SSM_CONV_DIM), BF16),
                        pltpu.VMEM((L, SSM_CONV_DIM), F32),
                        pltpu.VMEM((SSM_GROUPS, SSM_STATE, SSM_GROUP_W), F32)],
        compiler_params=_cparams(("parallel", "arbitrary")),
        name="ssd",
    )(p, p, p, dt, conv_w, conv_b, pcol, prow, dskip, norm_w, expand)


def _pad_lanes(v, offset=0):
    v = v.astype(F32)
    return jnp.zeros((1, LANES), F32).at[0, offset:offset + v.shape[0]].set(v)


def _even_layer(x, bsz, t, w_in, sinks, conv_w, a_log, dt_bias, norm_w, w_out, ln_g, ln_b,
                ffn_wg, ffn_wu, ffn_wd, rope_tables):
    o = np.cumsum((0, SWA_Q, SWA_KV, SWA_KV, DN_W, DN_W, DN_W, DN_W, DN_HEADS, DN_HEADS))
    seg = lambda i: w_in[:, o[i]:o[i + 1]]
    w_main = jnp.concatenate([seg(0), seg(3), seg(4), seg(5), seg(6), seg(1), seg(2)], axis=1).astype(BF16)
    w_tail = w_in[:, o[7]:o[9]]
    wt = jnp.zeros((D_MODEL, LANES), F32).at[:, :2 * DN_HEADS].set(w_tail)
    p, tail = _proj(x, w_main, wt)
    out_a = _swa(p, sinks.astype(F32), rope_tables, bsz, t,
                 q_col=0, k_col=(SWA_Q + 4 * DN_W) // SWA_KV, v_col=(SWA_Q + 4 * DN_W) // SWA_KV + 1)
    par = jnp.concatenate([_pad_lanes(a_log, DN_HEADS), _pad_lanes(dt_bias, DN_HEADS)], axis=0)
    out_b = _deltanet(p, tail, conv_w.astype(F32), par, norm_w.reshape(1, DN_HEAD_DIM).astype(F32),
                      bsz, t, cols=(1, 2, 3, 4))
    x = _mix_out([out_a, out_b], w_out.astype(BF16), x, ln_g[0:1], ln_b[0:1])
    return _ffn(x, ffn_wg.astype(BF16), ffn_wu.astype(BF16), ffn_wd.astype(BF16), ln_g[1:2], ln_b[1:2])


def _odd_layer(x, bsz, t, w_in, conv_w, conv_b, dt_bias, a_log, d_skip, norm_w, w_out, ln_g, ln_b,
               w_router, b_router, moe_wg, moe_wu, moe_wd, moe_layer):
    main = SSM_D_INNER + SSM_CONV_DIM
    w_tail = w_in[:, main:]
    wt = jnp.zeros((D_MODEL, LANES), F32).at[:, :SSM_HEADS].set(w_tail)
    p, dt = _proj(x, w_in[:, :main].astype(BF16), wt)
    a = -jnp.exp(a_log.astype(F32))
    pcol = jnp.concatenate([_pad_lanes(dt_bias), _pad_lanes(a)], axis=0)
    prow = jnp.stack([jnp.broadcast_to(dt_bias.astype(F32)[:, None], (SSM_HEADS, SSM_CHUNK)),
                      jnp.broadcast_to(a[:, None], (SSM_HEADS, SSM_CHUNK))])
    dskip = jnp.repeat(d_skip.astype(F32), SSM_HEAD_DIM)[None, :]
    expand = (jnp.arange(LANES)[:, None] == (jnp.arange(SSM_D_INNER) // SSM_HEAD_DIM)[None, :]).astype(BF16)
    y = _ssd(p, dt, conv_w.astype(F32), conv_b.astype(F32)[None, :], pcol, prow, dskip,
             norm_w.astype(F32)[None, :], expand, bsz, t)
    x = _mix_out([y], w_out.astype(BF16), x, ln_g[0:1], ln_b[0:1])
    wr = jnp.zeros((D_MODEL, LANES), F32).at[:, :N_EXPERTS].set(w_router)
    br = jnp.full((1, LANES), -1e30, F32).at[0, :N_EXPERTS].set(b_router.astype(F32))
    return _moe(x, wr, br, moe_wg, moe_wu, moe_wd, moe_layer, ln_g[1:2], ln_b[1:2])


def kernel(x, ln_g, ln_b, even_w_in, swa_sinks, dn_conv_w, dn_a_log, dn_dt_bias, dn_norm_w, even_w_out,
           ssm_w_in, ssm_conv_w, ssm_conv_b, ssm_dt_bias, ssm_a_log, ssm_d, ssm_norm_w, ssm_w_out,
           ffn_w_gate, ffn_w_up, ffn_w_down, moe_w_router, moe_b_router, moe_w_gate, moe_w_up, moe_w_down):
    bsz, t, d = x.shape
    h = x.reshape(bsz * t, d)
    rope_tables = _rope_tables(t)
    for i in range(DEPTH):
        j = i // 2
        if i % 2 == 0:
            h = _even_layer(h, bsz, t, even_w_in[j], swa_sinks[j], dn_conv_w[j], dn_a_log[j], dn_dt_bias[j],
                            dn_norm_w[j], even_w_out[j], ln_g[i], ln_b[i],
                            ffn_w_gate[j], ffn_w_up[j], ffn_w_down[j], rope_tables)
        else:
            h = _odd_layer(h, bsz, t, ssm_w_in[j], ssm_conv_w[j], ssm_conv_b[j], ssm_dt_bias[j], ssm_a_log[j],
                           ssm_d[j], ssm_norm_w[j], ssm_w_out[j], ln_g[i], ln_b[i],
                           moe_w_router[j], moe_b_router[j], moe_w_gate, moe_w_up, moe_w_down, j)
    return h.reshape(bsz, t, d)
```

```python
import functools

import numpy as np
import jax
import jax.numpy as jnp
from jax import lax
from jax.experimental import pallas as pl
from jax.experimental.pallas import tpu as pltpu

F32 = jnp.float32
BF16 = jnp.bfloat16
HIGHEST = lax.Precision.HIGHEST

D_MODEL = 1024
DEPTH = 4
ALPHA = (2 * DEPTH) ** 0.25
LN_EPS = 1e-5
RMS_EPS = 1e-6

SWA_HEADS = 8
SWA_KV_HEADS = 2
SWA_HEAD_DIM = 64
SWA_BLOCK = 128
ROPE_DIM = SWA_HEAD_DIM // 4
ROPE_THETA = 500000.0
SWA_Q = SWA_HEADS * SWA_HEAD_DIM
SWA_KV = SWA_KV_HEADS * SWA_HEAD_DIM

DN_HEADS = 4
DN_HEAD_DIM = 128
DN_CONV = 4
DN_CHUNK = 64
DN_W = DN_HEADS * DN_HEAD_DIM

SSM_D_INNER = 2 * D_MODEL
SSM_HEAD_DIM = 64
SSM_HEADS = SSM_D_INNER // SSM_HEAD_DIM
SSM_GROUPS = 4
SSM_STATE = 128
SSM_CHUNK = 128
SSM_GN = SSM_GROUPS * SSM_STATE
SSM_CONV_DIM = SSM_D_INNER + 2 * SSM_GN
SSM_GROUP_W = SSM_D_INNER // SSM_GROUPS

FFN_DIM = 2816
N_EXPERTS = 8
EXPERT_DIM = 3584

LANES = 128
HALO = 8
VMEM_LIMIT = 52 * 1024 * 1024

ROW_TILE = 512
MOE_TILE = 1024
MOE_SUBTILE = 512
MOE_FSPLIT = 7
SWA_STEP_BLOCKS = 2
DN_BLOCK = 256
DN_INV_PASSES = 1
GATHER_TILE = 256


def _cparams(sem):
    return pltpu.CompilerParams(dimension_semantics=sem, vmem_limit_bytes=VMEM_LIMIT)


def _sigmoid(x):
    return 1.0 / (1.0 + jnp.exp(-x))


def _silu(x):
    hx = 0.5 * x
    return hx + hx * jnp.tanh(hx)


def _softplus(x):
    return jnp.maximum(x, 0.0) + jnp.log(1.0 + jnp.exp(-jnp.abs(x)))


def _layer_norm(y, g, b):
    mu = jnp.mean(y, axis=-1, keepdims=True)
    d = y - mu
    var = jnp.mean(d * d, axis=-1, keepdims=True)
    return d * lax.rsqrt(var + LN_EPS) * g + b


def _dot(a, b, precision=None):
    return jnp.dot(a, b, preferred_element_type=F32, precision=precision)


def _dot_nt(a, b, precision=None):
    return lax.dot_general(a, b, (((1,), (1,)), ((), ())), preferred_element_type=F32, precision=precision)


def _dot_tn(a, b, precision=None):
    return lax.dot_general(a, b, (((0,), (0,)), ((), ())), preferred_element_type=F32, precision=precision)


def _const_spec(shape):
    nd = len(shape)
    return pl.BlockSpec(shape, lambda *_: (0,) * nd)


def _proj_kernel(x_ref, w_ref, wt_ref, o_ref, t_ref, *, chunk):
    x = x_ref[...]
    xb, xl = _split(x)
    for c in range(0, o_ref.shape[1], chunk):
        o_ref[:, c:c + chunk] = _dot(xb, w_ref[:, c:c + chunk]).astype(o_ref.dtype)
    wth, wtl = _split(wt_ref[...])
    t_ref[...] = _dot(xb, wth) + (_dot(xb, wtl) + _dot(xl, wth))


def _proj(x, w, wt):
    n = x.shape[0]
    c = w.shape[1]
    tm = ROW_TILE
    return pl.pallas_call(
        functools.partial(_proj_kernel, chunk=512),
        grid=(n // tm,),
        in_specs=[pl.BlockSpec((tm, D_MODEL), lambda i: (i, 0)),
                  _const_spec(w.shape), _const_spec(wt.shape)],
        out_specs=[pl.BlockSpec((tm, c), lambda i: (i, 0)),
                   pl.BlockSpec((tm, LANES), lambda i: (i, 0))],
        out_shape=[jax.ShapeDtypeStruct((n, c), BF16),
                   jax.ShapeDtypeStruct((n, LANES), F32)],
        compiler_params=_cparams(("parallel",)),
        name="proj",
    )(x, w, wt)


def _mix_out_kernel(*refs, n_in):
    a_refs = refs[:n_in]
    w_ref, x_ref, g_ref, b_ref, o_ref = refs[n_in:]
    half = x_ref.shape[0] // 2
    for r0 in (0, half):
        rows = slice(r0, r0 + half)
        a = a_refs[0][rows, :] if n_in == 1 else jnp.concatenate([r[rows, :] for r in a_refs], axis=1)
        y = ALPHA * x_ref[rows, :] + _dot(a, w_ref[...])
        o_ref[rows, :] = _layer_norm(y, g_ref[...], b_ref[...])


def _mix_out(a_list, w, x, g, b):
    n = x.shape[0]
    tm = ROW_TILE
    in_specs = [pl.BlockSpec((tm, a.shape[1]), lambda i: (i, 0)) for a in a_list]
    in_specs += [_const_spec(w.shape), pl.BlockSpec((tm, D_MODEL), lambda i: (i, 0)),
                 _const_spec(g.shape), _const_spec(b.shape)]
    return pl.pallas_call(
        functools.partial(_mix_out_kernel, n_in=len(a_list)),
        grid=(n // tm,),
        in_specs=in_specs,
        out_specs=pl.BlockSpec((tm, D_MODEL), lambda i: (i, 0)),
        out_shape=jax.ShapeDtypeStruct((n, D_MODEL), F32),
        compiler_params=_cparams(("parallel",)),
        name="mix_out",
    )(*a_list, w, x, g, b)


def _swiglu_chunks(xbs, wg_ref, wu_ref, wd_ref, chunk):
    f = wg_ref.shape[-1]
    accs = [None] * len(xbs)
    for c in range(0, f, chunk):
        w = min(chunk, f - c)
        wg = wg_ref[:, c:c + w].astype(BF16)
        wu = wu_ref[:, c:c + w].astype(BF16)
        wd = wd_ref[c:c + w, :].astype(BF16)
        for t, xb in enumerate(xbs):
            h = (_silu(_dot(xb, wg)) * _dot(xb, wu)).astype(BF16)
            part = _dot(h, wd)
            accs[t] = part if accs[t] is None else accs[t] + part
    return accs


def _ffn_kernel(x_ref, wg_ref, wu_ref, wd_ref, g_ref, b_ref, o_ref):
    x = x_ref[...]
    y, = _swiglu_chunks([x.astype(BF16)], wg_ref, wu_ref, wd_ref, 512)
    o_ref[...] = _layer_norm(ALPHA * x + y, g_ref[...], b_ref[...])


def _ffn(x, wg, wu, wd, g, b):
    n = x.shape[0]
    tm = ROW_TILE
    single = pl.Buffered(1)
    return pl.pallas_call(
        _ffn_kernel,
        grid=(n // tm,),
        in_specs=[pl.BlockSpec((tm, D_MODEL), lambda i: (i, 0)),
                  pl.BlockSpec(wg.shape, lambda i: (0, 0), pipeline_mode=single),
                  pl.BlockSpec(wu.shape, lambda i: (0, 0), pipeline_mode=single),
                  pl.BlockSpec(wd.shape, lambda i: (0, 0), pipeline_mode=single),
                  _const_spec(g.shape), _const_spec(b.shape)],
        out_specs=pl.BlockSpec((tm, D_MODEL), lambda i: (i, 0)),
        out_shape=jax.ShapeDtypeStruct((n, D_MODEL), F32),
        compiler_params=_cparams(("parallel",)),
        name="ffn",
    )(x, wg, wu, wd, g, b)


def _experts_kernel(te_ref, nv_ref, nu_ref, xs_ref, wg_ref, wu_ref, wd_ref, o_ref):
    i = pl.program_id(0)
    j = pl.program_id(1)
    valid = nv_ref[i]

    @pl.when(j == 0)
    def _():
        o_ref[...] = jnp.zeros_like(o_ref)

    def accumulate(n_sub):
        rows = [slice(s * MOE_SUBTILE, (s + 1) * MOE_SUBTILE) for s in range(n_sub)]
        parts = _swiglu_chunks([xs_ref[r, :].astype(BF16) for r in rows], wg_ref, wu_ref, wd_ref, 256)
        for r, part in zip(rows, parts):
            o_ref[r, :] += part

    n_sub_max = MOE_TILE // MOE_SUBTILE
    for n_sub in range(1, n_sub_max + 1):
        lo = (n_sub - 1) * MOE_SUBTILE
        hi = n_sub * MOE_SUBTILE
        pl.when((valid > lo) & (valid <= hi))(functools.partial(accumulate, n_sub))


def _experts(tile_expert, tile_valid, n_used, xs, wg, wu, wd, layer):
    npad = xs.shape[0]
    tm = MOE_TILE
    fb = EXPERT_DIM // MOE_FSPLIT
    last = MOE_FSPLIT - 1

    def tile(i, nu):
        return jnp.minimum(i, jnp.maximum(nu[0] - 1, 0))

    def fblock(i, j, nu):
        ju = jnp.where(i < nu[0], j, last)
        return ju + (tile(i, nu) % 2) * (last - 2 * ju)

    grid_spec = pltpu.PrefetchScalarGridSpec(
        num_scalar_prefetch=3,
        grid=(npad // tm, MOE_FSPLIT),
        in_specs=[pl.BlockSpec((tm, D_MODEL), lambda i, j, te, nv, nu: (tile(i, nu), 0)),
                  pl.BlockSpec((None, None, D_MODEL, fb),
                               lambda i, j, te, nv, nu: (layer, te[i], 0, fblock(i, j, nu))),
                  pl.BlockSpec((None, None, D_MODEL, fb),
                               lambda i, j, te, nv, nu: (layer, te[i], 0, fblock(i, j, nu))),
                  pl.BlockSpec((None, None, fb, D_MODEL),
                               lambda i, j, te, nv, nu: (layer, te[i], fblock(i, j, nu), 0))],
        out_specs=pl.BlockSpec((tm, D_MODEL), lambda i, j, te, nv, nu: (i, 0)),
    )
    return pl.pallas_call(
        _experts_kernel,
        grid_spec=grid_spec,
        out_shape=jax.ShapeDtypeStruct((npad, D_MODEL), F32),
        compiler_params=_cparams(("arbitrary", "arbitrary")),
        name="experts",
    )(tile_expert, tile_valid, n_used, xs, wg, wu, wd)


def _router_kernel(x_ref, wr_ref, br_ref, ltri_ref, o_ref, cnt_ref, carry_ref):
    @pl.when(pl.program_id(0) == 0)
    def _():
        carry_ref[...] = jnp.zeros_like(carry_ref)

    xh, xl = _split(x_ref[...])
    wh, wl = _split(wr_ref[...])
    logits = _dot(xh, wh) + (_dot(xh, wl) + _dot(xl, wh)) + br_ref[...]
    lane = lax.broadcasted_iota(jnp.int32, logits.shape, 1)
    m1 = jnp.max(logits, axis=-1, keepdims=True)
    i1 = jnp.min(jnp.where(logits == m1, lane, LANES), axis=-1, keepdims=True)
    rest = jnp.where(lane == i1, -jnp.inf, logits)
    m2 = jnp.max(rest, axis=-1, keepdims=True)
    i2 = jnp.min(jnp.where(rest == m2, lane, LANES), axis=-1, keepdims=True)
    e2 = jnp.exp(m2 - m1)
    g1 = 1.0 / (1.0 + e2)
    g2 = e2 / (1.0 + e2)
    sel1 = lane == i1
    sel2 = lane == i2
    onehot = jnp.where(sel1, 1.0, jnp.where(sel2, 1.0, 0.0))
    before = _dot(ltri_ref[...], onehot.astype(BF16)) + carry_ref[0:1, :]
    r1 = jnp.sum(jnp.where(sel1, before, 0.0), axis=-1, keepdims=True)
    r2 = jnp.sum(jnp.where(sel2, before, 0.0), axis=-1, keepdims=True)
    new_carry = carry_ref[0:1, :] + jnp.sum(onehot, axis=0, keepdims=True)
    carry_ref[...] = jnp.broadcast_to(new_carry, carry_ref.shape)
    cnt_ref[...] = jnp.broadcast_to(new_carry, cnt_ref.shape)
    vals = (i1.astype(F32), i2.astype(F32), r1, r2, g1, g2)
    out = jnp.zeros(logits.shape, F32)
    for k, v in enumerate(vals):
        out = jnp.where(lane == k, v, out)
    o_ref[...] = out


def _router(x, wr, br):
    n = x.shape[0]
    tm = ROW_TILE
    ltri = jnp.asarray(np.tril(np.ones((tm, tm), np.float32), -1), BF16)
    return pl.pallas_call(
        _router_kernel,
        grid=(n // tm,),
        in_specs=[pl.BlockSpec((tm, D_MODEL), lambda i: (i, 0)),
                  _const_spec(wr.shape), _const_spec(br.shape), _const_spec(ltri.shape)],
        out_specs=[pl.BlockSpec((tm, LANES), lambda i: (i, 0)),
                   pl.BlockSpec((HALO, LANES), lambda i: (0, 0))],
        out_shape=[jax.ShapeDtypeStruct((n, LANES), F32),
                   jax.ShapeDtypeStruct((HALO, LANES), F32)],
        scratch_shapes=[pltpu.VMEM((HALO, LANES), F32)],
        compiler_params=_cparams(("arbitrary",)),
        name="router",
    )(x, wr, br, ltri)


def _dispatch_kernel(slot_ref, x_ref, xs_in_ref, xs_ref, sem):
    del xs_in_ref
    tg = x_ref.shape[0]

    def issue(r, carry):
        for k in range(2):
            pltpu.make_async_copy(x_ref.at[pl.ds(r, 1)], xs_ref.at[pl.ds(slot_ref[0, 0, 2 * r + k], 1)],
                                  sem).start()
        return carry

    lax.fori_loop(0, tg, issue, 0, unroll=8)
    for k in range(2):
        pltpu.make_async_copy(x_ref, xs_ref.at[pl.ds(0, tg)], sem).wait()


def _dispatch(x, slots, npad):
    n = x.shape[0]
    tg = GATHER_TILE
    slots3 = slots.reshape(n // tg, 1, 2 * tg)
    zeros = jnp.zeros((npad, D_MODEL), F32)
    return pl.pallas_call(
        _dispatch_kernel,
        grid=(n // tg,),
        in_specs=[pl.BlockSpec((1, 1, 2 * tg), lambda i: (i, 0, 0), memory_space=pltpu.SMEM),
                  pl.BlockSpec((tg, D_MODEL), lambda i: (i, 0)),
                  pl.BlockSpec(memory_space=pl.ANY)],
        out_specs=pl.BlockSpec(memory_space=pl.ANY),
        out_shape=jax.ShapeDtypeStruct((npad, D_MODEL), F32),
        scratch_shapes=[pltpu.SemaphoreType.DMA(())],
        input_output_aliases={2: 0},
        compiler_params=_cparams(("arbitrary",)),
        name="moe_dispatch",
    )(slots3, x, zeros)


def _combine_kernel(slot_ref, ys_ref, x_ref, info_ref, g_ref, b_ref, o_ref, buf_ref, sem):
    tg = x_ref.shape[0]

    def issue(r, carry):
        for k in range(2):
            pltpu.make_async_copy(ys_ref.at[pl.ds(slot_ref[0, 0, 2 * r + k], 1)],
                                  buf_ref.at[k, pl.ds(r, 1)], sem).start()
        return carry

    lax.fori_loop(0, tg, issue, 0, unroll=8)
    for k in range(2):
        pltpu.make_async_copy(ys_ref.at[pl.ds(0, tg)], buf_ref.at[k], sem).wait()
    info = info_ref[...]
    y = info[:, 4:5] * buf_ref[0] + info[:, 5:6] * buf_ref[1]
    o_ref[...] = _layer_norm(ALPHA * x_ref[...] + y, g_ref[...], b_ref[...])


def _combine(ys, slots, x, info, g, b):
    n = x.shape[0]
    tg = GATHER_TILE
    slots3 = slots.reshape(n // tg, 1, 2 * tg)
    return pl.pallas_call(
        _combine_kernel,
        grid=(n // tg,),
        in_specs=[pl.BlockSpec((1, 1, 2 * tg), lambda i: (i, 0, 0), memory_space=pltpu.SMEM),
                  pl.BlockSpec(memory_space=pl.ANY),
                  pl.BlockSpec((tg, D_MODEL), lambda i: (i, 0)),
                  pl.BlockSpec((tg, LANES), lambda i: (i, 0)),
                  _const_spec(g.shape), _const_spec(b.shape)],
        out_specs=pl.BlockSpec((tg, D_MODEL), lambda i: (i, 0)),
        out_shape=jax.ShapeDtypeStruct((n, D_MODEL), F32),
        scratch_shapes=[pltpu.VMEM((2, tg, D_MODEL), F32), pltpu.SemaphoreType.DMA(())],
        compiler_params=_cparams(("arbitrary",)),
        name="moe_combine",
    )(slots3, ys, x, info, g, b)


def _moe(x, wr, br, wg, wu, wd, layer, g, b):
    n = x.shape[0]
    tm = MOE_TILE
    info, counts = _router(x, wr, br)
    experts = info[:, 0:2].astype(jnp.int32)
    ranks = info[:, 2:4].astype(jnp.int32)
    count = counts[0, :N_EXPERTS].astype(jnp.int32)
    tiles = (count + tm - 1) // tm
    tile_end = jnp.cumsum(tiles)
    tile_start = tile_end - tiles
    slots = tile_start[experts] * tm + ranks
    n_tiles = (2 * n) // tm + N_EXPERTS
    n_used = tile_end[-1]
    tile_ids = jnp.arange(n_tiles, dtype=jnp.int32)
    tile_id = jnp.minimum(tile_ids, n_used - 1)
    tile_expert = jnp.sum((tile_id[:, None] >= tile_end[None, :]).astype(jnp.int32), axis=1)
    tile_valid = jnp.clip(count[tile_expert] - (tile_ids - tile_start[tile_expert]) * tm, 0, tm)
    tile_valid = jnp.where(tile_ids < n_used, tile_valid, 0)
    xs = _dispatch(x, slots, n_tiles * tm)
    ys = _experts(tile_expert.astype(jnp.int32), tile_valid.astype(jnp.int32),
                  n_used.reshape(1).astype(jnp.int32), xs, wg, wu, wd, layer)
    return _combine(ys, slots, x, info, g, b)


def _rope(x, c, sa, sb):
    return x * c + pltpu.roll(x, LANES - ROPE_DIM // 2, 1) * sa + pltpu.roll(x, ROPE_DIM // 2, 1) * sb


def _swa_kernel(sink_ref, q_ref, k_ref, v_ref, c_ref, sa_ref, sb_ref, o_ref, kp_ref, vp_ref):
    n = pl.program_id(1)
    blk = SWA_BLOCK

    @pl.when(n == 0)
    def _():
        kp_ref[...] = jnp.zeros_like(kp_ref)
        vp_ref[...] = jnp.zeros_like(vp_ref)

    c = c_ref[...]
    sa = sa_ref[...]
    sb = sb_ref[...]
    lane = lax.broadcasted_iota(jnp.int32, (blk, LANES), 1)
    low = lane < SWA_HEAD_DIM
    row = lax.broadcasted_iota(jnp.int32, (blk, blk), 0)
    col = lax.broadcasted_iota(jnp.int32, (blk, blk), 1)
    mask_cur = col <= row
    zero = jnp.zeros((blk, LANES), BF16)
    nq = q_ref.shape[0] // blk

    def halves(t, g):
        swapped = jnp.concatenate([t[:, SWA_HEAD_DIM:], t[:, :SWA_HEAD_DIM]], axis=1)
        src_a, src_b = (t, swapped) if g == 0 else (swapped, t)
        return jnp.where(low, src_a, zero), jnp.where(low, zero, src_b)

    both = lambda t: [halves(t, g) for g in range(SWA_KV_HEADS)]
    k_blocks = [kp_ref[...]] + [_rope(k_ref[i * blk:(i + 1) * blk, :].astype(F32), c[i * blk:(i + 1) * blk],
                                      sa[i * blk:(i + 1) * blk], sb[i * blk:(i + 1) * blk]).astype(BF16)
                                for i in range(nq)]
    v_blocks = [vp_ref[...]] + [v_ref[i * blk:(i + 1) * blk, :] for i in range(nq)]
    k_halves = [both(t) for t in k_blocks]
    v_halves = [both(t) for t in v_blocks]

    group = SWA_HEADS // SWA_KV_HEADS
    entries = [(i, h) for i in range(nq) for h in range(SWA_HEADS)]
    stack = lambda f: jnp.stack([f(i, h) for i, h in entries])
    q_pairs = [[(_rope(q_ref[i * blk:(i + 1) * blk, pr * LANES:(pr + 1) * LANES].astype(F32),
                       c[i * blk:(i + 1) * blk], sa[i * blk:(i + 1) * blk], sb[i * blk:(i + 1) * blk])
                 * (SWA_HEAD_DIM ** -0.5)).astype(BF16) for pr in range(SWA_HEADS // 2)] for i in range(nq)]
    q_all = stack(lambda i, h: q_pairs[i][h // 2])
    sink = stack(lambda i, h: jnp.full((1, 1), sink_ref[h], F32))
    s_prev = jnp.where(col > row, _bdot_nt(q_all, stack(lambda i, h: k_halves[i][h // group][h % 2])), -jnp.inf)
    s_first = jnp.where(n > 0, s_prev[:SWA_HEADS], -jnp.inf)
    s_prev = s_first if nq == 1 else jnp.concatenate([s_first, s_prev[SWA_HEADS:]], axis=0)
    s_cur = jnp.where(mask_cur, _bdot_nt(q_all, stack(lambda i, h: k_halves[i + 1][h // group][h % 2])), -jnp.inf)
    m = jnp.maximum(jnp.maximum(jnp.max(s_prev, axis=-1, keepdims=True),
                                jnp.max(s_cur, axis=-1, keepdims=True)), sink)
    p_prev = jnp.exp(s_prev - m)
    p_cur = jnp.exp(s_cur - m)
    denom = (jnp.sum(p_prev, axis=-1, keepdims=True) + jnp.sum(p_cur, axis=-1, keepdims=True)
             + jnp.exp(sink - m))
    inv = 1.0 / denom
    out = (_bdot((p_prev * inv).astype(BF16), stack(lambda i, h: v_halves[i][h // group][h % 2]))
           + _bdot((p_cur * inv).astype(BF16), stack(lambda i, h: v_halves[i + 1][h // group][h % 2])))
    for i in range(nq):
        for pr in range(SWA_HEADS // 2):
            e = i * SWA_HEADS + 2 * pr
            o_ref[i * blk:(i + 1) * blk, pr * LANES:(pr + 1) * LANES] = (out[e] + out[e + 1]).astype(o_ref.dtype)

    kp_ref[...] = k_blocks[nq]
    vp_ref[...] = v_blocks[nq]


def _swa(p, sinks, tables, bsz, t, q_col, k_col, v_col):
    rows = SWA_STEP_BLOCKS * SWA_BLOCK
    nb = t // rows
    c, sa, sb = tables
    row = lambda b, n: b * nb + n
    return pl.pallas_call(
        _swa_kernel,
        grid=(bsz, nb),
        in_specs=[pl.BlockSpec(memory_space=pltpu.SMEM),
                  pl.BlockSpec((rows, SWA_Q), lambda b, n: (row(b, n), q_col)),
                  pl.BlockSpec((rows, SWA_KV), lambda b, n: (row(b, n), k_col)),
                  pl.BlockSpec((rows, SWA_KV), lambda b, n: (row(b, n), v_col)),
                  pl.BlockSpec((rows, LANES), lambda b, n: (n, 0)),
                  pl.BlockSpec((rows, LANES), lambda b, n: (n, 0)),
                  pl.BlockSpec((rows, LANES), lambda b, n: (n, 0))],
        out_specs=pl.BlockSpec((rows, SWA_Q), lambda b, n: (row(b, n), 0)),
        out_shape=jax.ShapeDtypeStruct((bsz * t, SWA_Q), BF16),
        scratch_shapes=[pltpu.VMEM((SWA_BLOCK, SWA_KV), BF16), pltpu.VMEM((SWA_BLOCK, SWA_KV), BF16)],
        compiler_params=_cparams(("parallel", "arbitrary")),
        name="swa",
    )(sinks, p, p, p, c, sa, sb)


def _rope_tables(t):
    half = ROPE_DIM // 2
    pos = jnp.arange(t, dtype=jnp.int32)
    inv_freq = jnp.power(ROPE_THETA, -jnp.arange(half, dtype=F32) / half)
    ang = pos.astype(F32)[:, None] * inv_freq[None, :]
    cos = jnp.cos(ang)
    sin = jnp.sin(ang)
    ones = jnp.ones((t, SWA_HEAD_DIM - ROPE_DIM), F32)
    zeros = jnp.zeros((t, SWA_HEAD_DIM - ROPE_DIM), F32)
    zhalf = jnp.zeros((t, half), F32)
    c = jnp.concatenate([cos, cos, ones], axis=1)
    sa = jnp.concatenate([-sin, zhalf, zeros], axis=1)
    sb = jnp.concatenate([zhalf, sin, zeros], axis=1)
    two = lambda a: jnp.concatenate([a, a], axis=1)
    return two(c), two(sa), two(sb)


def _split(a):
    hi = a.astype(BF16)
    return hi, (a - hi.astype(F32)).astype(BF16)


def _bdot(a, b):
    return lax.dot_general(a, b, (((2,), (1,)), ((0,), (0,))), preferred_element_type=F32)


def _bdot_nt(a, b):
    return lax.dot_general(a, b, (((2,), (2,)), ((0,), (0,))), preferred_element_type=F32)


def _mm(a, b):
    if DN_INV_PASSES == 1:
        return _bdot(a.astype(BF16), b.astype(BF16))
    ah, al = _split(a)
    bh, bl = _split(b)
    return _bdot(ah, bh) + (_bdot(ah, bl) + _bdot(al, bh))


def _unit_lower_inverse(a, eye, blk_mask, merge_masks):
    d0 = jnp.where(blk_mask, a, 0.0)
    x = eye - d0
    p = _mm(d0, d0)
    x = x + _mm(x, p)
    p = _mm(p, p)
    x = x + _mm(x, p)
    for m in merge_masks:
        lm = jnp.where(m, a, 0.0)
        x = x - _mm(x, _mm(lm, x))
    return x


def _dn_kernel(q_ref, k_ref, v_ref, gate_ref, tail_ref, cw_ref, par_ref, nw_ref, o_ref,
               xpad_ref, qkv_ref, bg_ref, s_ref, gc_ref, u_ref, oi_ref, kd_ref, wq_ref):
    n = pl.program_id(1)
    tb = q_ref.shape[0]
    ch = DN_CHUNK
    hd = DN_HEAD_DIM

    @pl.when(n == 0)
    def _():
        xpad_ref[0:HALO, :] = jnp.zeros((HALO, 3 * DN_W), F32)
        s_ref[...] = jnp.zeros_like(s_ref)

    xpad_ref[HALO:HALO + tb, 0:DN_W] = q_ref[...].astype(F32)
    xpad_ref[HALO:HALO + tb, DN_W:2 * DN_W] = k_ref[...].astype(F32)
    xpad_ref[HALO:HALO + tb, 2 * DN_W:3 * DN_W] = v_ref[...].astype(F32)
    conv = None
    for kk in range(DN_CONV):
        off = HALO - (DN_CONV - 1) + kk
        term = cw_ref[kk:kk + 1, :] * xpad_ref[off:off + tb, :]
        conv = term if conv is None else conv + term
    xpad_ref[0:HALO, :] = xpad_ref[tb:tb + HALO, :]
    qkv = _silu(conv)
    for h in range(2 * DN_HEADS):
        xh = qkv[:, h * hd:(h + 1) * hd]
        scale = lax.rsqrt(jnp.sum(xh * xh, axis=-1, keepdims=True) + RMS_EPS)
        if h < DN_HEADS:
            scale = scale * (hd ** -0.5)
        qkv_ref[:, h * hd:(h + 1) * hd] = xh * scale
    qkv_ref[:, 2 * DN_W:] = qkv[:, 2 * DN_W:]

    tail = tail_ref[...]
    lane = lax.broadcasted_iota(jnp.int32, tail.shape, 1)
    gval = -jnp.exp(par_ref[0:1, :]) * _softplus(tail + par_ref[1:2, :])
    bg_ref[...] = jnp.where(lane < DN_HEADS, _sigmoid(tail), gval)

    rt = lax.broadcasted_iota(jnp.int32, (tb, tb), 0)
    ct = lax.broadcasted_iota(jnp.int32, (tb, tb), 1)
    ltri = jnp.where((rt >= ct) & ((rt // ch) == (ct // ch)), 1.0, 0.0).astype(F32)
    er = lax.broadcasted_iota(jnp.int32, (LANES, LANES), 0)
    ec = lax.broadcasted_iota(jnp.int32, (LANES, LANES), 1)
    eye = jnp.where(er == ec, 1.0, 0.0).astype(F32)
    bg = bg_ref[...]
    gc_all = _dot(ltri, jnp.where(lane >= DN_HEADS, bg, 0.0), HIGHEST)
    gc_t = _dot_nt(eye, gc_all, HIGHEST)
    gc_ref[...] = gc_all

    sb = 2 * ch
    same = (er // ch) == (ec // ch)
    causal = same & (er >= ec)
    strict = same & (er > ec)
    blk_mask = (er > ec) & ((er // 8) == (ec // 8))
    merge_masks = [((er // (2 * s)) == (ec // (2 * s))) & ((er % (2 * s)) >= s) & ((ec % (2 * s)) < s)
                   for s in (8, 16, 32)]
    first = lax.broadcasted_iota(jnp.int32, (sb, 1), 0) < ch
    probs = [(h, b2) for h in range(DN_HEADS) for b2 in range(tb // sb)]
    rows_of = lambda b2: slice(b2 * sb, (b2 + 1) * sb)
    stack = lambda f: jnp.stack([f(h, b2) for h, b2 in probs])
    gcol = stack(lambda h, b2: gc_all[rows_of(b2), DN_HEADS + h:DN_HEADS + h + 1])
    grow = stack(lambda h, b2: gc_t[DN_HEADS + h:DN_HEADS + h + 1, rows_of(b2)])
    beta = stack(lambda h, b2: bg[rows_of(b2), h:h + 1])
    q = stack(lambda h, b2: qkv_ref[rows_of(b2), h * hd:(h + 1) * hd])
    k = stack(lambda h, b2: qkv_ref[rows_of(b2), DN_W + h * hd:DN_W + (h + 1) * hd])
    v = stack(lambda h, b2: qkv_ref[rows_of(b2), 2 * DN_W + h * hd:2 * DN_W + (h + 1) * hd])
    decay = jnp.where(causal, jnp.exp(gcol - grow), 0.0)
    k16 = k.astype(BF16)
    a = jnp.where(strict, _bdot_nt(k16, k16) * decay * beta, 0.0)
    tinv = _unit_lower_inverse(a, eye, blk_mask, merge_masks)
    egc = jnp.exp(gcol)
    u = _mm(tinv, v * beta)
    w16 = _mm(tinv, k * (beta * egc)).astype(BF16)
    attn16 = (_bdot_nt(q.astype(BF16), k16) * decay).astype(BF16)
    q_eff16 = (q * egc - _bdot(attn16, w16)).astype(BF16)
    o_intra = _bdot(attn16, u.astype(BF16))
    glast = jnp.where(first, gcol[:, ch - 1:ch, :], gcol[:, sb - 1:sb, :])
    k_dec16 = (k * jnp.exp(glast - gcol)).astype(BF16)
    for g, (h, b2) in enumerate(probs):
        u_ref[h, rows_of(b2), :] = u[g]
        oi_ref[h, rows_of(b2), :] = o_intra[g]
        kd_ref[h, rows_of(b2), :] = k_dec16[g]
        for c in range(2):
            wq_ref[h, 2 * b2 + c, 0:ch, :] = w16[g, c * ch:(c + 1) * ch]
            wq_ref[h, 2 * b2 + c, ch:sb, :] = q_eff16[g, c * ch:(c + 1) * ch]

    for c in range(tb // ch):
        rows = slice(c * ch, (c + 1) * ch)
        s = s_ref[...]
        r = _bdot(wq_ref[:, c], s.astype(BF16))
        v_new16 = (u_ref[:, rows, :] - r[:, 0:ch]).astype(BF16)
        for h in range(DN_HEADS):
            glast = gc_ref[(c + 1) * ch - 1:(c + 1) * ch, DN_HEADS + h:DN_HEADS + h + 1]
            s_ref[h] = s[h] * jnp.exp(glast) + _dot_tn(kd_ref[h, rows, :], v_new16[h])
        for h in range(DN_HEADS):
            o = r[h, ch:sb] + oi_ref[h, rows, :]
            gate = gate_ref[rows, h * hd:(h + 1) * hd].astype(F32)
            on = o * lax.rsqrt(jnp.mean(o * o, axis=-1, keepdims=True) + RMS_EPS) * nw_ref[...]
            o_ref[rows, h * hd:(h + 1) * hd] = (on * _silu(gate)).astype(o_ref.dtype)


def _deltanet(p, tail, conv_w, par, norm_w, bsz, t, cols):
    tb = DN_BLOCK
    nb = t // tb
    row = lambda b, n: b * nb + n
    col_spec = lambda cidx: pl.BlockSpec((tb, DN_W), lambda b, n: (row(b, n), cidx))
    return pl.pallas_call(
        _dn_kernel,
        grid=(bsz, nb),
        in_specs=[col_spec(cols[0]), col_spec(cols[1]), col_spec(cols[2]), col_spec(cols[3]),
                  pl.BlockSpec((tb, LANES), lambda b, n: (row(b, n), 0)),
                  _const_spec(conv_w.shape), _const_spec(par.shape), _const_spec(norm_w.shape)],
        out_specs=pl.BlockSpec((tb, DN_W), lambda b, n: (row(b, n), 0)),
        out_shape=jax.ShapeDtypeStruct((bsz * t, DN_W), BF16),
        scratch_shapes=[pltpu.VMEM((tb + HALO, 3 * DN_W), F32),
                        pltpu.VMEM((tb, 3 * DN_W), F32),
                        pltpu.VMEM((tb, LANES), F32),
                        pltpu.VMEM((DN_HEADS, DN_HEAD_DIM, DN_HEAD_DIM), F32),
                        pltpu.VMEM((tb, LANES), F32),
                        pltpu.VMEM((DN_HEADS, tb, DN_HEAD_DIM), F32),
                        pltpu.VMEM((DN_HEADS, tb, DN_HEAD_DIM), F32),
                        pltpu.VMEM((DN_HEADS, tb, DN_HEAD_DIM), BF16),
                        pltpu.VMEM((DN_HEADS, tb // DN_CHUNK, 2 * DN_CHUNK, DN_HEAD_DIM), BF16)],
        compiler_params=_cparams(("parallel", "arbitrary")),
        name="deltanet",
    )(p, p, p, p, tail, conv_w, par, norm_w)


def _ssd_kernel(z_ref, xs_ref, bc_ref, dt_ref, cw_ref, cb_ref, pcol_ref, prow_ref,
                dskip_ref, nw_ref, expand_ref, o_ref, xpad_ref, xc_ref, h_ref):
    n = pl.program_id(1)
    L = SSM_CHUNK

    @pl.when(n == 0)
    def _():
        xpad_ref[0:L, :] = jnp.zeros((L, SSM_CONV_DIM), BF16)
        h_ref[...] = jnp.zeros_like(h_ref)

    xpad_ref[L:2 * L, 0:SSM_D_INNER] = xs_ref[...]
    xpad_ref[L:2 * L, SSM_D_INNER:] = bc_ref[...]
    sr = lax.broadcasted_iota(jnp.int32, (L, 2 * L), 0)
    sc = lax.broadcasted_iota(jnp.int32, (L, 2 * L), 1)
    shifts = [jnp.where(sc == sr + (L - 3 + kk), 1.0, 0.0).astype(BF16) for kk in range(3)]
    cw_blk = 512
    for c0 in range(0, SSM_CONV_DIM, cw_blk):
        cols = slice(c0, c0 + cw_blk)
        xe = xpad_ref[:, cols]
        conv = cb_ref[:, cols] + cw_ref[3:4, cols] * xe[L:2 * L].astype(F32)
        for kk in range(3):
            conv = conv + cw_ref[kk:kk + 1, cols] * _dot(shifts[kk], xe)
        xc_ref[:, cols] = _silu(conv)
    xpad_ref[0:L, :] = xpad_ref[L:2 * L, :]

    ri = lax.broadcasted_iota(jnp.int32, (L, L), 0)
    ci = lax.broadcasted_iota(jnp.int32, (L, L), 1)
    causal = ri >= ci
    ltri = jnp.where(causal, 1.0, 0.0).astype(F32)
    utri = jnp.where(ri <= ci, 1.0, 0.0).astype(F32)
    eye = jnp.where(ri == ci, 1.0, 0.0).astype(F32)

    dt_raw = dt_ref[...]
    dt = _softplus(dt_raw + pcol_ref[0:1, :])
    acs = _dot(ltri, dt * pcol_ref[1:2, :], HIGHEST)
    tot = acs[L - 1:L, :]
    f1 = dt * jnp.exp(tot - acs)
    ea = jnp.exp(acs)
    dt_r = _softplus(_dot_nt(eye, dt_raw, HIGHEST)[0:SSM_HEADS] + prow_ref[0])
    acs_r = _dot(dt_r * prow_ref[1], utri, HIGHEST)

    stacked = jnp.concatenate([dt, f1, ea], axis=0).astype(BF16)
    cd = jnp.broadcast_to(jnp.exp(tot), (HALO, LANES))
    cd_hi, cd_lo = _split(cd)
    lane = lax.broadcasted_iota(jnp.int32, (L, LANES), 1)
    low = lane < SSM_HEAD_DIM
    zero16 = jnp.zeros((L, LANES), BF16)
    for g in range(SSM_GROUPS):
        gl = slice(g * SSM_GROUP_W, (g + 1) * SSM_GROUP_W)
        ex = expand_ref[:, gl]
        wide = _dot(stacked, ex)
        cd_x = (_dot(cd_hi, ex) + _dot(cd_lo, ex))[0:1, :]
        xs = xc_ref[:, gl]
        xdt16 = (xs * wide[0:L]).astype(BF16)
        xdd16 = (xs * wide[L:2 * L]).astype(BF16)
        bm = xc_ref[:, SSM_D_INNER + g * SSM_STATE:SSM_D_INNER + (g + 1) * SSM_STATE].astype(BF16)
        cm = xc_ref[:, SSM_D_INNER + SSM_GN + g * SSM_STATE:
                    SSM_D_INNER + SSM_GN + (g + 1) * SSM_STATE].astype(BF16)
        cb = _dot_nt(cm, bm)
        hg = h_ref[g]
        y_off = _dot(cm, hg.astype(BF16)) * wide[2 * L:3 * L]
        h_ref[g] = hg * cd_x + _dot_tn(bm, xdd16)
        parts = []
        for j in range(SSM_GROUP_W // LANES):
            xp16 = xdt16[:, j * LANES:(j + 1) * LANES]
            yd = None
            for side in range(2):
                hidx = (g * SSM_GROUP_W + j * LANES) // SSM_HEAD_DIM + side
                seg = jnp.where(causal, jnp.exp(acs[:, hidx:hidx + 1] - acs_r[hidx:hidx + 1, :]), 0.0)
                m = (cb * seg).astype(BF16)
                xm = jnp.where(low, xp16, zero16) if side == 0 else jnp.where(low, zero16, xp16)
                part = _dot(m, xm)
                yd = part if yd is None else yd + part
            parts.append(yd)
        y = jnp.concatenate(parts, axis=1) + y_off + dskip_ref[:, gl] * xs
        y = y * _silu(z_ref[:, gl].astype(F32))
        yn = y * lax.rsqrt(jnp.mean(y * y, axis=-1, keepdims=True) + RMS_EPS) * nw_ref[:, gl]
        o_ref[:, gl] = yn.astype(o_ref.dtype)


def _ssd(p, dt, conv_w, conv_b, pcol, prow, dskip, norm_w, expand, bsz, t):
    L = SSM_CHUNK
    nb = t // L
    row = lambda b, n: b * nb + n
    return pl.pallas_call(
        _ssd_kernel,
        grid=(bsz, nb),
        in_specs=[pl.BlockSpec((L, SSM_D_INNER), lambda b, n: (row(b, n), 0)),
                  pl.BlockSpec((L, SSM_D_INNER), lambda b, n: (row(b, n), 1)),
                  pl.BlockSpec((L, 2 * SSM_GN), lambda b, n: (row(b, n), 2 * SSM_D_INNER // (2 * SSM_GN))),
                  pl.BlockSpec((L, LANES), lambda b, n: (row(b, n), 0)),
                  _const_spec(conv_w.shape), _const_spec(conv_b.shape), _const_spec(pcol.shape),
                  _const_spec(prow.shape), _const_spec(dskip.shape), _const_spec(norm_w.shape),
                  _const_spec(expand.shape)],
        out_specs=pl.BlockSpec((L, SSM_D_INNER), lambda b, n: (row(b, n), 0)),
        out_shape=jax.ShapeDtypeStruct((bsz * t, SSM_D_INNER), BF16),
        scratch_shapes=[pltpu.VMEM((2 * L, SSM_CONV_DIM), BF16),
                        pltpu.VMEM((L, SSM_CONV_DIM), F32),
                        pltpu.VMEM((SSM_GROUPS, SSM_STATE, SSM_GROUP_W), F32)],
        compiler_params=_cparams(("parallel", "arbitrary")),
        name="ssd",
    )(p, p, p, dt, conv_w, conv_b, pcol, prow, dskip, norm_w, expand)


def _pad_lanes(v, offset=0):
    v = v.astype(F32)
    return jnp.zeros((1, LANES), F32).at[0, offset:offset + v.shape[0]].set(v)


def _even_layer(x, bsz, t, w_in, sinks, conv_w, a_log, dt_bias, norm_w, w_out, ln_g, ln_b,
                ffn_wg, ffn_wu, ffn_wd, rope_tables):
    o = np.cumsum((0, SWA_Q, SWA_KV, SWA_KV, DN_W, DN_W, DN_W, DN_W, DN_HEADS, DN_HEADS))
    seg = lambda i: w_in[:, o[i]:o[i + 1]]
    w_main = jnp.concatenate([seg(0), seg(3), seg(4), seg(5), seg(6), seg(1), seg(2)], axis=1).astype(BF16)
    w_tail = w_in[:, o[7]:o[9]]
    wt = jnp.zeros((D_MODEL, LANES), F32).at[:, :2 * DN_HEADS].set(w_tail)
    p, tail = _proj(x, w_main, wt)
    out_a = _swa(p, sinks.astype(F32), rope_tables, bsz, t,
                 q_col=0, k_col=(SWA_Q + 4 * DN_W) // SWA_KV, v_col=(SWA_Q + 4 * DN_W) // SWA_KV + 1)
    par = jnp.concatenate([_pad_lanes(a_log, DN_HEADS), _pad_lanes(dt_bias, DN_HEADS)], axis=0)
    out_b = _deltanet(p, tail, conv_w.astype(F32), par, norm_w.reshape(1, DN_HEAD_DIM).astype(F32),
                      bsz, t, cols=(1, 2, 3, 4))
    x = _mix_out([out_a, out_b], w_out.astype(BF16), x, ln_g[0:1], ln_b[0:1])
    return _ffn(x, ffn_wg.astype(BF16), ffn_wu.astype(BF16), ffn_wd.astype(BF16), ln_g[1:2], ln_b[1:2])


def _odd_layer(x, bsz, t, w_in, conv_w, conv_b, dt_bias, a_log, d_skip, norm_w, w_out, ln_g, ln_b,
               w_router, b_router, moe_wg, moe_wu, moe_wd, moe_layer):
    main = SSM_D_INNER + SSM_CONV_DIM
    w_tail = w_in[:, main:]
    wt = jnp.zeros((D_MODEL, LANES), F32).at[:, :SSM_HEADS].set(w_tail)
    p, dt = _proj(x, w_in[:, :main].astype(BF16), wt)
    a = -jnp.exp(a_log.astype(F32))
    pcol = jnp.concatenate([_pad_lanes(dt_bias), _pad_lanes(a)], axis=0)
    prow = jnp.stack([jnp.broadcast_to(dt_bias.astype(F32)[:, None], (SSM_HEADS, SSM_CHUNK)),
                      jnp.broadcast_to(a[:, None], (SSM_HEADS, SSM_CHUNK))])
    dskip = jnp.repeat(d_skip.astype(F32), SSM_HEAD_DIM)[None, :]
    expand = (jnp.arange(LANES)[:, None] == (jnp.arange(SSM_D_INNER) // SSM_HEAD_DIM)[None, :]).astype(BF16)
    y = _ssd(p, dt, conv_w.astype(F32), conv_b.astype(F32)[None, :], pcol, prow, dskip,
             norm_w.astype(F32)[None, :], expand, bsz, t)
    x = _mix_out([y], w_out.astype(BF16), x, ln_g[0:1], ln_b[0:1])
    wr = jnp.zeros((D_MODEL, LANES), F32).at[:, :N_EXPERTS].set(w_router)
    br = jnp.full((1, LANES), -1e30, F32).at[0, :N_EXPERTS].set(b_router.astype(F32))
    return _moe(x, wr, br, moe_wg, moe_wu, moe_wd, moe_layer, ln_g[1:2], ln_b[1:2])


def kernel(x, ln_g, ln_b, even_w_in, swa_sinks, dn_conv_w, dn_a_log, dn_dt_bias, dn_norm_w, even_w_out,
           ssm_w_in, ssm_conv_w, ssm_conv_b, ssm_dt_bias, ssm_a_log, ssm_d, ssm_norm_w, ssm_w_out,
           ffn_w_gate, ffn_w_up, ffn_w_down, moe_w_router, moe_b_router, moe_w_gate, moe_w_up, moe_w_down):
    bsz, t, d = x.shape
    h = x.reshape(bsz * t, d)
    rope_tables = _rope_tables(t)
    for i in range(DEPTH):
        j = i // 2
        if i % 2 == 0:
            h = _even_layer(h, bsz, t, even_w_in[j], swa_sinks[j], dn_conv_w[j], dn_a_log[j], dn_dt_bias[j],
                            dn_norm_w[j], even_w_out[j], ln_g[i], ln_b[i],
                            ffn_w_gate[j], ffn_w_up[j], ffn_w_down[j], rope_tables)
        else:
            h = _odd_layer(h, bsz, t, ssm_w_in[j], ssm_conv_w[j], ssm_conv_b[j], ssm_dt_bias[j], ssm_a_log[j],
                           ssm_d[j], ssm_norm_w[j], ssm_w_out[j], ln_g[i], ln_b[i],
                           moe_w_router[j], moe_b_router[j], moe_w_gate, moe_w_up, moe_w_down, j)
    return h.reshape(bsz, t, d)
```

```python
import functools

import numpy as np
import jax
import jax.numpy as jnp
from jax import lax
from jax.experimental import pallas as pl
from jax.experimental.pallas import tpu as pltpu

F32 = jnp.float32
BF16 = jnp.bfloat16
HIGHEST = lax.Precision.HIGHEST

D_MODEL = 1024
DEPTH = 4
ALPHA = (2 * DEPTH) ** 0.25
LN_EPS = 1e-5
RMS_EPS = 1e-6

SWA_HEADS = 8
SWA_KV_HEADS = 2
SWA_HEAD_DIM = 64
SWA_BLOCK = 128
ROPE_DIM = SWA_HEAD_DIM // 4
ROPE_THETA = 500000.0
SWA_Q = SWA_HEADS * SWA_HEAD_DIM
SWA_KV = SWA_KV_HEADS * SWA_HEAD_DIM

DN_HEADS = 4
DN_HEAD_DIM = 128
DN_CONV = 4
DN_CHUNK = 64
DN_W = DN_HEADS * DN_HEAD_DIM

SSM_D_INNER = 2 * D_MODEL
SSM_HEAD_DIM = 64
SSM_HEADS = SSM_D_INNER // SSM_HEAD_DIM
SSM_GROUPS = 4
SSM_STATE = 128
SSM_CHUNK = 128
SSM_GN = SSM_GROUPS * SSM_STATE
SSM_CONV_DIM = SSM_D_INNER + 2 * SSM_GN
SSM_GROUP_W = SSM_D_INNER // SSM_GROUPS

FFN_DIM = 2816
N_EXPERTS = 8
EXPERT_DIM = 3584

LANES = 128
HALO = 8
VMEM_LIMIT = 52 * 1024 * 1024

ROW_TILE = 512
MOE_TILE = 1024
MOE_SUBTILE = 512
MOE_FSPLIT = 7
SWA_STEP_BLOCKS = 4
DN_BLOCK = 256
DN_INV_PASSES = 1
GATHER_TILE = 512


def _cparams(sem):
    return pltpu.CompilerParams(dimension_semantics=sem, vmem_limit_bytes=VMEM_LIMIT)


def _sigmoid(x):
    return 1.0 / (1.0 + jnp.exp(-x))


def _silu(x):
    hx = 0.5 * x
    return hx + hx * jnp.tanh(hx)


def _softplus(x):
    return jnp.maximum(x, 0.0) + jnp.log(1.0 + jnp.exp(-jnp.abs(x)))


def _layer_norm(y, g, b):
    mu = jnp.mean(y, axis=-1, keepdims=True)
    d = y - mu
    var = jnp.mean(d * d, axis=-1, keepdims=True)
    return d * lax.rsqrt(var + LN_EPS) * g + b


def _dot(a, b, precision=None):
    return jnp.dot(a, b, preferred_element_type=F32, precision=precision)


def _dot_nt(a, b, precision=None):
    return lax.dot_general(a, b, (((1,), (1,)), ((), ())), preferred_element_type=F32, precision=precision)


def _dot_tn(a, b, precision=None):
    return lax.dot_general(a, b, (((0,), (0,)), ((), ())), preferred_element_type=F32, precision=precision)


def _const_spec(shape):
    nd = len(shape)
    return pl.BlockSpec(shape, lambda *_: (0,) * nd)


def _proj_kernel(x_ref, w_ref, wt_ref, o_ref, t_ref, *, chunk):
    x = x_ref[...]
    xb, xl = _split(x)
    for c in range(0, o_ref.shape[1], chunk):
        o_ref[:, c:c + chunk] = _dot(xb, w_ref[:, c:c + chunk]).astype(o_ref.dtype)
    wth, wtl = _split(wt_ref[...])
    both = _dot(xb, jnp.concatenate([wth, wtl], axis=1))
    t_ref[...] = both[:, :LANES] + (both[:, LANES:] + _dot(xl, wth))


def _proj(x, w, wt):
    n = x.shape[0]
    c = w.shape[1]
    tm = ROW_TILE
    return pl.pallas_call(
        functools.partial(_proj_kernel, chunk=512),
        grid=(n // tm,),
        in_specs=[pl.BlockSpec((tm, D_MODEL), lambda i: (i, 0)),
                  _const_spec(w.shape), _const_spec(wt.shape)],
        out_specs=[pl.BlockSpec((tm, c), lambda i: (i, 0)),
                   pl.BlockSpec((tm, LANES), lambda i: (i, 0))],
        out_shape=[jax.ShapeDtypeStruct((n, c), BF16),
                   jax.ShapeDtypeStruct((n, LANES), F32)],
        compiler_params=_cparams(("parallel",)),
        name="proj",
    )(x, w, wt)


def _mix_out_kernel(*refs, n_in):
    a_refs = refs[:n_in]
    w_ref, x_ref, g_ref, b_ref, o_ref = refs[n_in:]
    half = x_ref.shape[0] // 2
    for r0 in (0, half):
        rows = slice(r0, r0 + half)
        a = a_refs[0][rows, :] if n_in == 1 else jnp.concatenate([r[rows, :] for r in a_refs], axis=1)
        y = ALPHA * x_ref[rows, :] + _dot(a, w_ref[...])
        o_ref[rows, :] = _layer_norm(y, g_ref[...], b_ref[...])


def _mix_out(a_list, w, x, g, b):
    n = x.shape[0]
    tm = ROW_TILE
    in_specs = [pl.BlockSpec((tm, a.shape[1]), lambda i: (i, 0)) for a in a_list]
    in_specs += [_const_spec(w.shape), pl.BlockSpec((tm, D_MODEL), lambda i: (i, 0)),
                 _const_spec(g.shape), _const_spec(b.shape)]
    return pl.pallas_call(
        functools.partial(_mix_out_kernel, n_in=len(a_list)),
        grid=(n // tm,),
        in_specs=in_specs,
        out_specs=pl.BlockSpec((tm, D_MODEL), lambda i: (i, 0)),
        out_shape=jax.ShapeDtypeStruct((n, D_MODEL), F32),
        compiler_params=_cparams(("parallel",)),
        name="mix_out",
    )(*a_list, w, x, g, b)


def _swiglu_chunks(xbs, wg_ref, wu_ref, wd_ref, chunk):
    f = wg_ref.shape[-1]
    accs = [None] * len(xbs)
    for c in range(0, f, chunk):
        w = min(chunk, f - c)
        wg = wg_ref[:, c:c + w].astype(BF16)
        wu = wu_ref[:, c:c + w].astype(BF16)
        wd = wd_ref[c:c + w, :].astype(BF16)
        for t, xb in enumerate(xbs):
            h = (_silu(_dot(xb, wg)) * _dot(xb, wu)).astype(BF16)
            part = _dot(h, wd)
            accs[t] = part if accs[t] is None else accs[t] + part
    return accs


def _ffn_kernel(x_ref, wg_ref, wu_ref, wd_ref, g_ref, b_ref, o_ref):
    x = x_ref[...]
    y, = _swiglu_chunks([x.astype(BF16)], wg_ref, wu_ref, wd_ref, 512)
    o_ref[...] = _layer_norm(ALPHA * x + y, g_ref[...], b_ref[...])


def _ffn(x, wg, wu, wd, g, b):
    n = x.shape[0]
    tm = ROW_TILE
    single = pl.Buffered(1)
    return pl.pallas_call(
        _ffn_kernel,
        grid=(n // tm,),
        in_specs=[pl.BlockSpec((tm, D_MODEL), lambda i: (i, 0)),
                  pl.BlockSpec(wg.shape, lambda i: (0, 0), pipeline_mode=single),
                  pl.BlockSpec(wu.shape, lambda i: (0, 0), pipeline_mode=single),
                  pl.BlockSpec(wd.shape, lambda i: (0, 0), pipeline_mode=single),
                  _const_spec(g.shape), _const_spec(b.shape)],
        out_specs=pl.BlockSpec((tm, D_MODEL), lambda i: (i, 0)),
        out_shape=jax.ShapeDtypeStruct((n, D_MODEL), F32),
        compiler_params=_cparams(("parallel",)),
        name="ffn",
    )(x, wg, wu, wd, g, b)


def _experts_kernel(te_ref, nv_ref, nu_ref, xs_ref, wg_ref, wu_ref, wd_ref, o_ref):
    i = pl.program_id(0)
    j = pl.program_id(1)
    valid = nv_ref[i]

    @pl.when(j == 0)
    def _():
        o_ref[...] = jnp.zeros_like(o_ref)

    def accumulate(n_sub):
        rows = [slice(s * MOE_SUBTILE, (s + 1) * MOE_SUBTILE) for s in range(n_sub)]
        parts = _swiglu_chunks([xs_ref[r, :].astype(BF16) for r in rows], wg_ref, wu_ref, wd_ref, 256)
        for r, part in zip(rows, parts):
            o_ref[r, :] += part

    n_sub_max = MOE_TILE // MOE_SUBTILE
    for n_sub in range(1, n_sub_max + 1):
        lo = (n_sub - 1) * MOE_SUBTILE
        hi = n_sub * MOE_SUBTILE
        pl.when((valid > lo) & (valid <= hi))(functools.partial(accumulate, n_sub))


def _experts(tile_expert, tile_valid, n_used, xs, wg, wu, wd, layer):
    npad = xs.shape[0]
    tm = MOE_TILE
    fb = EXPERT_DIM // MOE_FSPLIT
    last = MOE_FSPLIT - 1

    def tile(i, nu):
        return jnp.minimum(i, jnp.maximum(nu[0] - 1, 0))

    def fblock(i, j, nu):
        ju = jnp.where(i < nu[0], j, last)
        return ju + (tile(i, nu) % 2) * (last - 2 * ju)

    grid_spec = pltpu.PrefetchScalarGridSpec(
        num_scalar_prefetch=3,
        grid=(npad // tm, MOE_FSPLIT),
        in_specs=[pl.BlockSpec((tm, D_MODEL), lambda i, j, te, nv, nu: (tile(i, nu), 0)),
                  pl.BlockSpec((None, None, D_MODEL, fb),
                               lambda i, j, te, nv, nu: (layer, te[i], 0, fblock(i, j, nu))),
                  pl.BlockSpec((None, None, D_MODEL, fb),
                               lambda i, j, te, nv, nu: (layer, te[i], 0, fblock(i, j, nu))),
                  pl.BlockSpec((None, None, fb, D_MODEL),
                               lambda i, j, te, nv, nu: (layer, te[i], fblock(i, j, nu), 0))],
        out_specs=pl.BlockSpec((tm, D_MODEL), lambda i, j, te, nv, nu: (i, 0)),
    )
    return pl.pallas_call(
        _experts_kernel,
        grid_spec=grid_spec,
        out_shape=jax.ShapeDtypeStruct((npad, D_MODEL), F32),
        compiler_params=_cparams(("arbitrary", "arbitrary")),
        name="experts",
    )(tile_expert, tile_valid, n_used, xs, wg, wu, wd)


def _router_kernel(x_ref, wr_ref, br_ref, ltri_ref, o_ref, cnt_ref, carry_ref):
    @pl.when(pl.program_id(0) == 0)
    def _():
        carry_ref[...] = jnp.zeros_like(carry_ref)

    xh, xl = _split(x_ref[...])
    wh, wl = _split(wr_ref[...])
    both = _dot(xh, jnp.concatenate([wh, wl], axis=1))
    logits = both[:, :LANES] + (both[:, LANES:] + _dot(xl, wh)) + br_ref[...]
    lane = lax.broadcasted_iota(jnp.int32, logits.shape, 1)
    m1 = jnp.max(logits, axis=-1, keepdims=True)
    i1 = jnp.min(jnp.where(logits == m1, lane, LANES), axis=-1, keepdims=True)
    rest = jnp.where(lane == i1, -jnp.inf, logits)
    m2 = jnp.max(rest, axis=-1, keepdims=True)
    i2 = jnp.min(jnp.where(rest == m2, lane, LANES), axis=-1, keepdims=True)
    e2 = jnp.exp(m2 - m1)
    g1 = 1.0 / (1.0 + e2)
    g2 = e2 / (1.0 + e2)
    sel1 = lane == i1
    sel2 = lane == i2
    onehot = jnp.where(sel1, 1.0, jnp.where(sel2, 1.0, 0.0))
    before = _dot(ltri_ref[...], onehot.astype(BF16)) + carry_ref[0:1, :]
    r1 = jnp.sum(jnp.where(sel1, before, 0.0), axis=-1, keepdims=True)
    r2 = jnp.sum(jnp.where(sel2, before, 0.0), axis=-1, keepdims=True)
    new_carry = carry_ref[0:1, :] + jnp.sum(onehot, axis=0, keepdims=True)
    carry_ref[...] = jnp.broadcast_to(new_carry, carry_ref.shape)
    cnt_ref[...] = jnp.broadcast_to(new_carry, cnt_ref.shape)
    vals = (i1.astype(F32), i2.astype(F32), r1, r2, g1, g2)
    out = jnp.zeros(logits.shape, F32)
    for k, v in enumerate(vals):
        out = jnp.where(lane == k, v, out)
    o_ref[...] = out


def _router(x, wr, br):
    n = x.shape[0]
    tm = ROW_TILE
    ltri = jnp.asarray(np.tril(np.ones((tm, tm), np.float32), -1), BF16)
    return pl.pallas_call(
        _router_kernel,
        grid=(n // tm,),
        in_specs=[pl.BlockSpec((tm, D_MODEL), lambda i: (i, 0)),
                  _const_spec(wr.shape), _const_spec(br.shape), _const_spec(ltri.shape)],
        out_specs=[pl.BlockSpec((tm, LANES), lambda i: (i, 0)),
                   pl.BlockSpec((HALO, LANES), lambda i: (0, 0))],
        out_shape=[jax.ShapeDtypeStruct((n, LANES), F32),
                   jax.ShapeDtypeStruct((HALO, LANES), F32)],
        scratch_shapes=[pltpu.VMEM((HALO, LANES), F32)],
        compiler_params=_cparams(("arbitrary",)),
        name="router",
    )(x, wr, br, ltri)


def _dispatch_kernel(slot_ref, x_ref, xs_in_ref, xs_ref, sem):
    del xs_in_ref
    tg = x_ref.shape[0]

    def issue(r, carry):
        for k in range(2):
            pltpu.make_async_copy(x_ref.at[pl.ds(r, 1)], xs_ref.at[pl.ds(slot_ref[0, 0, 2 * r + k], 1)],
                                  sem).start()
        return carry

    lax.fori_loop(0, tg, issue, 0, unroll=8)
    for k in range(2):
        pltpu.make_async_copy(x_ref, xs_ref.at[pl.ds(0, tg)], sem).wait()


def _dispatch(x, slots, npad):
    n = x.shape[0]
    tg = GATHER_TILE
    slots3 = slots.reshape(n // tg, 1, 2 * tg)
    zeros = jnp.zeros((npad, D_MODEL), F32)
    return pl.pallas_call(
        _dispatch_kernel,
        grid=(n // tg,),
        in_specs=[pl.BlockSpec((1, 1, 2 * tg), lambda i: (i, 0, 0), memory_space=pltpu.SMEM),
                  pl.BlockSpec((tg, D_MODEL), lambda i: (i, 0)),
                  pl.BlockSpec(memory_space=pl.ANY)],
        out_specs=pl.BlockSpec(memory_space=pl.ANY),
        out_shape=jax.ShapeDtypeStruct((npad, D_MODEL), F32),
        scratch_shapes=[pltpu.SemaphoreType.DMA(())],
        input_output_aliases={2: 0},
        compiler_params=_cparams(("arbitrary",)),
        name="moe_dispatch",
    )(slots3, x, zeros)


def _combine_kernel(slot_ref, ys_ref, x_ref, info_ref, g_ref, b_ref, o_ref, buf_ref, sem):
    tg = x_ref.shape[0]

    def issue(r, carry):
        for k in range(2):
            pltpu.make_async_copy(ys_ref.at[pl.ds(slot_ref[0, 0, 2 * r + k], 1)],
                                  buf_ref.at[k, pl.ds(r, 1)], sem).start()
        return carry

    lax.fori_loop(0, tg, issue, 0, unroll=8)
    for k in range(2):
        pltpu.make_async_copy(ys_ref.at[pl.ds(0, tg)], buf_ref.at[k], sem).wait()
    info = info_ref[...]
    y = info[:, 4:5] * buf_ref[0] + info[:, 5:6] * buf_ref[1]
    o_ref[...] = _layer_norm(ALPHA * x_ref[...] + y, g_ref[...], b_ref[...])


def _combine(ys, slots, x, info, g, b):
    n = x.shape[0]
    tg = GATHER_TILE
    slots3 = slots.reshape(n // tg, 1, 2 * tg)
    return pl.pallas_call(
        _combine_kernel,
        grid=(n // tg,),
        in_specs=[pl.BlockSpec((1, 1, 2 * tg), lambda i: (i, 0, 0), memory_space=pltpu.SMEM),
                  pl.BlockSpec(memory_space=pl.ANY),
                  pl.BlockSpec((tg, D_MODEL), lambda i: (i, 0)),
                  pl.BlockSpec((tg, LANES), lambda i: (i, 0)),
                  _const_spec(g.shape), _const_spec(b.shape)],
        out_specs=pl.BlockSpec((tg, D_MODEL), lambda i: (i, 0)),
        out_shape=jax.ShapeDtypeStruct((n, D_MODEL), F32),
        scratch_shapes=[pltpu.VMEM((2, tg, D_MODEL), F32), pltpu.SemaphoreType.DMA(())],
        compiler_params=_cparams(("arbitrary",)),
        name="moe_combine",
    )(slots3, ys, x, info, g, b)


def _moe(x, wr, br, wg, wu, wd, layer, g, b):
    n = x.shape[0]
    tm = MOE_TILE
    info, counts = _router(x, wr, br)
    experts = info[:, 0:2].astype(jnp.int32)
    ranks = info[:, 2:4].astype(jnp.int32)
    count = counts[0, :N_EXPERTS].astype(jnp.int32)
    tiles = (count + tm - 1) // tm
    tile_end = jnp.cumsum(tiles)
    tile_start = tile_end - tiles
    slots = tile_start[experts] * tm + ranks
    n_tiles = (2 * n) // tm + N_EXPERTS
    n_used = tile_end[-1]
    tile_ids = jnp.arange(n_tiles, dtype=jnp.int32)
    tile_id = jnp.minimum(tile_ids, n_used - 1)
    tile_expert = jnp.sum((tile_id[:, None] >= tile_end[None, :]).astype(jnp.int32), axis=1)
    tile_valid = jnp.clip(count[tile_expert] - (tile_ids - tile_start[tile_expert]) * tm, 0, tm)
    tile_valid = jnp.where(tile_ids < n_used, tile_valid, 0)
    xs = _dispatch(x, slots, n_tiles * tm)
    ys = _experts(tile_expert.astype(jnp.int32), tile_valid.astype(jnp.int32),
                  n_used.reshape(1).astype(jnp.int32), xs, wg, wu, wd, layer)
    return _combine(ys, slots, x, info, g, b)


def _rope(x, c, sa, sb):
    return x * c + pltpu.roll(x, LANES - ROPE_DIM // 2, 1) * sa + pltpu.roll(x, ROPE_DIM // 2, 1) * sb


def _swa_kernel(sink_ref, q_ref, k_ref, v_ref, c_ref, sa_ref, sb_ref, o_ref, kp_ref, vp_ref):
    n = pl.program_id(1)
    blk = SWA_BLOCK

    @pl.when(n == 0)
    def _():
        kp_ref[...] = jnp.zeros_like(kp_ref)
        vp_ref[...] = jnp.zeros_like(vp_ref)

    c = c_ref[...]
    sa = sa_ref[...]
    sb = sb_ref[...]
    lane = lax.broadcasted_iota(jnp.int32, (blk, LANES), 1)
    low = lane < SWA_HEAD_DIM
    row = lax.broadcasted_iota(jnp.int32, (blk, blk), 0)
    col = lax.broadcasted_iota(jnp.int32, (blk, blk), 1)
    mask_cur = col <= row
    zero = jnp.zeros((blk, LANES), BF16)
    nq = q_ref.shape[0] // blk

    def halves(t, g):
        swapped = jnp.concatenate([t[:, SWA_HEAD_DIM:], t[:, :SWA_HEAD_DIM]], axis=1)
        src_a, src_b = (t, swapped) if g == 0 else (swapped, t)
        return jnp.where(low, src_a, zero), jnp.where(low, zero, src_b)

    both = lambda t: [halves(t, g) for g in range(SWA_KV_HEADS)]
    k_blocks = [kp_ref[...]] + [_rope(k_ref[i * blk:(i + 1) * blk, :].astype(F32), c[i * blk:(i + 1) * blk],
                                      sa[i * blk:(i + 1) * blk], sb[i * blk:(i + 1) * blk]).astype(BF16)
                                for i in range(nq)]
    v_blocks = [vp_ref[...]] + [v_ref[i * blk:(i + 1) * blk, :] for i in range(nq)]
    k_halves = [both(t) for t in k_blocks]
    v_halves = [both(t) for t in v_blocks]

    group = SWA_HEADS // SWA_KV_HEADS
    entries = [(i, h) for i in range(nq) for h in range(SWA_HEADS)]
    stack = lambda f: jnp.stack([f(i, h) for i, h in entries])
    q_pairs = [[(_rope(q_ref[i * blk:(i + 1) * blk, pr * LANES:(pr + 1) * LANES].astype(F32),
                       c[i * blk:(i + 1) * blk], sa[i * blk:(i + 1) * blk], sb[i * blk:(i + 1) * blk])
                 * (SWA_HEAD_DIM ** -0.5)).astype(BF16) for pr in range(SWA_HEADS // 2)] for i in range(nq)]
    q_all = stack(lambda i, h: q_pairs[i][h // 2])
    sink = stack(lambda i, h: jnp.full((1, 1), sink_ref[h], F32))
    s_prev = jnp.where(col > row, _bdot_nt(q_all, stack(lambda i, h: k_halves[i][h // group][h % 2])), -jnp.inf)
    s_first = jnp.where(n > 0, s_prev[:SWA_HEADS], -jnp.inf)
    s_prev = s_first if nq == 1 else jnp.concatenate([s_first, s_prev[SWA_HEADS:]], axis=0)
    s_cur = jnp.where(mask_cur, _bdot_nt(q_all, stack(lambda i, h: k_halves[i + 1][h // group][h % 2])), -jnp.inf)
    m = jnp.maximum(jnp.maximum(jnp.max(s_prev, axis=-1, keepdims=True),
                                jnp.max(s_cur, axis=-1, keepdims=True)), sink)
    p_prev = jnp.exp(s_prev - m)
    p_cur = jnp.exp(s_cur - m)
    denom = (jnp.sum(p_prev, axis=-1, keepdims=True) + jnp.sum(p_cur, axis=-1, keepdims=True)
             + jnp.exp(sink - m))
    inv = 1.0 / denom
    out = (_bdot((p_prev * inv).astype(BF16), stack(lambda i, h: v_halves[i][h // group][h % 2]))
           + _bdot((p_cur * inv).astype(BF16), stack(lambda i, h: v_halves[i + 1][h // group][h % 2])))
    for i in range(nq):
        for pr in range(SWA_HEADS // 2):
            e = i * SWA_HEADS + 2 * pr
            o_ref[i * blk:(i + 1) * blk, pr * LANES:(pr + 1) * LANES] = (out[e] + out[e + 1]).astype(o_ref.dtype)

    kp_ref[...] = k_blocks[nq]
    vp_ref[...] = v_blocks[nq]


def _swa(p, sinks, tables, bsz, t, q_col, k_col, v_col):
    rows = SWA_STEP_BLOCKS * SWA_BLOCK
    nb = t // rows
    c, sa, sb = tables
    row = lambda b, n: b * nb + n
    return pl.pallas_call(
        _swa_kernel,
        grid=(bsz, nb),
        in_specs=[pl.BlockSpec(memory_space=pltpu.SMEM),
                  pl.BlockSpec((rows, SWA_Q), lambda b, n: (row(b, n), q_col)),
                  pl.BlockSpec((rows, SWA_KV), lambda b, n: (row(b, n), k_col)),
                  pl.BlockSpec((rows, SWA_KV), lambda b, n: (row(b, n), v_col)),
                  pl.BlockSpec((rows, LANES), lambda b, n: (n, 0)),
                  pl.BlockSpec((rows, LANES), lambda b, n: (n, 0)),
                  pl.BlockSpec((rows, LANES), lambda b, n: (n, 0))],
        out_specs=pl.BlockSpec((rows, SWA_Q), lambda b, n: (row(b, n), 0)),
        out_shape=jax.ShapeDtypeStruct((bsz * t, SWA_Q), BF16),
        scratch_shapes=[pltpu.VMEM((SWA_BLOCK, SWA_KV), BF16), pltpu.VMEM((SWA_BLOCK, SWA_KV), BF16)],
        compiler_params=_cparams(("parallel", "arbitrary")),
        name="swa",
    )(sinks, p, p, p, c, sa, sb)


def _rope_tables(t):
    half = ROPE_DIM // 2
    pos = jnp.arange(t, dtype=jnp.int32)
    inv_freq = jnp.power(ROPE_THETA, -jnp.arange(half, dtype=F32) / half)
    ang = pos.astype(F32)[:, None] * inv_freq[None, :]
    cos = jnp.cos(ang)
    sin = jnp.sin(ang)
    ones = jnp.ones((t, SWA_HEAD_DIM - ROPE_DIM), F32)
    zeros = jnp.zeros((t, SWA_HEAD_DIM - ROPE_DIM), F32)
    zhalf = jnp.zeros((t, half), F32)
    c = jnp.concatenate([cos, cos, ones], axis=1)
    sa = jnp.concatenate([-sin, zhalf, zeros], axis=1)
    sb = jnp.concatenate([zhalf, sin, zeros], axis=1)
    two = lambda a: jnp.concatenate([a, a], axis=1)
    return two(c), two(sa), two(sb)


def _split(a):
    hi = a.astype(BF16)
    return hi, (a - hi.astype(F32)).astype(BF16)


def _bdot(a, b):
    return lax.dot_general(a, b, (((2,), (1,)), ((0,), (0,))), preferred_element_type=F32)


def _bdot_nt(a, b):
    return lax.dot_general(a, b, (((2,), (2,)), ((0,), (0,))), preferred_element_type=F32)


def _mm(a, b):
    if DN_INV_PASSES == 1:
        return _bdot(a.astype(BF16), b.astype(BF16))
    ah, al = _split(a)
    bh, bl = _split(b)
    return _bdot(ah, bh) + (_bdot(ah, bl) + _bdot(al, bh))


def _unit_lower_inverse(a, eye, blk_mask, merge_masks):
    d0 = jnp.where(blk_mask, a, 0.0)
    x = eye - d0
    p = _mm(d0, d0)
    x = x + _mm(x, p)
    p = _mm(p, p)
    x = x + _mm(x, p)
    for m in merge_masks:
        lm = jnp.where(m, a, 0.0)
        x = x - _mm(x, _mm(lm, x))
    return x


def _dn_kernel(q_ref, k_ref, v_ref, gate_ref, tail_ref, cw_ref, par_ref, nw_ref, o_ref,
               xpad_ref, qkv_ref, bg_ref, s_ref, gc_ref, u_ref, oi_ref, kd_ref, wq_ref):
    n = pl.program_id(1)
    tb = q_ref.shape[0]
    ch = DN_CHUNK
    hd = DN_HEAD_DIM

    @pl.when(n == 0)
    def _():
        xpad_ref[0:HALO, :] = jnp.zeros((HALO, 3 * DN_W), F32)
        s_ref[...] = jnp.zeros_like(s_ref)

    xpad_ref[HALO:HALO + tb, 0:DN_W] = q_ref[...].astype(F32)
    xpad_ref[HALO:HALO + tb, DN_W:2 * DN_W] = k_ref[...].astype(F32)
    xpad_ref[HALO:HALO + tb, 2 * DN_W:3 * DN_W] = v_ref[...].astype(F32)
    conv = None
    for kk in range(DN_CONV):
        off = HALO - (DN_CONV - 1) + kk
        term = cw_ref[kk:kk + 1, :] * xpad_ref[off:off + tb, :]
        conv = term if conv is None else conv + term
    xpad_ref[0:HALO, :] = xpad_ref[tb:tb + HALO, :]
    qkv = _silu(conv)
    for h in range(2 * DN_HEADS):
        xh = qkv[:, h * hd:(h + 1) * hd]
        scale = lax.rsqrt(jnp.sum(xh * xh, axis=-1, keepdims=True) + RMS_EPS)
        if h < DN_HEADS:
            scale = scale * (hd ** -0.5)
        qkv_ref[:, h * hd:(h + 1) * hd] = xh * scale
    qkv_ref[:, 2 * DN_W:] = qkv[:, 2 * DN_W:]

    tail = tail_ref[...]
    lane = lax.broadcasted_iota(jnp.int32, tail.shape, 1)
    gval = -jnp.exp(par_ref[0:1, :]) * _softplus(tail + par_ref[1:2, :])
    bg_ref[...] = jnp.where(lane < DN_HEADS, _sigmoid(tail), gval)

    rt = lax.broadcasted_iota(jnp.int32, (tb, tb), 0)
    ct = lax.broadcasted_iota(jnp.int32, (tb, tb), 1)
    ltri = jnp.where((rt >= ct) & ((rt // ch) == (ct // ch)), 1.0, 0.0).astype(F32)
    er = lax.broadcasted_iota(jnp.int32, (LANES, LANES), 0)
    ec = lax.broadcasted_iota(jnp.int32, (LANES, LANES), 1)
    eye = jnp.where(er == ec, 1.0, 0.0).astype(F32)
    bg = bg_ref[...]
    gc_all = _dot(ltri, jnp.where(lane >= DN_HEADS, bg, 0.0), HIGHEST)
    gc_t = _dot_nt(eye, gc_all, HIGHEST)
    gc_ref[...] = gc_all

    sb = 2 * ch
    same = (er // ch) == (ec // ch)
    causal = same & (er >= ec)
    strict = same & (er > ec)
    blk_mask = (er > ec) & ((er // 8) == (ec // 8))
    merge_masks = [((er // (2 * s)) == (ec // (2 * s))) & ((er % (2 * s)) >= s) & ((ec % (2 * s)) < s)
                   for s in (8, 16, 32)]
    first = lax.broadcasted_iota(jnp.int32, (sb, 1), 0) < ch
    probs = [(h, b2) for h in range(DN_HEADS) for b2 in range(tb // sb)]
    rows_of = lambda b2: slice(b2 * sb, (b2 + 1) * sb)
    stack = lambda f: jnp.stack([f(h, b2) for h, b2 in probs])
    gcol = stack(lambda h, b2: gc_all[rows_of(b2), DN_HEADS + h:DN_HEADS + h + 1])
    grow = stack(lambda h, b2: gc_t[DN_HEADS + h:DN_HEADS + h + 1, rows_of(b2)])
    beta = stack(lambda h, b2: bg[rows_of(b2), h:h + 1])
    q = stack(lambda h, b2: qkv_ref[rows_of(b2), h * hd:(h + 1) * hd])
    k = stack(lambda h, b2: qkv_ref[rows_of(b2), DN_W + h * hd:DN_W + (h + 1) * hd])
    v = stack(lambda h, b2: qkv_ref[rows_of(b2), 2 * DN_W + h * hd:2 * DN_W + (h + 1) * hd])
    decay = jnp.where(causal, jnp.exp(gcol - grow), 0.0)
    k16 = k.astype(BF16)
    a = jnp.where(strict, _bdot_nt(k16, k16) * decay * beta, 0.0)
    tinv = _unit_lower_inverse(a, eye, blk_mask, merge_masks)
    egc = jnp.exp(gcol)
    u = _mm(tinv, v * beta)
    w16 = _mm(tinv, k * (beta * egc)).astype(BF16)
    attn16 = (_bdot_nt(q.astype(BF16), k16) * decay).astype(BF16)
    q_eff16 = (q * egc - _bdot(attn16, w16)).astype(BF16)
    o_intra = _bdot(attn16, u.astype(BF16))
    glast = jnp.where(first, gcol[:, ch - 1:ch, :], gcol[:, sb - 1:sb, :])
    k_dec16 = (k * jnp.exp(glast - gcol)).astype(BF16)
    for g, (h, b2) in enumerate(probs):
        u_ref[h, rows_of(b2), :] = u[g]
        oi_ref[h, rows_of(b2), :] = o_intra[g]
        kd_ref[h, rows_of(b2), :] = k_dec16[g]
        for c in range(2):
            wq_ref[h, 2 * b2 + c, 0:ch, :] = w16[g, c * ch:(c + 1) * ch]
            wq_ref[h, 2 * b2 + c, ch:sb, :] = q_eff16[g, c * ch:(c + 1) * ch]

    for c in range(tb // ch):
        rows = slice(c * ch, (c + 1) * ch)
        s = s_ref[...]
        r = _bdot(wq_ref[:, c], s.astype(BF16))
        v_new16 = (u_ref[:, rows, :] - r[:, 0:ch]).astype(BF16)
        for h in range(DN_HEADS):
            glast = gc_ref[(c + 1) * ch - 1:(c + 1) * ch, DN_HEADS + h:DN_HEADS + h + 1]
            s_ref[h] = s[h] * jnp.exp(glast) + _dot_tn(kd_ref[h, rows, :], v_new16[h])
        for h in range(DN_HEADS):
            o = r[h, ch:sb] + oi_ref[h, rows, :]
            gate = gate_ref[rows, h * hd:(h + 1) * hd].astype(F32)
            on = o * lax.rsqrt(jnp.mean(o * o, axis=-1, keepdims=True) + RMS_EPS) * nw_ref[...]
            o_ref[rows, h * hd:(h + 1) * hd] = (on * _silu(gate)).astype(o_ref.dtype)


def _deltanet(p, tail, conv_w, par, norm_w, bsz, t, cols):
    tb = DN_BLOCK
    nb = t // tb
    row = lambda b, n: b * nb + n
    col_spec = lambda cidx: pl.BlockSpec((tb, DN_W), lambda b, n: (row(b, n), cidx))
    return pl.pallas_call(
        _dn_kernel,
        grid=(bsz, nb),
        in_specs=[col_spec(cols[0]), col_spec(cols[1]), col_spec(cols[2]), col_spec(cols[3]),
                  pl.BlockSpec((tb, LANES), lambda b, n: (row(b, n), 0)),
                  _const_spec(conv_w.shape), _const_spec(par.shape), _const_spec(norm_w.shape)],
        out_specs=pl.BlockSpec((tb, DN_W), lambda b, n: (row(b, n), 0)),
        out_shape=jax.ShapeDtypeStruct((bsz * t, DN_W), BF16),
        scratch_shapes=[pltpu.VMEM((tb + HALO, 3 * DN_W), F32),
                        pltpu.VMEM((tb, 3 * DN_W), F32),
                        pltpu.VMEM((tb, LANES), F32),
                        pltpu.VMEM((DN_HEADS, DN_HEAD_DIM, DN_HEAD_DIM), F32),
                        pltpu.VMEM((tb, LANES), F32),
                        pltpu.VMEM((DN_HEADS, tb, DN_HEAD_DIM), F32),
                        pltpu.VMEM((DN_HEADS, tb, DN_HEAD_DIM), F32),
                        pltpu.VMEM((DN_HEADS, tb, DN_HEAD_DIM), BF16),
                        pltpu.VMEM((DN_HEADS, tb // DN_CHUNK, 2 * DN_CHUNK, DN_HEAD_DIM), BF16)],
        compiler_params=_cparams(("parallel", "arbitrary")),
        name="deltanet",
    )(p, p, p, p, tail, conv_w, par, norm_w)


def _ssd_kernel(z_ref, xs_ref, bc_ref, dt_ref, cw_ref, cb_ref, pcol_ref, prow_ref,
                dskip_ref, nw_ref, expand_ref, o_ref, xpad_ref, xc_ref, h_ref):
    n = pl.program_id(1)
    L = SSM_CHUNK

    @pl.when(n == 0)
    def _():
        xpad_ref[0:L, :] = jnp.zeros((L, SSM_CONV_DIM), BF16)
        h_ref[...] = jnp.zeros_like(h_ref)

    xpad_ref[L:2 * L, 0:SSM_D_INNER] = xs_ref[...]
    xpad_ref[L:2 * L, SSM_D_INNER:] = bc_ref[...]
    sr = lax.broadcasted_iota(jnp.int32, (L, 2 * L), 0)
    sc = lax.broadcasted_iota(jnp.int32, (L, 2 * L), 1)
    shifts = [jnp.where(sc == sr + (L - 3 + kk), 1.0, 0.0).astype(BF16) for kk in range(3)]
    cw_blk = 256
    for c0 in range(0, SSM_CONV_DIM, cw_blk):
        cols = slice(c0, c0 + cw_blk)
        xe = xpad_ref[:, cols]
        conv = cb_ref[:, cols] + cw_ref[3:4, cols] * xe[L:2 * L].astype(F32)
        for kk in range(3):
            conv = conv + cw_ref[kk:kk + 1, cols] * _dot(shifts[kk], xe)
        xc_ref[:, cols] = _silu(conv)
    xpad_ref[0:L, :] = xpad_ref[L:2 * L, :]

    ri = lax.broadcasted_iota(jnp.int32, (L, L), 0)
    ci = lax.broadcasted_iota(jnp.int32, (L, L), 1)
    causal = ri >= ci
    ltri = jnp.where(causal, 1.0, 0.0).astype(F32)
    utri = jnp.where(ri <= ci, 1.0, 0.0).astype(F32)
    eye = jnp.where(ri == ci, 1.0, 0.0).astype(F32)

    dt_raw = dt_ref[...]
    dt = _softplus(dt_raw + pcol_ref[0:1, :])
    acs = _dot(ltri, dt * pcol_ref[1:2, :], HIGHEST)
    tot = acs[L - 1:L, :]
    f1 = dt * jnp.exp(tot - acs)
    ea = jnp.exp(acs)
    dt_r = _softplus(_dot_nt(eye, dt_raw, HIGHEST)[0:SSM_HEADS] + prow_ref[0])
    acs_r = _dot(dt_r * prow_ref[1], utri, HIGHEST)

    stacked = jnp.concatenate([dt, f1, ea], axis=0).astype(BF16)
    cd = jnp.broadcast_to(jnp.exp(tot), (HALO, LANES))
    cd_hi, cd_lo = _split(cd)
    lane = lax.broadcasted_iota(jnp.int32, (L, LANES), 1)
    low = lane < SSM_HEAD_DIM
    zero16 = jnp.zeros((L, LANES), BF16)
    for g in range(SSM_GROUPS):
        gl = slice(g * SSM_GROUP_W, (g + 1) * SSM_GROUP_W)
        ex = expand_ref[:, gl]
        wide = _dot(stacked, ex)
        cd_x = (_dot(cd_hi, ex) + _dot(cd_lo, ex))[0:1, :]
        xs = xc_ref[:, gl]
        xdt16 = (xs * wide[0:L]).astype(BF16)
        xdd16 = (xs * wide[L:2 * L]).astype(BF16)
        bm = xc_ref[:, SSM_D_INNER + g * SSM_STATE:SSM_D_INNER + (g + 1) * SSM_STATE].astype(BF16)
        cm = xc_ref[:, SSM_D_INNER + SSM_GN + g * SSM_STATE:
                    SSM_D_INNER + SSM_GN + (g + 1) * SSM_STATE].astype(BF16)
        cb = _dot_nt(cm, bm)
        hg = h_ref[g]
        y_off = _dot(cm, hg.astype(BF16)) * wide[2 * L:3 * L]
        h_ref[g] = hg * cd_x + _dot_tn(bm, xdd16)
        parts = []
        for j in range(SSM_GROUP_W // LANES):
            xp16 = xdt16[:, j * LANES:(j + 1) * LANES]
            yd = None
            for side in range(2):
                hidx = (g * SSM_GROUP_W + j * LANES) // SSM_HEAD_DIM + side
                seg = jnp.where(causal, jnp.exp(acs[:, hidx:hidx + 1] - acs_r[hidx:hidx + 1, :]), 0.0)
                m = (cb * seg).astype(BF16)
                xm = jnp.where(low, xp16, zero16) if side == 0 else jnp.where(low, zero16, xp16)
                part = _dot(m, xm)
                yd = part if yd is None else yd + part
            parts.append(yd)
        y = jnp.concatenate(parts, axis=1) + y_off + dskip_ref[:, gl] * xs
        y = y * _silu(z_ref[:, gl].astype(F32))
        yn = y * lax.rsqrt(jnp.mean(y * y, axis=-1, keepdims=True) + RMS_EPS) * nw_ref[:, gl]
        o_ref[:, gl] = yn.astype(o_ref.dtype)


def _ssd(p, dt, conv_w, conv_b, pcol, prow, dskip, norm_w, expand, bsz, t):
    L = SSM_CHUNK
    nb = t // L
    row = lambda b, n: b * nb + n
    return pl.pallas_call(
        _ssd_kernel,
        grid=(bsz, nb),
        in_specs=[pl.BlockSpec((L, SSM_D_INNER), lambda b, n: (row(b, n), 0)),
                  pl.BlockSpec((L, SSM_D_INNER), lambda b, n: (row(b, n), 1)),
                  pl.BlockSpec((L, 2 * SSM_GN), lambda b, n: (row(b, n), 2 * SSM_D_INNER // (2 * SSM_GN))),
                  pl.BlockSpec((L, LANES), lambda b, n: (row(b, n), 0)),
                  _const_spec(conv_w.shape), _const_spec(conv_b.shape), _const_spec(pcol.shape),
                  _const_spec(prow.shape), _const_spec(dskip.shape), _const_spec(norm_w.shape),
                  _const_spec(expand.shape)],
        out_specs=pl.BlockSpec((L, SSM_D_INNER), lambda b, n: (row(b, n), 0)),
        out_shape=jax.ShapeDtypeStruct((bsz * t, SSM_D_INNER), BF16),
        scratch_shapes=[pltpu.VMEM((2 * L, SSM_CONV_DIM), BF16),
                        pltpu.VMEM((L, SSM_CONV_DIM), F32),
                        pltpu.VMEM((SSM_GROUPS, SSM_STATE, SSM_GROUP_W), F32)],
        compiler_params=_cparams(("parallel", "arbitrary")),
        name="ssd",
    )(p, p, p, dt, conv_w, conv_b, pcol, prow, dskip, norm_w, expand)


def _pad_lanes(v, offset=0):
    v = v.astype(F32)
    return jnp.zeros((1, LANES), F32).at[0, offset:offset + v.shape[0]].set(v)


def _even_layer(x, bsz, t, w_in, sinks, conv_w, a_log, dt_bias, norm_w, w_out, ln_g, ln_b,
                ffn_wg, ffn_wu, ffn_wd, rope_tables):
    o = np.cumsum((0, SWA_Q, SWA_KV, SWA_KV, DN_W, DN_W, DN_W, DN_W, DN_HEADS, DN_HEADS))
    seg = lambda i: w_in[:, o[i]:o[i + 1]]
    w_main = jnp.concatenate([seg(0), seg(3), seg(4), seg(5), seg(6), seg(1), seg(2)], axis=1).astype(BF16)
    w_tail = w_in[:, o[7]:o[9]]
    wt = jnp.zeros((D_MODEL, LANES), F32).at[:, :2 * DN_HEADS].set(w_tail)
    p, tail = _proj(x, w_main, wt)
    out_a = _swa(p, sinks.astype(F32), rope_tables, bsz, t,
                 q_col=0, k_col=(SWA_Q + 4 * DN_W) // SWA_KV, v_col=(SWA_Q + 4 * DN_W) // SWA_KV + 1)
    par = jnp.concatenate([_pad_lanes(a_log, DN_HEADS), _pad_lanes(dt_bias, DN_HEADS)], axis=0)
    out_b = _deltanet(p, tail, conv_w.astype(F32), par, norm_w.reshape(1, DN_HEAD_DIM).astype(F32),
                      bsz, t, cols=(1, 2, 3, 4))
    x = _mix_out([out_a, out_b], w_out.astype(BF16), x, ln_g[0:1], ln_b[0:1])
    return _ffn(x, ffn_wg.astype(BF16), ffn_wu.astype(BF16), ffn_wd.astype(BF16), ln_g[1:2], ln_b[1:2])


def _odd_layer(x, bsz, t, w_in, conv_w, conv_b, dt_bias, a_log, d_skip, norm_w, w_out, ln_g, ln_b,
               w_router, b_router, moe_wg, moe_wu, moe_wd, moe_layer):
    main = SSM_D_INNER + SSM_CONV_DIM
    w_tail = w_in[:, main:]
    wt = jnp.zeros((D_MODEL, LANES), F32).at[:, :SSM_HEADS].set(w_tail)
    p, dt = _proj(x, w_in[:, :main].astype(BF16), wt)
    a = -jnp.exp(a_log.astype(F32))
    pcol = jnp.concatenate([_pad_lanes(dt_bias), _pad_lanes(a)], axis=0)
    prow = jnp.stack([jnp.broadcast_to(dt_bias.astype(F32)[:, None], (SSM_HEADS, SSM_CHUNK)),
                      jnp.broadcast_to(a[:, None], (SSM_HEADS, SSM_CHUNK))])
    dskip = jnp.repeat(d_skip.astype(F32), SSM_HEAD_DIM)[None, :]
    expand = (jnp.arange(LANES)[:, None] == (jnp.arange(SSM_D_INNER) // SSM_HEAD_DIM)[None, :]).astype(BF16)
    y = _ssd(p, dt, conv_w.astype(F32), conv_b.astype(F32)[None, :], pcol, prow, dskip,
             norm_w.astype(F32)[None, :], expand, bsz, t)
    x = _mix_out([y], w_out.astype(BF16), x, ln_g[0:1], ln_b[0:1])
    wr = jnp.zeros((D_MODEL, LANES), F32).at[:, :N_EXPERTS].set(w_router)
    br = jnp.full((1, LANES), -1e30, F32).at[0, :N_EXPERTS].set(b_router.astype(F32))
    return _moe(x, wr, br, moe_wg, moe_wu, moe_wd, moe_layer, ln_g[1:2], ln_b[1:2])


def kernel(x, ln_g, ln_b, even_w_in, swa_sinks, dn_conv_w, dn_a_log, dn_dt_bias, dn_norm_w, even_w_out,
           ssm_w_in, ssm_conv_w, ssm_conv_b, ssm_dt_bias, ssm_a_log, ssm_d, ssm_norm_w, ssm_w_out,
           ffn_w_gate, ffn_w_up, ffn_w_down, moe_w_router, moe_b_router, moe_w_gate, moe_w_up, moe_w_down):
    bsz, t, d = x.shape
    h = x.reshape(bsz * t, d)
    rope_tables = _rope_tables(t)
    for i in range(DEPTH):
        j = i // 2
        if i % 2 == 0:
            h = _even_layer(h, bsz, t, even_w_in[j], swa_sinks[j], dn_conv_w[j], dn_a_log[j], dn_dt_bias[j],
                            dn_norm_w[j], even_w_out[j], ln_g[i], ln_b[i],
                            ffn_w_gate[j], ffn_w_up[j], ffn_w_down[j], rope_tables)
        else:
            h = _odd_layer(h, bsz, t, ssm_w_in[j], ssm_conv_w[j], ssm_conv_b[j], ssm_dt_bias[j], ssm_a_log[j],
                           ssm_d[j], ssm_norm_w[j], ssm_w_out[j], ln_g[i], ln_b[i],
                           moe_w_router[j], moe_b_router[j], moe_w_gate, moe_w_up, moe_w_down, j)
    return h.reshape(bsz, t, d)
```

```python
import functools

import numpy as np
import jax
import jax.numpy as jnp
from jax import lax
from jax.experimental import pallas as pl
from jax.experimental.pallas import tpu as pltpu

F32 = jnp.float32
BF16 = jnp.bfloat16
HIGHEST = lax.Precision.HIGHEST

D_MODEL = 1024
DEPTH = 4
ALPHA = (2 * DEPTH) ** 0.25
LN_EPS = 1e-5
RMS_EPS = 1e-6

SWA_HEADS = 8
SWA_KV_HEADS = 2
SWA_HEAD_DIM = 64
SWA_BLOCK = 128
ROPE_DIM = SWA_HEAD_DIM // 4
ROPE_THETA = 500000.0
SWA_Q = SWA_HEADS * SWA_HEAD_DIM
SWA_KV = SWA_KV_HEADS * SWA_HEAD_DIM

DN_HEADS = 4
DN_HEAD_DIM = 128
DN_CONV = 4
DN_CHUNK = 64
DN_W = DN_HEADS * DN_HEAD_DIM

SSM_D_INNER = 2 * D_MODEL
SSM_HEAD_DIM = 64
SSM_HEADS = SSM_D_INNER // SSM_HEAD_DIM
SSM_GROUPS = 4
SSM_STATE = 128
SSM_CHUNK = 128
SSM_GN = SSM_GROUPS * SSM_STATE
SSM_CONV_DIM = SSM_D_INNER + 2 * SSM_GN
SSM_GROUP_W = SSM_D_INNER // SSM_GROUPS

FFN_DIM = 2816
N_EXPERTS = 8
EXPERT_DIM = 3584

LANES = 128
HALO = 8
VMEM_LIMIT = 52 * 1024 * 1024

ROW_TILE = 512
MOE_TILE = 1024
MOE_SUBTILE = 512
MOE_FSPLIT = 7
SWA_STEP_BLOCKS = 8
DN_BLOCK = 256
DN_INV_PASSES = 1
GATHER_TILE = 1024


def _cparams(sem):
    return pltpu.CompilerParams(dimension_semantics=sem, vmem_limit_bytes=VMEM_LIMIT)


def _sigmoid(x):
    return 1.0 / (1.0 + jnp.exp(-x))


def _silu(x):
    hx = 0.5 * x
    return hx + hx * jnp.tanh(hx)


def _softplus(x):
    return jnp.maximum(x, 0.0) + jnp.log(1.0 + jnp.exp(-jnp.abs(x)))


def _layer_norm(y, g, b):
    mu = jnp.mean(y, axis=-1, keepdims=True)
    d = y - mu
    var = jnp.mean(d * d, axis=-1, keepdims=True)
    return d * lax.rsqrt(var + LN_EPS) * g + b


def _dot(a, b, precision=None):
    return jnp.dot(a, b, preferred_element_type=F32, precision=precision)


def _dot_nt(a, b, precision=None):
    return lax.dot_general(a, b, (((1,), (1,)), ((), ())), preferred_element_type=F32, precision=precision)


def _dot_tn(a, b, precision=None):
    return lax.dot_general(a, b, (((0,), (0,)), ((), ())), preferred_element_type=F32, precision=precision)


def _const_spec(shape):
    nd = len(shape)
    return pl.BlockSpec(shape, lambda *_: (0,) * nd)


def _proj_kernel(x_ref, w_ref, wt_ref, o_ref, t_ref, *, chunk):
    x = x_ref[...]
    xb, xl = _split(x)
    for c in range(0, o_ref.shape[1], chunk):
        o_ref[:, c:c + chunk] = _dot(xb, w_ref[:, c:c + chunk]).astype(o_ref.dtype)
    wth, wtl = _split(wt_ref[...])
    both = _dot(xb, jnp.concatenate([wth, wtl], axis=1))
    t_ref[...] = both[:, :LANES] + (both[:, LANES:] + _dot(xl, wth))


def _proj(x, w, wt):
    n = x.shape[0]
    c = w.shape[1]
    tm = ROW_TILE
    return pl.pallas_call(
        functools.partial(_proj_kernel, chunk=512),
        grid=(n // tm,),
        in_specs=[pl.BlockSpec((tm, D_MODEL), lambda i: (i, 0)),
                  _const_spec(w.shape), _const_spec(wt.shape)],
        out_specs=[pl.BlockSpec((tm, c), lambda i: (i, 0)),
                   pl.BlockSpec((tm, LANES), lambda i: (i, 0))],
        out_shape=[jax.ShapeDtypeStruct((n, c), BF16),
                   jax.ShapeDtypeStruct((n, LANES), F32)],
        compiler_params=_cparams(("parallel",)),
        name="proj",
    )(x, w, wt)


def _mix_out_kernel(*refs, n_in):
    a_refs = refs[:n_in]
    w_ref, x_ref, g_ref, b_ref, o_ref = refs[n_in:]
    half = x_ref.shape[0] // 2
    for r0 in (0, half):
        rows = slice(r0, r0 + half)
        a = a_refs[0][rows, :] if n_in == 1 else jnp.concatenate([r[rows, :] for r in a_refs], axis=1)
        y = ALPHA * x_ref[rows, :] + _dot(a, w_ref[...])
        o_ref[rows, :] = _layer_norm(y, g_ref[...], b_ref[...])


def _mix_out(a_list, w, x, g, b):
    n = x.shape[0]
    tm = ROW_TILE
    in_specs = [pl.BlockSpec((tm, a.shape[1]), lambda i: (i, 0)) for a in a_list]
    in_specs += [_const_spec(w.shape), pl.BlockSpec((tm, D_MODEL), lambda i: (i, 0)),
                 _const_spec(g.shape), _const_spec(b.shape)]
    return pl.pallas_call(
        functools.partial(_mix_out_kernel, n_in=len(a_list)),
        grid=(n // tm,),
        in_specs=in_specs,
        out_specs=pl.BlockSpec((tm, D_MODEL), lambda i: (i, 0)),
        out_shape=jax.ShapeDtypeStruct((n, D_MODEL), F32),
        compiler_params=_cparams(("parallel",)),
        name="mix_out",
    )(*a_list, w, x, g, b)


def _swiglu_chunks(xbs, wg_ref, wu_ref, wd_ref, chunk):
    f = wg_ref.shape[-1]
    accs = [None] * len(xbs)
    for c in range(0, f, chunk):
        w = min(chunk, f - c)
        wg = wg_ref[:, c:c + w].astype(BF16)
        wu = wu_ref[:, c:c + w].astype(BF16)
        wd = wd_ref[c:c + w, :].astype(BF16)
        for t, xb in enumerate(xbs):
            h = (_silu(_dot(xb, wg)) * _dot(xb, wu)).astype(BF16)
            part = _dot(h, wd)
            accs[t] = part if accs[t] is None else accs[t] + part
    return accs


def _ffn_kernel(x_ref, wg_ref, wu_ref, wd_ref, g_ref, b_ref, o_ref):
    x = x_ref[...]
    y, = _swiglu_chunks([x.astype(BF16)], wg_ref, wu_ref, wd_ref, 512)
    o_ref[...] = _layer_norm(ALPHA * x + y, g_ref[...], b_ref[...])


def _ffn(x, wg, wu, wd, g, b):
    n = x.shape[0]
    tm = ROW_TILE
    single = pl.Buffered(1)
    return pl.pallas_call(
        _ffn_kernel,
        grid=(n // tm,),
        in_specs=[pl.BlockSpec((tm, D_MODEL), lambda i: (i, 0)),
                  pl.BlockSpec(wg.shape, lambda i: (0, 0), pipeline_mode=single),
                  pl.BlockSpec(wu.shape, lambda i: (0, 0), pipeline_mode=single),
                  pl.BlockSpec(wd.shape, lambda i: (0, 0), pipeline_mode=single),
                  _const_spec(g.shape), _const_spec(b.shape)],
        out_specs=pl.BlockSpec((tm, D_MODEL), lambda i: (i, 0)),
        out_shape=jax.ShapeDtypeStruct((n, D_MODEL), F32),
        compiler_params=_cparams(("parallel",)),
        name="ffn",
    )(x, wg, wu, wd, g, b)


def _experts_kernel(te_ref, nv_ref, nu_ref, xs_ref, wg_ref, wu_ref, wd_ref, o_ref):
    i = pl.program_id(0)
    j = pl.program_id(1)
    valid = nv_ref[i]

    @pl.when(j == 0)
    def _():
        o_ref[...] = jnp.zeros_like(o_ref)

    def accumulate(n_sub):
        rows = [slice(s * MOE_SUBTILE, (s + 1) * MOE_SUBTILE) for s in range(n_sub)]
        parts = _swiglu_chunks([xs_ref[r, :].astype(BF16) for r in rows], wg_ref, wu_ref, wd_ref, 256)
        for r, part in zip(rows, parts):
            o_ref[r, :] += part

    n_sub_max = MOE_TILE // MOE_SUBTILE
    for n_sub in range(1, n_sub_max + 1):
        lo = (n_sub - 1) * MOE_SUBTILE
        hi = n_sub * MOE_SUBTILE
        pl.when((valid > lo) & (valid <= hi))(functools.partial(accumulate, n_sub))


def _experts(tile_expert, tile_valid, n_used, xs, wg, wu, wd, layer):
    npad = xs.shape[0]
    tm = MOE_TILE
    fb = EXPERT_DIM // MOE_FSPLIT
    last = MOE_FSPLIT - 1

    def tile(i, nu):
        return jnp.minimum(i, jnp.maximum(nu[0] - 1, 0))

    def fblock(i, j, nu):
        ju = jnp.where(i < nu[0], j, last)
        return ju + (tile(i, nu) % 2) * (last - 2 * ju)

    grid_spec = pltpu.PrefetchScalarGridSpec(
        num_scalar_prefetch=3,
        grid=(npad // tm, MOE_FSPLIT),
        in_specs=[pl.BlockSpec((tm, D_MODEL), lambda i, j, te, nv, nu: (tile(i, nu), 0)),
                  pl.BlockSpec((None, None, D_MODEL, fb),
                               lambda i, j, te, nv, nu: (layer, te[i], 0, fblock(i, j, nu))),
                  pl.BlockSpec((None, None, D_MODEL, fb),
                               lambda i, j, te, nv, nu: (layer, te[i], 0, fblock(i, j, nu))),
                  pl.BlockSpec((None, None, fb, D_MODEL),
                               lambda i, j, te, nv, nu: (layer, te[i], fblock(i, j, nu), 0))],
        out_specs=pl.BlockSpec((tm, D_MODEL), lambda i, j, te, nv, nu: (i, 0)),
    )
    return pl.pallas_call(
        _experts_kernel,
        grid_spec=grid_spec,
        out_shape=jax.ShapeDtypeStruct((npad, D_MODEL), F32),
        compiler_params=_cparams(("arbitrary", "arbitrary")),
        name="experts",
    )(tile_expert, tile_valid, n_used, xs, wg, wu, wd)


def _router_kernel(x_ref, wr_ref, br_ref, ltri_ref, o_ref, cnt_ref, carry_ref):
    @pl.when(pl.program_id(0) == 0)
    def _():
        carry_ref[...] = jnp.zeros_like(carry_ref)

    xh, xl = _split(x_ref[...])
    wh, wl = _split(wr_ref[...])
    both = _dot(xh, jnp.concatenate([wh, wl], axis=1))
    logits = both[:, :LANES] + (both[:, LANES:] + _dot(xl, wh)) + br_ref[...]
    lane = lax.broadcasted_iota(jnp.int32, logits.shape, 1)
    m1 = jnp.max(logits, axis=-1, keepdims=True)
    i1 = jnp.min(jnp.where(logits == m1, lane, LANES), axis=-1, keepdims=True)
    rest = jnp.where(lane == i1, -jnp.inf, logits)
    m2 = jnp.max(rest, axis=-1, keepdims=True)
    i2 = jnp.min(jnp.where(rest == m2, lane, LANES), axis=-1, keepdims=True)
    e2 = jnp.exp(m2 - m1)
    g1 = 1.0 / (1.0 + e2)
    g2 = e2 / (1.0 + e2)
    sel1 = lane == i1
    sel2 = lane == i2
    onehot = jnp.where(sel1, 1.0, jnp.where(sel2, 1.0, 0.0))
    before = _dot(ltri_ref[...], onehot.astype(BF16)) + carry_ref[0:1, :]
    r1 = jnp.sum(jnp.where(sel1, before, 0.0), axis=-1, keepdims=True)
    r2 = jnp.sum(jnp.where(sel2, before, 0.0), axis=-1, keepdims=True)
    new_carry = carry_ref[0:1, :] + jnp.sum(onehot, axis=0, keepdims=True)
    carry_ref[...] = jnp.broadcast_to(new_carry, carry_ref.shape)
    cnt_ref[...] = jnp.broadcast_to(new_carry, cnt_ref.shape)
    vals = (i1.astype(F32), i2.astype(F32), r1, r2, g1, g2)
    out = jnp.zeros(logits.shape, F32)
    for k, v in enumerate(vals):
        out = jnp.where(lane == k, v, out)
    o_ref[...] = out


def _router(x, wr, br):
    n = x.shape[0]
    tm = ROW_TILE
    ltri = jnp.asarray(np.tril(np.ones((tm, tm), np.float32), -1), BF16)
    return pl.pallas_call(
        _router_kernel,
        grid=(n // tm,),
        in_specs=[pl.BlockSpec((tm, D_MODEL), lambda i: (i, 0)),
                  _const_spec(wr.shape), _const_spec(br.shape), _const_spec(ltri.shape)],
        out_specs=[pl.BlockSpec((tm, LANES), lambda i: (i, 0)),
                   pl.BlockSpec((HALO, LANES), lambda i: (0, 0))],
        out_shape=[jax.ShapeDtypeStruct((n, LANES), F32),
                   jax.ShapeDtypeStruct((HALO, LANES), F32)],
        scratch_shapes=[pltpu.VMEM((HALO, LANES), F32)],
        compiler_params=_cparams(("arbitrary",)),
        name="router",
    )(x, wr, br, ltri)


def _dispatch_kernel(slot_ref, x_ref, xs_in_ref, xs_ref, sem):
    del xs_in_ref
    tg = x_ref.shape[0]

    def issue(r, carry):
        for k in range(2):
            pltpu.make_async_copy(x_ref.at[pl.ds(r, 1)], xs_ref.at[pl.ds(slot_ref[0, 0, 2 * r + k], 1)],
                                  sem).start()
        return carry

    lax.fori_loop(0, tg, issue, 0, unroll=8)
    for k in range(2):
        pltpu.make_async_copy(x_ref, xs_ref.at[pl.ds(0, tg)], sem).wait()


def _dispatch(x, slots, npad):
    n = x.shape[0]
    tg = GATHER_TILE
    slots3 = slots.reshape(n // tg, 1, 2 * tg)
    zeros = jnp.zeros((npad, D_MODEL), F32)
    return pl.pallas_call(
        _dispatch_kernel,
        grid=(n // tg,),
        in_specs=[pl.BlockSpec((1, 1, 2 * tg), lambda i: (i, 0, 0), memory_space=pltpu.SMEM),
                  pl.BlockSpec((tg, D_MODEL), lambda i: (i, 0)),
                  pl.BlockSpec(memory_space=pl.ANY)],
        out_specs=pl.BlockSpec(memory_space=pl.ANY),
        out_shape=jax.ShapeDtypeStruct((npad, D_MODEL), F32),
        scratch_shapes=[pltpu.SemaphoreType.DMA(())],
        input_output_aliases={2: 0},
        compiler_params=_cparams(("arbitrary",)),
        name="moe_dispatch",
    )(slots3, x, zeros)


def _combine_kernel(slot_ref, ys_ref, x_ref, info_ref, g_ref, b_ref, o_ref, buf_ref, sem):
    tg = x_ref.shape[0]

    def issue(r, carry):
        for k in range(2):
            pltpu.make_async_copy(ys_ref.at[pl.ds(slot_ref[0, 0, 2 * r + k], 1)],
                                  buf_ref.at[k, pl.ds(r, 1)], sem).start()
        return carry

    lax.fori_loop(0, tg, issue, 0, unroll=8)
    for k in range(2):
        pltpu.make_async_copy(ys_ref.at[pl.ds(0, tg)], buf_ref.at[k], sem).wait()
    info = info_ref[...]
    y = info[:, 4:5] * buf_ref[0] + info[:, 5:6] * buf_ref[1]
    o_ref[...] = _layer_norm(ALPHA * x_ref[...] + y, g_ref[...], b_ref[...])


def _combine(ys, slots, x, info, g, b):
    n = x.shape[0]
    tg = GATHER_TILE
    slots3 = slots.reshape(n // tg, 1, 2 * tg)
    return pl.pallas_call(
        _combine_kernel,
        grid=(n // tg,),
        in_specs=[pl.BlockSpec((1, 1, 2 * tg), lambda i: (i, 0, 0), memory_space=pltpu.SMEM),
                  pl.BlockSpec(memory_space=pl.ANY),
                  pl.BlockSpec((tg, D_MODEL), lambda i: (i, 0)),
                  pl.BlockSpec((tg, LANES), lambda i: (i, 0)),
                  _const_spec(g.shape), _const_spec(b.shape)],
        out_specs=pl.BlockSpec((tg, D_MODEL), lambda i: (i, 0)),
        out_shape=jax.ShapeDtypeStruct((n, D_MODEL), F32),
        scratch_shapes=[pltpu.VMEM((2, tg, D_MODEL), F32), pltpu.SemaphoreType.DMA(())],
        compiler_params=_cparams(("arbitrary",)),
        name="moe_combine",
    )(slots3, ys, x, info, g, b)


def _moe(x, wr, br, wg, wu, wd, layer, g, b):
    n = x.shape[0]
    tm = MOE_TILE
    info, counts = _router(x, wr, br)
    experts = info[:, 0:2].astype(jnp.int32)
    ranks = info[:, 2:4].astype(jnp.int32)
    count = counts[0, :N_EXPERTS].astype(jnp.int32)
    tiles = (count + tm - 1) // tm
    tile_end = jnp.cumsum(tiles)
    tile_start = tile_end - tiles
    slots = tile_start[experts] * tm + ranks
    n_tiles = (2 * n) // tm + N_EXPERTS
    n_used = tile_end[-1]
    tile_ids = jnp.arange(n_tiles, dtype=jnp.int32)
    tile_id = jnp.minimum(tile_ids, n_used - 1)
    tile_expert = jnp.sum((tile_id[:, None] >= tile_end[None, :]).astype(jnp.int32), axis=1)
    tile_valid = jnp.clip(count[tile_expert] - (tile_ids - tile_start[tile_expert]) * tm, 0, tm)
    tile_valid = jnp.where(tile_ids < n_used, tile_valid, 0)
    xs = _dispatch(x, slots, n_tiles * tm)
    ys = _experts(tile_expert.astype(jnp.int32), tile_valid.astype(jnp.int32),
                  n_used.reshape(1).astype(jnp.int32), xs, wg, wu, wd, layer)
    return _combine(ys, slots, x, info, g, b)


def _rope(x, c, sa, sb):
    return x * c + pltpu.roll(x, LANES - ROPE_DIM // 2, 1) * sa + pltpu.roll(x, ROPE_DIM // 2, 1) * sb


def _swa_kernel(sink_ref, q_ref, k_ref, v_ref, c_ref, sa_ref, sb_ref, o_ref, kp_ref, vp_ref):
    n = pl.program_id(1)
    blk = SWA_BLOCK

    @pl.when(n == 0)
    def _():
        kp_ref[...] = jnp.zeros_like(kp_ref)
        vp_ref[...] = jnp.zeros_like(vp_ref)

    c = c_ref[...]
    sa = sa_ref[...]
    sb = sb_ref[...]
    lane = lax.broadcasted_iota(jnp.int32, (blk, LANES), 1)
    low = lane < SWA_HEAD_DIM
    row = lax.broadcasted_iota(jnp.int32, (blk, blk), 0)
    col = lax.broadcasted_iota(jnp.int32, (blk, blk), 1)
    mask_cur = col <= row
    zero = jnp.zeros((blk, LANES), BF16)
    nq = q_ref.shape[0] // blk

    def halves(t, g):
        swapped = jnp.concatenate([t[:, SWA_HEAD_DIM:], t[:, :SWA_HEAD_DIM]], axis=1)
        src_a, src_b = (t, swapped) if g == 0 else (swapped, t)
        return jnp.where(low, src_a, zero), jnp.where(low, zero, src_b)

    both = lambda t: [halves(t, g) for g in range(SWA_KV_HEADS)]
    k_blocks = [kp_ref[...]] + [_rope(k_ref[i * blk:(i + 1) * blk, :].astype(F32), c[i * blk:(i + 1) * blk],
                                      sa[i * blk:(i + 1) * blk], sb[i * blk:(i + 1) * blk]).astype(BF16)
                                for i in range(nq)]
    v_blocks = [vp_ref[...]] + [v_ref[i * blk:(i + 1) * blk, :] for i in range(nq)]
    k_halves = [both(t) for t in k_blocks]
    v_halves = [both(t) for t in v_blocks]

    group = SWA_HEADS // SWA_KV_HEADS
    entries = [(i, h) for i in range(nq) for h in range(SWA_HEADS)]
    stack = lambda f: jnp.stack([f(i, h) for i, h in entries])
    q_pairs = [[(_rope(q_ref[i * blk:(i + 1) * blk, pr * LANES:(pr + 1) * LANES].astype(F32),
                       c[i * blk:(i + 1) * blk], sa[i * blk:(i + 1) * blk], sb[i * blk:(i + 1) * blk])
                 * (SWA_HEAD_DIM ** -0.5)).astype(BF16) for pr in range(SWA_HEADS // 2)] for i in range(nq)]
    q_all = stack(lambda i, h: q_pairs[i][h // 2])
    sink = stack(lambda i, h: jnp.full((1, 1), sink_ref[h], F32))
    s_prev = jnp.where(col > row, _bdot_nt(q_all, stack(lambda i, h: k_halves[i][h // group][h % 2])), -jnp.inf)
    s_first = jnp.where(n > 0, s_prev[:SWA_HEADS], -jnp.inf)
    s_prev = s_first if nq == 1 else jnp.concatenate([s_first, s_prev[SWA_HEADS:]], axis=0)
    s_cur = jnp.where(mask_cur, _bdot_nt(q_all, stack(lambda i, h: k_halves[i + 1][h // group][h % 2])), -jnp.inf)
    m = jnp.maximum(jnp.maximum(jnp.max(s_prev, axis=-1, keepdims=True),
                                jnp.max(s_cur, axis=-1, keepdims=True)), sink)
    p_prev = jnp.exp(s_prev - m)
    p_cur = jnp.exp(s_cur - m)
    denom = (jnp.sum(p_prev, axis=-1, keepdims=True) + jnp.sum(p_cur, axis=-1, keepdims=True)
             + jnp.exp(sink - m))
    inv = 1.0 / denom
    out = (_bdot((p_prev * inv).astype(BF16), stack(lambda i, h: v_halves[i][h // group][h % 2]))
           + _bdot((p_cur * inv).astype(BF16), stack(lambda i, h: v_halves[i + 1][h // group][h % 2])))
    for i in range(nq):
        for pr in range(SWA_HEADS // 2):
            e = i * SWA_HEADS + 2 * pr
            o_ref[i * blk:(i + 1) * blk, pr * LANES:(pr + 1) * LANES] = (out[e] + out[e + 1]).astype(o_ref.dtype)

    kp_ref[...] = k_blocks[nq]
    vp_ref[...] = v_blocks[nq]


def _swa(p, sinks, tables, bsz, t, q_col, k_col, v_col):
    rows = SWA_STEP_BLOCKS * SWA_BLOCK
    nb = t // rows
    c, sa, sb = tables
    row = lambda b, n: b * nb + n
    return pl.pallas_call(
        _swa_kernel,
        grid=(bsz, nb),
        in_specs=[pl.BlockSpec(memory_space=pltpu.SMEM),
                  pl.BlockSpec((rows, SWA_Q), lambda b, n: (row(b, n), q_col)),
                  pl.BlockSpec((rows, SWA_KV), lambda b, n: (row(b, n), k_col)),
                  pl.BlockSpec((rows, SWA_KV), lambda b, n: (row(b, n), v_col)),
                  pl.BlockSpec((rows, LANES), lambda b, n: (n, 0)),
                  pl.BlockSpec((rows, LANES), lambda b, n: (n, 0)),
                  pl.BlockSpec((rows, LANES), lambda b, n: (n, 0))],
        out_specs=pl.BlockSpec((rows, SWA_Q), lambda b, n: (row(b, n), 0)),
        out_shape=jax.ShapeDtypeStruct((bsz * t, SWA_Q), BF16),
        scratch_shapes=[pltpu.VMEM((SWA_BLOCK, SWA_KV), BF16), pltpu.VMEM((SWA_BLOCK, SWA_KV), BF16)],
        compiler_params=_cparams(("parallel", "arbitrary")),
        name="swa",
    )(sinks, p, p, p, c, sa, sb)


def _rope_tables(t):
    half = ROPE_DIM // 2
    pos = jnp.arange(t, dtype=jnp.int32)
    inv_freq = jnp.power(ROPE_THETA, -jnp.arange(half, dtype=F32) / half)
    ang = pos.astype(F32)[:, None] * inv_freq[None, :]
    cos = jnp.cos(ang)
    sin = jnp.sin(ang)
    ones = jnp.ones((t, SWA_HEAD_DIM - ROPE_DIM), F32)
    zeros = jnp.zeros((t, SWA_HEAD_DIM - ROPE_DIM), F32)
    zhalf = jnp.zeros((t, half), F32)
    c = jnp.concatenate([cos, cos, ones], axis=1)
    sa = jnp.concatenate([-sin, zhalf, zeros], axis=1)
    sb = jnp.concatenate([zhalf, sin, zeros], axis=1)
    two = lambda a: jnp.concatenate([a, a], axis=1)
    return two(c), two(sa), two(sb)


def _split(a):
    hi = a.astype(BF16)
    return hi, (a - hi.astype(F32)).astype(BF16)


def _bdot(a, b):
    return lax.dot_general(a, b, (((2,), (1,)), ((0,), (0,))), preferred_element_type=F32)


def _bdot_nt(a, b):
    return lax.dot_general(a, b, (((2,), (2,)), ((0,), (0,))), preferred_element_type=F32)


def _mm(a, b):
    if DN_INV_PASSES == 1:
        return _bdot(a.astype(BF16), b.astype(BF16))
    ah, al = _split(a)
    bh, bl = _split(b)
    return _bdot(ah, bh) + (_bdot(ah, bl) + _bdot(al, bh))


def _unit_lower_inverse(a, eye, blk_mask, merge_masks):
    d0 = jnp.where(blk_mask, a, 0.0)
    x = eye - d0
    p = _mm(d0, d0)
    x = x + _mm(x, p)
    p = _mm(p, p)
    x = x + _mm(x, p)
    for m in merge_masks:
        lm = jnp.where(m, a, 0.0)
        x = x - _mm(x, _mm(lm, x))
    return x


def _dn_kernel(q_ref, k_ref, v_ref, gate_ref, tail_ref, cw_ref, par_ref, nw_ref, o_ref,
               xpad_ref, qkv_ref, bg_ref, s_ref, gc_ref, u_ref, oi_ref, kd_ref, wq_ref):
    n = pl.program_id(1)
    tb = q_ref.shape[0]
    ch = DN_CHUNK
    hd = DN_HEAD_DIM

    @pl.when(n == 0)
    def _():
        xpad_ref[0:HALO, :] = jnp.zeros((HALO, 3 * DN_W), F32)
        s_ref[...] = jnp.zeros_like(s_ref)

    xpad_ref[HALO:HALO + tb, 0:DN_W] = q_ref[...].astype(F32)
    xpad_ref[HALO:HALO + tb, DN_W:2 * DN_W] = k_ref[...].astype(F32)
    xpad_ref[HALO:HALO + tb, 2 * DN_W:3 * DN_W] = v_ref[...].astype(F32)
    conv = None
    for kk in range(DN_CONV):
        off = HALO - (DN_CONV - 1) + kk
        term = cw_ref[kk:kk + 1, :] * xpad_ref[off:off + tb, :]
        conv = term if conv is None else conv + term
    xpad_ref[0:HALO, :] = xpad_ref[tb:tb + HALO, :]
    qkv = _silu(conv)
    for h in range(2 * DN_HEADS):
        xh = qkv[:, h * hd:(h + 1) * hd]
        scale = lax.rsqrt(jnp.sum(xh * xh, axis=-1, keepdims=True) + RMS_EPS)
        if h < DN_HEADS:
            scale = scale * (hd ** -0.5)
        qkv_ref[:, h * hd:(h + 1) * hd] = xh * scale
    qkv_ref[:, 2 * DN_W:] = qkv[:, 2 * DN_W:]

    tail = tail_ref[...]
    lane = lax.broadcasted_iota(jnp.int32, tail.shape, 1)
    gval = -jnp.exp(par_ref[0:1, :]) * _softplus(tail + par_ref[1:2, :])
    bg_ref[...] = jnp.where(lane < DN_HEADS, _sigmoid(tail), gval)

    rt = lax.broadcasted_iota(jnp.int32, (tb, tb), 0)
    ct = lax.broadcasted_iota(jnp.int32, (tb, tb), 1)
    ltri = jnp.where((rt >= ct) & ((rt // ch) == (ct // ch)), 1.0, 0.0).astype(F32)
    er = lax.broadcasted_iota(jnp.int32, (LANES, LANES), 0)
    ec = lax.broadcasted_iota(jnp.int32, (LANES, LANES), 1)
    eye = jnp.where(er == ec, 1.0, 0.0).astype(F32)
    bg = bg_ref[...]
    gc_all = _dot(ltri, jnp.where(lane >= DN_HEADS, bg, 0.0), HIGHEST)
    gc_t = _dot_nt(eye, gc_all, HIGHEST)
    gc_ref[...] = gc_all

    sb = 2 * ch
    same = (er // ch) == (ec // ch)
    causal = same & (er >= ec)
    strict = same & (er > ec)
    blk_mask = (er > ec) & ((er // 8) == (ec // 8))
    merge_masks = [((er // (2 * s)) == (ec // (2 * s))) & ((er % (2 * s)) >= s) & ((ec % (2 * s)) < s)
                   for s in (8, 16, 32)]
    first = lax.broadcasted_iota(jnp.int32, (sb, 1), 0) < ch
    probs = [(h, b2) for h in range(DN_HEADS) for b2 in range(tb // sb)]
    rows_of = lambda b2: slice(b2 * sb, (b2 + 1) * sb)
    stack = lambda f: jnp.stack([f(h, b2) for h, b2 in probs])
    gcol = stack(lambda h, b2: gc_all[rows_of(b2), DN_HEADS + h:DN_HEADS + h + 1])
    grow = stack(lambda h, b2: gc_t[DN_HEADS + h:DN_HEADS + h + 1, rows_of(b2)])
    beta = stack(lambda h, b2: bg[rows_of(b2), h:h + 1])
    q = stack(lambda h, b2: qkv_ref[rows_of(b2), h * hd:(h + 1) * hd])
    k = stack(lambda h, b2: qkv_ref[rows_of(b2), DN_W + h * hd:DN_W + (h + 1) * hd])
    v = stack(lambda h, b2: qkv_ref[rows_of(b2), 2 * DN_W + h * hd:2 * DN_W + (h + 1) * hd])
    decay = jnp.where(causal, jnp.exp(gcol - grow), 0.0)
    k16 = k.astype(BF16)
    a = jnp.where(strict, _bdot_nt(k16, k16) * decay * beta, 0.0)
    tinv = _unit_lower_inverse(a, eye, blk_mask, merge_masks)
    egc = jnp.exp(gcol)
    u = _mm(tinv, v * beta)
    w16 = _mm(tinv, k * (beta * egc)).astype(BF16)
    attn16 = (_bdot_nt(q.astype(BF16), k16) * decay).astype(BF16)
    q_eff16 = (q * egc - _bdot(attn16, w16)).astype(BF16)
    o_intra = _bdot(attn16, u.astype(BF16))
    glast = jnp.where(first, gcol[:, ch - 1:ch, :], gcol[:, sb - 1:sb, :])
    k_dec16 = (k * jnp.exp(glast - gcol)).astype(BF16)
    for g, (h, b2) in enumerate(probs):
        u_ref[h, rows_of(b2), :] = u[g]
        oi_ref[h, rows_of(b2), :] = o_intra[g]
        kd_ref[h, rows_of(b2), :] = k_dec16[g]
        for c in range(2):
            wq_ref[h, 2 * b2 + c, 0:ch, :] = w16[g, c * ch:(c + 1) * ch]
            wq_ref[h, 2 * b2 + c, ch:sb, :] = q_eff16[g, c * ch:(c + 1) * ch]

    for c in range(tb // ch):
        rows = slice(c * ch, (c + 1) * ch)
        s = s_ref[...]
        r = _bdot(wq_ref[:, c], s.astype(BF16))
        v_new16 = (u_ref[:, rows, :] - r[:, 0:ch]).astype(BF16)
        for h in range(DN_HEADS):
            glast = gc_ref[(c + 1) * ch - 1:(c + 1) * ch, DN_HEADS + h:DN_HEADS + h + 1]
            s_ref[h] = s[h] * jnp.exp(glast) + _dot_tn(kd_ref[h, rows, :], v_new16[h])
        for h in range(DN_HEADS):
            o = r[h, ch:sb] + oi_ref[h, rows, :]
            gate = gate_ref[rows, h * hd:(h + 1) * hd].astype(F32)
            on = o * lax.rsqrt(jnp.mean(o * o, axis=-1, keepdims=True) + RMS_EPS) * nw_ref[...]
            o_ref[rows, h * hd:(h + 1) * hd] = (on * _silu(gate)).astype(o_ref.dtype)


def _deltanet(p, tail, conv_w, par, norm_w, bsz, t, cols):
    tb = DN_BLOCK
    nb = t // tb
    row = lambda b, n: b * nb + n
    col_spec = lambda cidx: pl.BlockSpec((tb, DN_W), lambda b, n: (row(b, n), cidx))
    return pl.pallas_call(
        _dn_kernel,
        grid=(bsz, nb),
        in_specs=[col_spec(cols[0]), col_spec(cols[1]), col_spec(cols[2]), col_spec(cols[3]),
                  pl.BlockSpec((tb, LANES), lambda b, n: (row(b, n), 0)),
                  _const_spec(conv_w.shape), _const_spec(par.shape), _const_spec(norm_w.shape)],
        out_specs=pl.BlockSpec((tb, DN_W), lambda b, n: (row(b, n), 0)),
        out_shape=jax.ShapeDtypeStruct((bsz * t, DN_W), BF16),
        scratch_shapes=[pltpu.VMEM((tb + HALO, 3 * DN_W), F32),
                        pltpu.VMEM((tb, 3 * DN_W), F32),
                        pltpu.VMEM((tb, LANES), F32),
                        pltpu.VMEM((DN_HEADS, DN_HEAD_DIM, DN_HEAD_DIM), F32),
                        pltpu.VMEM((tb, LANES), F32),
                        pltpu.VMEM((DN_HEADS, tb, DN_HEAD_DIM), F32),
                        pltpu.VMEM((DN_HEADS, tb, DN_HEAD_DIM), F32),
                        pltpu.VMEM((DN_HEADS, tb, DN_HEAD_DIM), BF16),
                        pltpu.VMEM((DN_HEADS, tb // DN_CHUNK, 2 * DN_CHUNK, DN_HEAD_DIM), BF16)],
        compiler_params=_cparams(("parallel", "arbitrary")),
        name="deltanet",
    )(p, p, p, p, tail, conv_w, par, norm_w)


def _ssd_kernel(z_ref, xs_ref, bc_ref, dt_ref, cw_ref, cb_ref, pcol_ref, prow_ref,
                dskip_ref, nw_ref, expand_ref, o_ref, xpad_ref, xc_ref, h_ref):
    n = pl.program_id(1)
    L = SSM_CHUNK

    @pl.when(n == 0)
    def _():
        xpad_ref[0:L, :] = jnp.zeros((L, SSM_CONV_DIM), BF16)
        h_ref[...] = jnp.zeros_like(h_ref)

    xpad_ref[L:2 * L, 0:SSM_D_INNER] = xs_ref[...]
    xpad_ref[L:2 * L, SSM_D_INNER:] = bc_ref[...]
    sr = lax.broadcasted_iota(jnp.int32, (L, 2 * L), 0)
    sc = lax.broadcasted_iota(jnp.int32, (L, 2 * L), 1)
    shifts = [jnp.where(sc == sr + (L - 3 + kk), 1.0, 0.0).astype(BF16) for kk in range(3)]
    cw_blk = 256
    for c0 in range(0, SSM_CONV_DIM, cw_blk):
        cols = slice(c0, c0 + cw_blk)
        xe = xpad_ref[:, cols]
        conv = cb_ref[:, cols] + cw_ref[3:4, cols] * xe[L:2 * L].astype(F32)
        for kk in range(3):
            conv = conv + cw_ref[kk:kk + 1, cols] * _dot(shifts[kk], xe)
        xc_ref[:, cols] = _silu(conv)
    xpad_ref[0:L, :] = xpad_ref[L:2 * L, :]

    ri = lax.broadcasted_iota(jnp.int32, (L, L), 0)
    ci = lax.broadcasted_iota(jnp.int32, (L, L), 1)
    causal = ri >= ci
    ltri = jnp.where(causal, 1.0, 0.0).astype(F32)
    utri = jnp.where(ri <= ci, 1.0, 0.0).astype(F32)
    eye = jnp.where(ri == ci, 1.0, 0.0).astype(F32)

    dt_raw = dt_ref[...]
    dt = _softplus(dt_raw + pcol_ref[0:1, :])
    acs = _dot(ltri, dt * pcol_ref[1:2, :], HIGHEST)
    tot = acs[L - 1:L, :]
    f1 = dt * jnp.exp(tot - acs)
    ea = jnp.exp(acs)
    dt_r = _softplus(_dot_nt(eye, dt_raw, HIGHEST)[0:SSM_HEADS] + prow_ref[0])
    acs_r = _dot(dt_r * prow_ref[1], utri, HIGHEST)

    stacked = jnp.concatenate([dt, f1, ea], axis=0).astype(BF16)
    cd = jnp.broadcast_to(jnp.exp(tot), (HALO, LANES))
    cd_hi, cd_lo = _split(cd)
    lane = lax.broadcasted_iota(jnp.int32, (L, LANES), 1)
    low = lane < SSM_HEAD_DIM
    zero16 = jnp.zeros((L, LANES), BF16)
    for g in range(SSM_GROUPS):
        gl = slice(g * SSM_GROUP_W, (g + 1) * SSM_GROUP_W)
        ex = expand_ref[:, gl]
        wide = _dot(stacked, ex)
        cd_x = (_dot(cd_hi, ex) + _dot(cd_lo, ex))[0:1, :]
        xs = xc_ref[:, gl]
        xdt16 = (xs * wide[0:L]).astype(BF16)
        xdd16 = (xs * wide[L:2 * L]).astype(BF16)
        bm = xc_ref[:, SSM_D_INNER + g * SSM_STATE:SSM_D_INNER + (g + 1) * SSM_STATE].astype(BF16)
        cm = xc_ref[:, SSM_D_INNER + SSM_GN + g * SSM_STATE:
                    SSM_D_INNER + SSM_GN + (g + 1) * SSM_STATE].astype(BF16)
        cb = _dot_nt(cm, bm)
        hg = h_ref[g]
        y_off = _dot(cm, hg.astype(BF16)) * wide[2 * L:3 * L]
        h_ref[g] = hg * cd_x + _dot_tn(bm, xdd16)
        parts = []
        for j in range(SSM_GROUP_W // LANES):
            xp16 = xdt16[:, j * LANES:(j + 1) * LANES]
            yd = None
            for side in range(2):
                hidx = (g * SSM_GROUP_W + j * LANES) // SSM_HEAD_DIM + side
                seg = jnp.where(causal, jnp.exp(acs[:, hidx:hidx + 1] - acs_r[hidx:hidx + 1, :]), 0.0)
                m = (cb * seg).astype(BF16)
                xm = jnp.where(low, xp16, zero16) if side == 0 else jnp.where(low, zero16, xp16)
                part = _dot(m, xm)
                yd = part if yd is None else yd + part
            parts.append(yd)
        y = jnp.concatenate(parts, axis=1) + y_off + dskip_ref[:, gl] * xs
        y = y * _silu(z_ref[:, gl].astype(F32))
        yn = y * lax.rsqrt(jnp.mean(y * y, axis=-1, keepdims=True) + RMS_EPS) * nw_ref[:, gl]
        o_ref[:, gl] = yn.astype(o_ref.dtype)


def _ssd(p, dt, conv_w, conv_b, pcol, prow, dskip, norm_w, expand, bsz, t):
    L = SSM_CHUNK
    nb = t // L
    row = lambda b, n: b * nb + n
    return pl.pallas_call(
        _ssd_kernel,
        grid=(bsz, nb),
        in_specs=[pl.BlockSpec((L, SSM_D_INNER), lambda b, n: (row(b, n), 0)),
                  pl.BlockSpec((L, SSM_D_INNER), lambda b, n: (row(b, n), 1)),
                  pl.BlockSpec((L, 2 * SSM_GN), lambda b, n: (row(b, n), 2 * SSM_D_INNER // (2 * SSM_GN))),
                  pl.BlockSpec((L, LANES), lambda b, n: (row(b, n), 0)),
                  _const_spec(conv_w.shape), _const_spec(conv_b.shape), _const_spec(pcol.shape),
                  _const_spec(prow.shape), _const_spec(dskip.shape), _const_spec(norm_w.shape),
                  _const_spec(expand.shape)],
        out_specs=pl.BlockSpec((L, SSM_D_INNER), lambda b, n: (row(b, n), 0)),
        out_shape=jax.ShapeDtypeStruct((bsz * t, SSM_D_INNER), BF16),
        scratch_shapes=[pltpu.VMEM((2 * L, SSM_CONV_DIM), BF16),
                        pltpu.VMEM((L, SSM_CONV_DIM), F32),
                        pltpu.VMEM((SSM_GROUPS, SSM_STATE, SSM_GROUP_W), F32)],
        compiler_params=_cparams(("parallel", "arbitrary")),
        name="ssd",
    )(p, p, p, dt, conv_w, conv_b, pcol, prow, dskip, norm_w, expand)


def _pad_lanes(v, offset=0):
    v = v.astype(F32)
    return jnp.zeros((1, LANES), F32).at[0, offset:offset + v.shape[0]].set(v)


def _even_layer(x, bsz, t, w_in, sinks, conv_w, a_log, dt_bias, norm_w, w_out, ln_g, ln_b,
                ffn_wg, ffn_wu, ffn_wd, rope_tables):
    o = np.cumsum((0, SWA_Q, SWA_KV, SWA_KV, DN_W, DN_W, DN_W, DN_W, DN_HEADS, DN_HEADS))
    seg = lambda i: w_in[:, o[i]:o[i + 1]]
    w_main = jnp.concatenate([seg(0), seg(3), seg(4), seg(5), seg(6), seg(1), seg(2)], axis=1).astype(BF16)
    w_tail = w_in[:, o[7]:o[9]]
    wt = jnp.zeros((D_MODEL, LANES), F32).at[:, :2 * DN_HEADS].set(w_tail)
    p, tail = _proj(x, w_main, wt)
    out_a = _swa(p, sinks.astype(F32), rope_tables, bsz, t,
                 q_col=0, k_col=(SWA_Q + 4 * DN_W) // SWA_KV, v_col=(SWA_Q + 4 * DN_W) // SWA_KV + 1)
    par = jnp.concatenate([_pad_lanes(a_log, DN_HEADS), _pad_lanes(dt_bias, DN_HEADS)], axis=0)
    out_b = _deltanet(p, tail, conv_w.astype(F32), par, norm_w.reshape(1, DN_HEAD_DIM).astype(F32),
                      bsz, t, cols=(1, 2, 3, 4))
    x = _mix_out([out_a, out_b], w_out.astype(BF16), x, ln_g[0:1], ln_b[0:1])
    return _ffn(x, ffn_wg.astype(BF16), ffn_wu.astype(BF16), ffn_wd.astype(BF16), ln_g[1:2], ln_b[1:2])


def _odd_layer(x, bsz, t, w_in, conv_w, conv_b, dt_bias, a_log, d_skip, norm_w, w_out, ln_g, ln_b,
               w_router, b_router, moe_wg, moe_wu, moe_wd, moe_layer):
    main = SSM_D_INNER + SSM_CONV_DIM
    w_tail = w_in[:, main:]
    wt = jnp.zeros((D_MODEL, LANES), F32).at[:, :SSM_HEADS].set(w_tail)
    p, dt = _proj(x, w_in[:, :main].astype(BF16), wt)
    a = -jnp.exp(a_log.astype(F32))
    pcol = jnp.concatenate([_pad_lanes(dt_bias), _pad_lanes(a)], axis=0)
    prow = jnp.stack([jnp.broadcast_to(dt_bias.astype(F32)[:, None], (SSM_HEADS, SSM_CHUNK)),
                      jnp.broadcast_to(a[:, None], (SSM_HEADS, SSM_CHUNK))])
    dskip = jnp.repeat(d_skip.astype(F32), SSM_HEAD_DIM)[None, :]
    expand = (jnp.arange(LANES)[:, None] == (jnp.arange(SSM_D_INNER) // SSM_HEAD_DIM)[None, :]).astype(BF16)
    y = _ssd(p, dt, conv_w.astype(F32), conv_b.astype(F32)[None, :], pcol, prow, dskip,
             norm_w.astype(F32)[None, :], expand, bsz, t)
    x = _mix_out([y], w_out.astype(BF16), x, ln_g[0:1], ln_b[0:1])
    wr = jnp.zeros((D_MODEL, LANES), F32).at[:, :N_EXPERTS].set(w_router)
    br = jnp.full((1, LANES), -1e30, F32).at[0, :N_EXPERTS].set(b_router.astype(F32))
    return _moe(x, wr, br, moe_wg, moe_wu, moe_wd, moe_layer, ln_g[1:2], ln_b[1:2])


def kernel(x, ln_g, ln_b, even_w_in, swa_sinks, dn_conv_w, dn_a_log, dn_dt_bias, dn_norm_w, even_w_out,
           ssm_w_in, ssm_conv_w, ssm_conv_b, ssm_dt_bias, ssm_a_log, ssm_d, ssm_norm_w, ssm_w_out,
           ffn_w_gate, ffn_w_up, ffn_w_down, moe_w_router, moe_b_router, moe_w_gate, moe_w_up, moe_w_down):
    bsz, t, d = x.shape
    h = x.reshape(bsz * t, d)
    rope_tables = _rope_tables(t)
    for i in range(DEPTH):
        j = i // 2
        if i % 2 == 0:
            h = _even_layer(h, bsz, t, even_w_in[j], swa_sinks[j], dn_conv_w[j], dn_a_log[j], dn_dt_bias[j],
                            dn_norm_w[j], even_w_out[j], ln_g[i], ln_b[i],
                            ffn_w_gate[j], ffn_w_up[j], ffn_w_down[j], rope_tables)
        else:
            h = _odd_layer(h, bsz, t, ssm_w_in[j], ssm_conv_w[j], ssm_conv_b[j], ssm_dt_bias[j], ssm_a_log[j],
                           ssm_d[j], ssm_norm_w[j], ssm_w_out[j], ln_g[i], ln_b[i],
                           moe_w_router[j], moe_b_router[j], moe_w_gate, moe_w_up, moe_w_down, j)
    return h.reshape(bsz, t, d)
```

```python
import functools

import numpy as np
import jax
import jax.numpy as jnp
from jax import lax
from jax.experimental import pallas as pl
from jax.experimental.pallas import tpu as pltpu

F32 = jnp.float32
BF16 = jnp.bfloat16
HIGHEST = lax.Precision.HIGHEST

D_MODEL = 1024
DEPTH = 4
ALPHA = (2 * DEPTH) ** 0.25
LN_EPS = 1e-5
RMS_EPS = 1e-6

SWA_HEADS = 8
SWA_KV_HEADS = 2
SWA_HEAD_DIM = 64
SWA_BLOCK = 128
ROPE_DIM = SWA_HEAD_DIM // 4
ROPE_THETA = 500000.0
SWA_Q = SWA_HEADS * SWA_HEAD_DIM
SWA_KV = SWA_KV_HEADS * SWA_HEAD_DIM

DN_HEADS = 4
DN_HEAD_DIM = 128
DN_CONV = 4
DN_CHUNK = 64
DN_W = DN_HEADS * DN_HEAD_DIM

SSM_D_INNER = 2 * D_MODEL
SSM_HEAD_DIM = 64
SSM_HEADS = SSM_D_INNER // SSM_HEAD_DIM
SSM_GROUPS = 4
SSM_STATE = 128
SSM_CHUNK = 128
SSM_GN = SSM_GROUPS * SSM_STATE
SSM_CONV_DIM = SSM_D_INNER + 2 * SSM_GN
SSM_GROUP_W = SSM_D_INNER // SSM_GROUPS

FFN_DIM = 2816
N_EXPERTS = 8
EXPERT_DIM = 3584

LANES = 128
HALO = 8
VMEM_LIMIT = 52 * 1024 * 1024

ROW_TILE = 512
FFN_ROW_TILES = 2
MOE_TILE = 1024
MOE_SUBTILE = 512
MOE_FSPLIT = 7
SWA_STEP_BLOCKS = 8
DN_BLOCK = 256
DN_INV_PASSES = 1
GATHER_TILE = 1024


def _cparams(sem):
    return pltpu.CompilerParams(dimension_semantics=sem, vmem_limit_bytes=VMEM_LIMIT)


def _sigmoid(x):
    return 1.0 / (1.0 + jnp.exp(-x))


def _silu(x):
    hx = 0.5 * x
    return hx + hx * jnp.tanh(hx)


def _softplus(x):
    return jnp.maximum(x, 0.0) + jnp.log(1.0 + jnp.exp(-jnp.abs(x)))


def _layer_norm(y, g, b):
    mu = jnp.mean(y, axis=-1, keepdims=True)
    d = y - mu
    var = jnp.mean(d * d, axis=-1, keepdims=True)
    return d * lax.rsqrt(var + LN_EPS) * g + b


def _dot(a, b, precision=None):
    return jnp.dot(a, b, preferred_element_type=F32, precision=precision)


def _dot_nt(a, b, precision=None):
    return lax.dot_general(a, b, (((1,), (1,)), ((), ())), preferred_element_type=F32, precision=precision)


def _dot_tn(a, b, precision=None):
    return lax.dot_general(a, b, (((0,), (0,)), ((), ())), preferred_element_type=F32, precision=precision)


def _const_spec(shape):
    nd = len(shape)
    return pl.BlockSpec(shape, lambda *_: (0,) * nd)


def _proj_kernel(x_ref, w_ref, wt_ref, o_ref, t_ref, *, chunk):
    x = x_ref[...]
    xb, xl = _split(x)
    for c in range(0, o_ref.shape[1], chunk):
        o_ref[:, c:c + chunk] = _dot(xb, w_ref[:, c:c + chunk]).astype(o_ref.dtype)
    wth, wtl = _split(wt_ref[...])
    both = _dot(xb, jnp.concatenate([wth, wtl], axis=1))
    t_ref[...] = both[:, :LANES] + (both[:, LANES:] + _dot(xl, wth))


def _proj(x, w, wt):
    n = x.shape[0]
    c = w.shape[1]
    tm = ROW_TILE
    return pl.pallas_call(
        functools.partial(_proj_kernel, chunk=512),
        grid=(n // tm,),
        in_specs=[pl.BlockSpec((tm, D_MODEL), lambda i: (i, 0)),
                  _const_spec(w.shape), _const_spec(wt.shape)],
        out_specs=[pl.BlockSpec((tm, c), lambda i: (i, 0)),
                   pl.BlockSpec((tm, LANES), lambda i: (i, 0))],
        out_shape=[jax.ShapeDtypeStruct((n, c), BF16),
                   jax.ShapeDtypeStruct((n, LANES), F32)],
        compiler_params=_cparams(("parallel",)),
        name="proj",
    )(x, w, wt)


def _mix_out_kernel(*refs, n_in):
    a_refs = refs[:n_in]
    w_ref, x_ref, g_ref, b_ref, o_ref = refs[n_in:]
    half = x_ref.shape[0] // 2
    for r0 in (0, half):
        rows = slice(r0, r0 + half)
        a = a_refs[0][rows, :] if n_in == 1 else jnp.concatenate([r[rows, :] for r in a_refs], axis=1)
        y = ALPHA * x_ref[rows, :] + _dot(a, w_ref[...])
        o_ref[rows, :] = _layer_norm(y, g_ref[...], b_ref[...])


def _mix_out(a_list, w, x, g, b):
    n = x.shape[0]
    tm = 2 * ROW_TILE
    in_specs = [pl.BlockSpec((tm, a.shape[1]), lambda i: (i, 0)) for a in a_list]
    in_specs += [_const_spec(w.shape), pl.BlockSpec((tm, D_MODEL), lambda i: (i, 0)),
                 _const_spec(g.shape), _const_spec(b.shape)]
    return pl.pallas_call(
        functools.partial(_mix_out_kernel, n_in=len(a_list)),
        grid=(n // tm,),
        in_specs=in_specs,
        out_specs=pl.BlockSpec((tm, D_MODEL), lambda i: (i, 0)),
        out_shape=jax.ShapeDtypeStruct((n, D_MODEL), F32),
        compiler_params=_cparams(("parallel",)),
        name="mix_out",
    )(*a_list, w, x, g, b)


def _swiglu_chunks(xbs, wg_ref, wu_ref, wd_ref, chunk):
    f = wg_ref.shape[-1]
    accs = [None] * len(xbs)
    for c in range(0, f, chunk):
        w = min(chunk, f - c)
        wg = wg_ref[:, c:c + w].astype(BF16)
        wu = wu_ref[:, c:c + w].astype(BF16)
        wd = wd_ref[c:c + w, :].astype(BF16)
        for t, xb in enumerate(xbs):
            h = (_silu(_dot(xb, wg)) * _dot(xb, wu)).astype(BF16)
            part = _dot(h, wd)
            accs[t] = part if accs[t] is None else accs[t] + part
    return accs


def _ffn_kernel(x_ref, wg_ref, wu_ref, wd_ref, g_ref, b_ref, o_ref):
    rows = [slice(r, r + ROW_TILE) for r in range(0, x_ref.shape[0], ROW_TILE)]
    ys = _swiglu_chunks([x_ref[r, :].astype(BF16) for r in rows], wg_ref, wu_ref, wd_ref, 512)
    for r, y in zip(rows, ys):
        o_ref[r, :] = _layer_norm(ALPHA * x_ref[r, :] + y, g_ref[...], b_ref[...])


def _ffn(x, wg, wu, wd, g, b):
    n = x.shape[0]
    tm = FFN_ROW_TILES * ROW_TILE
    single = pl.Buffered(1)
    return pl.pallas_call(
        _ffn_kernel,
        grid=(n // tm,),
        in_specs=[pl.BlockSpec((tm, D_MODEL), lambda i: (i, 0)),
                  pl.BlockSpec(wg.shape, lambda i: (0, 0), pipeline_mode=single),
                  pl.BlockSpec(wu.shape, lambda i: (0, 0), pipeline_mode=single),
                  pl.BlockSpec(wd.shape, lambda i: (0, 0), pipeline_mode=single),
                  _const_spec(g.shape), _const_spec(b.shape)],
        out_specs=pl.BlockSpec((tm, D_MODEL), lambda i: (i, 0)),
        out_shape=jax.ShapeDtypeStruct((n, D_MODEL), F32),
        compiler_params=_cparams(("parallel",)),
        name="ffn",
    )(x, wg, wu, wd, g, b)


def _experts_kernel(te_ref, nv_ref, nu_ref, xs_ref, wg_ref, wu_ref, wd_ref, o_ref):
    i = pl.program_id(0)
    j = pl.program_id(1)
    valid = nv_ref[i]

    @pl.when(j == 0)
    def _():
        o_ref[...] = jnp.zeros_like(o_ref)

    def accumulate(n_sub):
        rows = [slice(s * MOE_SUBTILE, (s + 1) * MOE_SUBTILE) for s in range(n_sub)]
        parts = _swiglu_chunks([xs_ref[r, :].astype(BF16) for r in rows], wg_ref, wu_ref, wd_ref, 256)
        for r, part in zip(rows, parts):
            o_ref[r, :] += part

    n_sub_max = MOE_TILE // MOE_SUBTILE
    for n_sub in range(1, n_sub_max + 1):
        lo = (n_sub - 1) * MOE_SUBTILE
        hi = n_sub * MOE_SUBTILE
        pl.when((valid > lo) & (valid <= hi))(functools.partial(accumulate, n_sub))


def _experts(tile_expert, tile_valid, n_used, xs, wg, wu, wd, layer):
    npad = xs.shape[0]
    tm = MOE_TILE
    fb = EXPERT_DIM // MOE_FSPLIT
    last = MOE_FSPLIT - 1

    def tile(i, nu):
        return jnp.minimum(i, jnp.maximum(nu[0] - 1, 0))

    def fblock(i, j, nu):
        ju = jnp.where(i < nu[0], j, last)
        return ju + (tile(i, nu) % 2) * (last - 2 * ju)

    grid_spec = pltpu.PrefetchScalarGridSpec(
        num_scalar_prefetch=3,
        grid=(npad // tm, MOE_FSPLIT),
        in_specs=[pl.BlockSpec((tm, D_MODEL), lambda i, j, te, nv, nu: (tile(i, nu), 0)),
                  pl.BlockSpec((None, None, D_MODEL, fb),
                               lambda i, j, te, nv, nu: (layer, te[i], 0, fblock(i, j, nu))),
                  pl.BlockSpec((None, None, D_MODEL, fb),
                               lambda i, j, te, nv, nu: (layer, te[i], 0, fblock(i, j, nu))),
                  pl.BlockSpec((None, None, fb, D_MODEL),
                               lambda i, j, te, nv, nu: (layer, te[i], fblock(i, j, nu), 0))],
        out_specs=pl.BlockSpec((tm, D_MODEL), lambda i, j, te, nv, nu: (i, 0)),
    )
    return pl.pallas_call(
        _experts_kernel,
        grid_spec=grid_spec,
        out_shape=jax.ShapeDtypeStruct((npad, D_MODEL), F32),
        compiler_params=_cparams(("arbitrary", "arbitrary")),
        name="experts",
    )(tile_expert, tile_valid, n_used, xs, wg, wu, wd)


def _router_kernel(x_ref, wr_ref, br_ref, ltri_ref, o_ref, cnt_ref, carry_ref):
    @pl.when(pl.program_id(0) == 0)
    def _():
        carry_ref[...] = jnp.zeros_like(carry_ref)

    xh, xl = _split(x_ref[...])
    wh, wl = _split(wr_ref[...])
    both = _dot(xh, jnp.concatenate([wh, wl], axis=1))
    logits = both[:, :LANES] + (both[:, LANES:] + _dot(xl, wh)) + br_ref[...]
    lane = lax.broadcasted_iota(jnp.int32, logits.shape, 1)
    m1 = jnp.max(logits, axis=-1, keepdims=True)
    i1 = jnp.min(jnp.where(logits == m1, lane, LANES), axis=-1, keepdims=True)
    rest = jnp.where(lane == i1, -jnp.inf, logits)
    m2 = jnp.max(rest, axis=-1, keepdims=True)
    i2 = jnp.min(jnp.where(rest == m2, lane, LANES), axis=-1, keepdims=True)
    e2 = jnp.exp(m2 - m1)
    g1 = 1.0 / (1.0 + e2)
    g2 = e2 / (1.0 + e2)
    sel1 = lane == i1
    sel2 = lane == i2
    onehot = jnp.where(sel1, 1.0, jnp.where(sel2, 1.0, 0.0))
    before = _dot(ltri_ref[...], onehot.astype(BF16)) + carry_ref[0:1, :]
    r1 = jnp.sum(jnp.where(sel1, before, 0.0), axis=-1, keepdims=True)
    r2 = jnp.sum(jnp.where(sel2, before, 0.0), axis=-1, keepdims=True)
    new_carry = carry_ref[0:1, :] + jnp.sum(onehot, axis=0, keepdims=True)
    carry_ref[...] = jnp.broadcast_to(new_carry, carry_ref.shape)
    cnt_ref[...] = jnp.broadcast_to(new_carry, cnt_ref.shape)
    vals = (i1.astype(F32), i2.astype(F32), r1, r2, g1, g2)
    out = jnp.zeros(logits.shape, F32)
    for k, v in enumerate(vals):
        out = jnp.where(lane == k, v, out)
    o_ref[...] = out


def _router(x, wr, br):
    n = x.shape[0]
    tm = ROW_TILE
    ltri = jnp.asarray(np.tril(np.ones((tm, tm), np.float32), -1), BF16)
    return pl.pallas_call(
        _router_kernel,
        grid=(n // tm,),
        in_specs=[pl.BlockSpec((tm, D_MODEL), lambda i: (i, 0)),
                  _const_spec(wr.shape), _const_spec(br.shape), _const_spec(ltri.shape)],
        out_specs=[pl.BlockSpec((tm, LANES), lambda i: (i, 0)),
                   pl.BlockSpec((HALO, LANES), lambda i: (0, 0))],
        out_shape=[jax.ShapeDtypeStruct((n, LANES), F32),
                   jax.ShapeDtypeStruct((HALO, LANES), F32)],
        scratch_shapes=[pltpu.VMEM((HALO, LANES), F32)],
        compiler_params=_cparams(("arbitrary",)),
        name="router",
    )(x, wr, br, ltri)


def _dispatch_kernel(slot_ref, x_ref, xs_in_ref, xs_ref, sem):
    del xs_in_ref
    tg = x_ref.shape[0]

    def issue(r, carry):
        for k in range(2):
            pltpu.make_async_copy(x_ref.at[pl.ds(r, 1)], xs_ref.at[pl.ds(slot_ref[0, 0, 2 * r + k], 1)],
                                  sem).start()
        return carry

    lax.fori_loop(0, tg, issue, 0, unroll=8)
    for k in range(2):
        pltpu.make_async_copy(x_ref, xs_ref.at[pl.ds(0, tg)], sem).wait()


def _dispatch(x, slots, npad):
    n = x.shape[0]
    tg = GATHER_TILE
    slots3 = slots.reshape(n // tg, 1, 2 * tg)
    zeros = jnp.zeros((npad, D_MODEL), F32)
    return pl.pallas_call(
        _dispatch_kernel,
        grid=(n // tg,),
        in_specs=[pl.BlockSpec((1, 1, 2 * tg), lambda i: (i, 0, 0), memory_space=pltpu.SMEM),
                  pl.BlockSpec((tg, D_MODEL), lambda i: (i, 0)),
                  pl.BlockSpec(memory_space=pl.ANY)],
        out_specs=pl.BlockSpec(memory_space=pl.ANY),
        out_shape=jax.ShapeDtypeStruct((npad, D_MODEL), F32),
        scratch_shapes=[pltpu.SemaphoreType.DMA(())],
        input_output_aliases={2: 0},
        compiler_params=_cparams(("arbitrary",)),
        name="moe_dispatch",
    )(slots3, x, zeros)


def _combine_kernel(slot_ref, ys_ref, x_ref, info_ref, g_ref, b_ref, o_ref, buf_ref, sem):
    tg = x_ref.shape[0]

    def issue(r, carry):
        for k in range(2):
            pltpu.make_async_copy(ys_ref.at[pl.ds(slot_ref[0, 0, 2 * r + k], 1)],
                                  buf_ref.at[k, pl.ds(r, 1)], sem).start()
        return carry

    lax.fori_loop(0, tg, issue, 0, unroll=8)
    for k in range(2):
        pltpu.make_async_copy(ys_ref.at[pl.ds(0, tg)], buf_ref.at[k], sem).wait()
    info = info_ref[...]
    y = info[:, 4:5] * buf_ref[0] + info[:, 5:6] * buf_ref[1]
    o_ref[...] = _layer_norm(ALPHA * x_ref[...] + y, g_ref[...], b_ref[...])


def _combine(ys, slots, x, info, g, b):
    n = x.shape[0]
    tg = GATHER_TILE
    slots3 = slots.reshape(n // tg, 1, 2 * tg)
    return pl.pallas_call(
        _combine_kernel,
        grid=(n // tg,),
        in_specs=[pl.BlockSpec((1, 1, 2 * tg), lambda i: (i, 0, 0), memory_space=pltpu.SMEM),
                  pl.BlockSpec(memory_space=pl.ANY),
                  pl.BlockSpec((tg, D_MODEL), lambda i: (i, 0)),
                  pl.BlockSpec((tg, LANES), lambda i: (i, 0)),
                  _const_spec(g.shape), _const_spec(b.shape)],
        out_specs=pl.BlockSpec((tg, D_MODEL), lambda i: (i, 0)),
        out_shape=jax.ShapeDtypeStruct((n, D_MODEL), F32),
        scratch_shapes=[pltpu.VMEM((2, tg, D_MODEL), F32), pltpu.SemaphoreType.DMA(())],
        compiler_params=_cparams(("arbitrary",)),
        name="moe_combine",
    )(slots3, ys, x, info, g, b)


def _moe(x, wr, br, wg, wu, wd, layer, g, b):
    n = x.shape[0]
    tm = MOE_TILE
    info, counts = _router(x, wr, br)
    experts = info[:, 0:2].astype(jnp.int32)
    ranks = info[:, 2:4].astype(jnp.int32)
    count = counts[0, :N_EXPERTS].astype(jnp.int32)
    tiles = (count + tm - 1) // tm
    tile_end = jnp.cumsum(tiles)
    tile_start = tile_end - tiles
    slots = tile_start[experts] * tm + ranks
    n_tiles = (2 * n) // tm + N_EXPERTS
    n_used = tile_end[-1]
    tile_ids = jnp.arange(n_tiles, dtype=jnp.int32)
    tile_id = jnp.minimum(tile_ids, n_used - 1)
    tile_expert = jnp.sum((tile_id[:, None] >= tile_end[None, :]).astype(jnp.int32), axis=1)
    tile_valid = jnp.clip(count[tile_expert] - (tile_ids - tile_start[tile_expert]) * tm, 0, tm)
    tile_valid = jnp.where(tile_ids < n_used, tile_valid, 0)
    xs = _dispatch(x, slots, n_tiles * tm)
    ys = _experts(tile_expert.astype(jnp.int32), tile_valid.astype(jnp.int32),
                  n_used.reshape(1).astype(jnp.int32), xs, wg, wu, wd, layer)
    return _combine(ys, slots, x, info, g, b)


def _rope(x, c, sa, sb):
    return x * c + pltpu.roll(x, LANES - ROPE_DIM // 2, 1) * sa + pltpu.roll(x, ROPE_DIM // 2, 1) * sb


def _swa_kernel(sink_ref, q_ref, k_ref, v_ref, c_ref, sa_ref, sb_ref, o_ref, kp_ref, vp_ref):
    n = pl.program_id(1)
    blk = SWA_BLOCK

    @pl.when(n == 0)
    def _():
        kp_ref[...] = jnp.zeros_like(kp_ref)
        vp_ref[...] = jnp.zeros_like(vp_ref)

    c = c_ref[...]
    sa = sa_ref[...]
    sb = sb_ref[...]
    lane = lax.broadcasted_iota(jnp.int32, (blk, LANES), 1)
    low = lane < SWA_HEAD_DIM
    row = lax.broadcasted_iota(jnp.int32, (blk, blk), 0)
    col = lax.broadcasted_iota(jnp.int32, (blk, blk), 1)
    mask_cur = col <= row
    zero = jnp.zeros((blk, LANES), BF16)
    nq = q_ref.shape[0] // blk

    def halves(t, g):
        swapped = jnp.concatenate([t[:, SWA_HEAD_DIM:], t[:, :SWA_HEAD_DIM]], axis=1)
        src_a, src_b = (t, swapped) if g == 0 else (swapped, t)
        return jnp.where(low, src_a, zero), jnp.where(low, zero, src_b)

    both = lambda t: [halves(t, g) for g in range(SWA_KV_HEADS)]
    k_blocks = [kp_ref[...]] + [_rope(k_ref[i * blk:(i + 1) * blk, :].astype(F32), c[i * blk:(i + 1) * blk],
                                      sa[i * blk:(i + 1) * blk], sb[i * blk:(i + 1) * blk]).astype(BF16)
                                for i in range(nq)]
    v_blocks = [vp_ref[...]] + [v_ref[i * blk:(i + 1) * blk, :] for i in range(nq)]
    k_halves = [both(t) for t in k_blocks]
    v_halves = [both(t) for t in v_blocks]

    group = SWA_HEADS // SWA_KV_HEADS
    entries = [(i, h) for i in range(nq) for h in range(SWA_HEADS)]
    stack = lambda f: jnp.stack([f(i, h) for i, h in entries])
    q_pairs = [[(_rope(q_ref[i * blk:(i + 1) * blk, pr * LANES:(pr + 1) * LANES].astype(F32),
                       c[i * blk:(i + 1) * blk], sa[i * blk:(i + 1) * blk], sb[i * blk:(i + 1) * blk])
                 * (SWA_HEAD_DIM ** -0.5)).astype(BF16) for pr in range(SWA_HEADS // 2)] for i in range(nq)]
    q_all = stack(lambda i, h: q_pairs[i][h // 2])
    sink = stack(lambda i, h: jnp.full((1, 1), sink_ref[h], F32))
    s_prev = jnp.where(col > row, _bdot_nt(q_all, stack(lambda i, h: k_halves[i][h // group][h % 2])), -jnp.inf)
    s_first = jnp.where(n > 0, s_prev[:SWA_HEADS], -jnp.inf)
    s_prev = s_first if nq == 1 else jnp.concatenate([s_first, s_prev[SWA_HEADS:]], axis=0)
    s_cur = jnp.where(mask_cur, _bdot_nt(q_all, stack(lambda i, h: k_halves[i + 1][h // group][h % 2])), -jnp.inf)
    m = jnp.maximum(jnp.maximum(jnp.max(s_prev, axis=-1, keepdims=True),
                                jnp.max(s_cur, axis=-1, keepdims=True)), sink)
    p_prev = jnp.exp(s_prev - m)
    p_cur = jnp.exp(s_cur - m)
    denom = (jnp.sum(p_prev, axis=-1, keepdims=True) + jnp.sum(p_cur, axis=-1, keepdims=True)
             + jnp.exp(sink - m))
    inv = 1.0 / denom
    out = (_bdot((p_prev * inv).astype(BF16), stack(lambda i, h: v_halves[i][h // group][h % 2]))
           + _bdot((p_cur * inv).astype(BF16), stack(lambda i, h: v_halves[i + 1][h // group][h % 2])))
    for i in range(nq):
        for pr in range(SWA_HEADS // 2):
            e = i * SWA_HEADS + 2 * pr
            o_ref[i * blk:(i + 1) * blk, pr * LANES:(pr + 1) * LANES] = (out[e] + out[e + 1]).astype(o_ref.dtype)

    kp_ref[...] = k_blocks[nq]
    vp_ref[...] = v_blocks[nq]


def _swa(p, sinks, tables, bsz, t, q_col, k_col, v_col):
    rows = SWA_STEP_BLOCKS * SWA_BLOCK
    nb = t // rows
    c, sa, sb = tables
    row = lambda b, n: b * nb + n
    return pl.pallas_call(
        _swa_kernel,
        grid=(bsz, nb),
        in_specs=[pl.BlockSpec(memory_space=pltpu.SMEM),
                  pl.BlockSpec((rows, SWA_Q), lambda b, n: (row(b, n), q_col)),
                  pl.BlockSpec((rows, SWA_KV), lambda b, n: (row(b, n), k_col)),
                  pl.BlockSpec((rows, SWA_KV), lambda b, n: (row(b, n), v_col)),
                  pl.BlockSpec((rows, LANES), lambda b, n: (n, 0)),
                  pl.BlockSpec((rows, LANES), lambda b, n: (n, 0)),
                  pl.BlockSpec((rows, LANES), lambda b, n: (n, 0))],
        out_specs=pl.BlockSpec((rows, SWA_Q), lambda b, n: (row(b, n), 0)),
        out_shape=jax.ShapeDtypeStruct((bsz * t, SWA_Q), BF16),
        scratch_shapes=[pltpu.VMEM((SWA_BLOCK, SWA_KV), BF16), pltpu.VMEM((SWA_BLOCK, SWA_KV), BF16)],
        compiler_params=_cparams(("parallel", "arbitrary")),
        name="swa",
    )(sinks, p, p, p, c, sa, sb)


def _rope_tables(t):
    half = ROPE_DIM // 2
    pos = jnp.arange(t, dtype=jnp.int32)
    inv_freq = jnp.power(ROPE_THETA, -jnp.arange(half, dtype=F32) / half)
    ang = pos.astype(F32)[:, None] * inv_freq[None, :]
    cos = jnp.cos(ang)
    sin = jnp.sin(ang)
    ones = jnp.ones((t, SWA_HEAD_DIM - ROPE_DIM), F32)
    zeros = jnp.zeros((t, SWA_HEAD_DIM - ROPE_DIM), F32)
    zhalf = jnp.zeros((t, half), F32)
    c = jnp.concatenate([cos, cos, ones], axis=1)
    sa = jnp.concatenate([-sin, zhalf, zeros], axis=1)
    sb = jnp.concatenate([zhalf, sin, zeros], axis=1)
    two = lambda a: jnp.concatenate([a, a], axis=1)
    return two(c), two(sa), two(sb)


def _split(a):
    hi = a.astype(BF16)
    return hi, (a - hi.astype(F32)).astype(BF16)


def _bdot(a, b):
    return lax.dot_general(a, b, (((2,), (1,)), ((0,), (0,))), preferred_element_type=F32)


def _bdot_nt(a, b):
    return lax.dot_general(a, b, (((2,), (2,)), ((0,), (0,))), preferred_element_type=F32)


def _mm(a, b):
    if DN_INV_PASSES == 1:
        return _bdot(a.astype(BF16), b.astype(BF16))
    ah, al = _split(a)
    bh, bl = _split(b)
    return _bdot(ah, bh) + (_bdot(ah, bl) + _bdot(al, bh))


def _unit_lower_inverse(a, eye, blk_mask, merge_masks):
    d0 = jnp.where(blk_mask, a, 0.0)
    x = eye - d0
    p = _mm(d0, d0)
    x = x + _mm(x, p)
    p = _mm(p, p)
    x = x + _mm(x, p)
    for m in merge_masks:
        lm = jnp.where(m, a, 0.0)
        x = x - _mm(x, _mm(lm, x))
    return x


def _dn_kernel(q_ref, k_ref, v_ref, gate_ref, tail_ref, cw_ref, par_ref, nw_ref, o_ref,
               xpad_ref, qkv_ref, bg_ref, s_ref, gc_ref, u_ref, oi_ref, kd_ref, wq_ref):
    n = pl.program_id(1)
    tb = q_ref.shape[0]
    ch = DN_CHUNK
    hd = DN_HEAD_DIM

    @pl.when(n == 0)
    def _():
        xpad_ref[0:HALO, :] = jnp.zeros((HALO, 3 * DN_W), F32)
        s_ref[...] = jnp.zeros_like(s_ref)

    xpad_ref[HALO:HALO + tb, 0:DN_W] = q_ref[...].astype(F32)
    xpad_ref[HALO:HALO + tb, DN_W:2 * DN_W] = k_ref[...].astype(F32)
    xpad_ref[HALO:HALO + tb, 2 * DN_W:3 * DN_W] = v_ref[...].astype(F32)
    conv = None
    for kk in range(DN_CONV):
        off = HALO - (DN_CONV - 1) + kk
        term = cw_ref[kk:kk + 1, :] * xpad_ref[off:off + tb, :]
        conv = term if conv is None else conv + term
    xpad_ref[0:HALO, :] = xpad_ref[tb:tb + HALO, :]
    qkv = _silu(conv)
    for h in range(2 * DN_HEADS):
        xh = qkv[:, h * hd:(h + 1) * hd]
        scale = lax.rsqrt(jnp.sum(xh * xh, axis=-1, keepdims=True) + RMS_EPS)
        if h < DN_HEADS:
            scale = scale * (hd ** -0.5)
        qkv_ref[:, h * hd:(h + 1) * hd] = xh * scale
    qkv_ref[:, 2 * DN_W:] = qkv[:, 2 * DN_W:]

    tail = tail_ref[...]
    lane = lax.broadcasted_iota(jnp.int32, tail.shape, 1)
    gval = -jnp.exp(par_ref[0:1, :]) * _softplus(tail + par_ref[1:2, :])
    bg_ref[...] = jnp.where(lane < DN_HEADS, _sigmoid(tail), gval)

    rt = lax.broadcasted_iota(jnp.int32, (tb, tb), 0)
    ct = lax.broadcasted_iota(jnp.int32, (tb, tb), 1)
    ltri = jnp.where((rt >= ct) & ((rt // ch) == (ct // ch)), 1.0, 0.0).astype(F32)
    er = lax.broadcasted_iota(jnp.int32, (LANES, LANES), 0)
    ec = lax.broadcasted_iota(jnp.int32, (LANES, LANES), 1)
    eye = jnp.where(er == ec, 1.0, 0.0).astype(F32)
    bg = bg_ref[...]
    gc_all = _dot(ltri, jnp.where(lane >= DN_HEADS, bg, 0.0), HIGHEST)
    gc_t = _dot_nt(eye, gc_all, HIGHEST)
    gc_ref[...] = gc_all

    sb = 2 * ch
    same = (er // ch) == (ec // ch)
    causal = same & (er >= ec)
    strict = same & (er > ec)
    blk_mask = (er > ec) & ((er // 8) == (ec // 8))
    merge_masks = [((er // (2 * s)) == (ec // (2 * s))) & ((er % (2 * s)) >= s) & ((ec % (2 * s)) < s)
                   for s in (8, 16, 32)]
    first = lax.broadcasted_iota(jnp.int32, (sb, 1), 0) < ch
    probs = [(h, b2) for h in range(DN_HEADS) for b2 in range(tb // sb)]
    rows_of = lambda b2: slice(b2 * sb, (b2 + 1) * sb)
    stack = lambda f: jnp.stack([f(h, b2) for h, b2 in probs])
    gcol = stack(lambda h, b2: gc_all[rows_of(b2), DN_HEADS + h:DN_HEADS + h + 1])
    grow = stack(lambda h, b2: gc_t[DN_HEADS + h:DN_HEADS + h + 1, rows_of(b2)])
    beta = stack(lambda h, b2: bg[rows_of(b2), h:h + 1])
    q = stack(lambda h, b2: qkv_ref[rows_of(b2), h * hd:(h + 1) * hd])
    k = stack(lambda h, b2: qkv_ref[rows_of(b2), DN_W + h * hd:DN_W + (h + 1) * hd])
    v = stack(lambda h, b2: qkv_ref[rows_of(b2), 2 * DN_W + h * hd:2 * DN_W + (h + 1) * hd])
    decay = jnp.where(causal, jnp.exp(gcol - grow), 0.0)
    k16 = k.astype(BF16)
    a = jnp.where(strict, _bdot_nt(k16, k16) * decay * beta, 0.0)
    tinv = _unit_lower_inverse(a, eye, blk_mask, merge_masks)
    egc = jnp.exp(gcol)
    u = _mm(tinv, v * beta)
    w16 = _mm(tinv, k * (beta * egc)).astype(BF16)
    attn16 = (_bdot_nt(q.astype(BF16), k16) * decay).astype(BF16)
    q_eff16 = (q * egc - _bdot(attn16, w16)).astype(BF16)
    o_intra = _bdot(attn16, u.astype(BF16))
    glast = jnp.where(first, gcol[:, ch - 1:ch, :], gcol[:, sb - 1:sb, :])
    k_dec16 = (k * jnp.exp(glast - gcol)).astype(BF16)
    for g, (h, b2) in enumerate(probs):
        u_ref[h, rows_of(b2), :] = u[g]
        oi_ref[h, rows_of(b2), :] = o_intra[g]
        kd_ref[h, rows_of(b2), :] = k_dec16[g]
        for c in range(2):
            wq_ref[h, 2 * b2 + c, 0:ch, :] = w16[g, c * ch:(c + 1) * ch]
            wq_ref[h, 2 * b2 + c, ch:sb, :] = q_eff16[g, c * ch:(c + 1) * ch]

    for c in range(tb // ch):
        rows = slice(c * ch, (c + 1) * ch)
        s = s_ref[...]
        r = _bdot(wq_ref[:, c], s.astype(BF16))
        v_new16 = (u_ref[:, rows, :] - r[:, 0:ch]).astype(BF16)
        for h in range(DN_HEADS):
            glast = gc_ref[(c + 1) * ch - 1:(c + 1) * ch, DN_HEADS + h:DN_HEADS + h + 1]
            s_ref[h] = s[h] * jnp.exp(glast) + _dot_tn(kd_ref[h, rows, :], v_new16[h])
        for h in range(DN_HEADS):
            o = r[h, ch:sb] + oi_ref[h, rows, :]
            gate = gate_ref[rows, h * hd:(h + 1) * hd].astype(F32)
            on = o * lax.rsqrt(jnp.mean(o * o, axis=-1, keepdims=True) + RMS_EPS) * nw_ref[...]
            o_ref[rows, h * hd:(h + 1) * hd] = (on * _silu(gate)).astype(o_ref.dtype)


def _deltanet(p, tail, conv_w, par, norm_w, bsz, t, cols):
    tb = DN_BLOCK
    nb = t // tb
    row = lambda b, n: b * nb + n
    col_spec = lambda cidx: pl.BlockSpec((tb, DN_W), lambda b, n: (row(b, n), cidx))
    return pl.pallas_call(
        _dn_kernel,
        grid=(bsz, nb),
        in_specs=[col_spec(cols[0]), col_spec(cols[1]), col_spec(cols[2]), col_spec(cols[3]),
                  pl.BlockSpec((tb, LANES), lambda b, n: (row(b, n), 0)),
                  _const_spec(conv_w.shape), _const_spec(par.shape), _const_spec(norm_w.shape)],
        out_specs=pl.BlockSpec((tb, DN_W), lambda b, n: (row(b, n), 0)),
        out_shape=jax.ShapeDtypeStruct((bsz * t, DN_W), BF16),
        scratch_shapes=[pltpu.VMEM((tb + HALO, 3 * DN_W), F32),
                        pltpu.VMEM((tb, 3 * DN_W), F32),
                        pltpu.VMEM((tb, LANES), F32),
                        pltpu.VMEM((DN_HEADS, DN_HEAD_DIM, DN_HEAD_DIM), F32),
                        pltpu.VMEM((tb, LANES), F32),
                        pltpu.VMEM((DN_HEADS, tb, DN_HEAD_DIM), F32),
                        pltpu.VMEM((DN_HEADS, tb, DN_HEAD_DIM), F32),
                        pltpu.VMEM((DN_HEADS, tb, DN_HEAD_DIM), BF16),
                        pltpu.VMEM((DN_HEADS, tb // DN_CHUNK, 2 * DN_CHUNK, DN_HEAD_DIM), BF16)],
        compiler_params=_cparams(("parallel", "arbitrary")),
        name="deltanet",
    )(p, p, p, p, tail, conv_w, par, norm_w)


def _ssd_kernel(z_ref, xs_ref, bc_ref, dt_ref, cw_ref, cb_ref, pcol_ref, prow_ref,
                dskip_ref, nw_ref, expand_ref, o_ref, xpad_ref, xc_ref, h_ref):
    n = pl.program_id(1)
    L = SSM_CHUNK

    @pl.when(n == 0)
    def _():
        xpad_ref[0:L, :] = jnp.zeros((L, SSM_CONV_DIM), BF16)
        h_ref[...] = jnp.zeros_like(h_ref)

    xpad_ref[L:2 * L, 0:SSM_D_INNER] = xs_ref[...]
    xpad_ref[L:2 * L, SSM_D_INNER:] = bc_ref[...]
    sr = lax.broadcasted_iota(jnp.int32, (L, 2 * L), 0)
    sc = lax.broadcasted_iota(jnp.int32, (L, 2 * L), 1)
    shifts = [jnp.where(sc == sr + (L - 3 + kk), 1.0, 0.0).astype(BF16) for kk in range(3)]
    cw_blk = 256
    for c0 in range(0, SSM_CONV_DIM, cw_blk):
        cols = slice(c0, c0 + cw_blk)
        xe = xpad_ref[:, cols]
        conv = cb_ref[:, cols] + cw_ref[3:4, cols] * xe[L:2 * L].astype(F32)
        for kk in range(3):
            conv = conv + cw_ref[kk:kk + 1, cols] * _dot(shifts[kk], xe)
        xc_ref[:, cols] = _silu(conv)
    xpad_ref[0:L, :] = xpad_ref[L:2 * L, :]

    ri = lax.broadcasted_iota(jnp.int32, (L, L), 0)
    ci = lax.broadcasted_iota(jnp.int32, (L, L), 1)
    causal = ri >= ci
    ltri = jnp.where(causal, 1.0, 0.0).astype(F32)
    utri = jnp.where(ri <= ci, 1.0, 0.0).astype(F32)
    eye = jnp.where(ri == ci, 1.0, 0.0).astype(F32)

    dt_raw = dt_ref[...]
    dt = _softplus(dt_raw + pcol_ref[0:1, :])
    acs = _dot(ltri, dt * pcol_ref[1:2, :], HIGHEST)
    tot = acs[L - 1:L, :]
    f1 = dt * jnp.exp(tot - acs)
    ea = jnp.exp(acs)
    dt_r = _softplus(_dot_nt(eye, dt_raw, HIGHEST)[0:SSM_HEADS] + prow_ref[0])
    acs_r = _dot(dt_r * prow_ref[1], utri, HIGHEST)

    stacked = jnp.concatenate([dt, f1, ea], axis=0).astype(BF16)
    cd = jnp.broadcast_to(jnp.exp(tot), (HALO, LANES))
    cd_hi, cd_lo = _split(cd)
    lane = lax.broadcasted_iota(jnp.int32, (L, LANES), 1)
    low = lane < SSM_HEAD_DIM
    zero16 = jnp.zeros((L, LANES), BF16)
    for g in range(SSM_GROUPS):
        gl = slice(g * SSM_GROUP_W, (g + 1) * SSM_GROUP_W)
        ex = expand_ref[:, gl]
        wide = _dot(stacked, ex)
        cd_x = (_dot(cd_hi, ex) + _dot(cd_lo, ex))[0:1, :]
        xs = xc_ref[:, gl]
        xdt16 = (xs * wide[0:L]).astype(BF16)
        xdd16 = (xs * wide[L:2 * L]).astype(BF16)
        bm = xc_ref[:, SSM_D_INNER + g * SSM_STATE:SSM_D_INNER + (g + 1) * SSM_STATE].astype(BF16)
        cm = xc_ref[:, SSM_D_INNER + SSM_GN + g * SSM_STATE:
                    SSM_D_INNER + SSM_GN + (g + 1) * SSM_STATE].astype(BF16)
        cb = _dot_nt(cm, bm)
        hg = h_ref[g]
        y_off = _dot(cm, hg.astype(BF16)) * wide[2 * L:3 * L]
        h_ref[g] = hg * cd_x + _dot_tn(bm, xdd16)
        parts = []
        for j in range(SSM_GROUP_W // LANES):
            xp16 = xdt16[:, j * LANES:(j + 1) * LANES]
            yd = None
            for side in range(2):
                hidx = (g * SSM_GROUP_W + j * LANES) // SSM_HEAD_DIM + side
                seg = jnp.where(causal, jnp.exp(acs[:, hidx:hidx + 1] - acs_r[hidx:hidx + 1, :]), 0.0)
                m = (cb * seg).astype(BF16)
                xm = jnp.where(low, xp16, zero16) if side == 0 else jnp.where(low, zero16, xp16)
                part = _dot(m, xm)
                yd = part if yd is None else yd + part
            parts.append(yd)
        y = jnp.concatenate(parts, axis=1) + y_off + dskip_ref[:, gl] * xs
        y = y * _silu(z_ref[:, gl].astype(F32))
        yn = y * lax.rsqrt(jnp.mean(y * y, axis=-1, keepdims=True) + RMS_EPS) * nw_ref[:, gl]
        o_ref[:, gl] = yn.astype(o_ref.dtype)


def _ssd(p, dt, conv_w, conv_b, pcol, prow, dskip, norm_w, expand, bsz, t):
    L = SSM_CHUNK
    nb = t // L
    row = lambda b, n: b * nb + n
    return pl.pallas_call(
        _ssd_kernel,
        grid=(bsz, nb),
        in_specs=[pl.BlockSpec((L, SSM_D_INNER), lambda b, n: (row(b, n), 0)),
                  pl.BlockSpec((L, SSM_D_INNER), lambda b, n: (row(b, n), 1)),
                  pl.BlockSpec((L, 2 * SSM_GN), lambda b, n: (row(b, n), 2 * SSM_D_INNER // (2 * SSM_GN))),
                  pl.BlockSpec((L, LANES), lambda b, n: (row(b, n), 0)),
                  _const_spec(conv_w.shape), _const_spec(conv_b.shape), _const_spec(pcol.shape),
                  _const_spec(prow.shape), _const_spec(dskip.shape), _const_spec(norm_w.shape),
                  _const_spec(expand.shape)],
        out_specs=pl.BlockSpec((L, SSM_D_INNER), lambda b, n: (row(b, n), 0)),
        out_shape=jax.ShapeDtypeStruct((bsz * t, SSM_D_INNER), BF16),
        scratch_shapes=[pltpu.VMEM((2 * L, SSM_CONV_DIM), BF16),
                        pltpu.VMEM((L, SSM_CONV_DIM), F32),
                        pltpu.VMEM((SSM_GROUPS, SSM_STATE, SSM_GROUP_W), F32)],
        compiler_params=_cparams(("parallel", "arbitrary")),
        name="ssd",
    )(p, p, p, dt, conv_w, conv_b, pcol, prow, dskip, norm_w, expand)


def _pad_lanes(v, offset=0):
    v = v.astype(F32)
    return jnp.zeros((1, LANES), F32).at[0, offset:offset + v.shape[0]].set(v)


def _even_layer(x, bsz, t, w_in, sinks, conv_w, a_log, dt_bias, norm_w, w_out, ln_g, ln_b,
                ffn_wg, ffn_wu, ffn_wd, rope_tables):
    o = np.cumsum((0, SWA_Q, SWA_KV, SWA_KV, DN_W, DN_W, DN_W, DN_W, DN_HEADS, DN_HEADS))
    seg = lambda i: w_in[:, o[i]:o[i + 1]]
    w_main = jnp.concatenate([seg(0), seg(3), seg(4), seg(5), seg(6), seg(1), seg(2)], axis=1).astype(BF16)
    w_tail = w_in[:, o[7]:o[9]]
    wt = jnp.zeros((D_MODEL, LANES), F32).at[:, :2 * DN_HEADS].set(w_tail)
    p, tail = _proj(x, w_main, wt)
    out_a = _swa(p, sinks.astype(F32), rope_tables, bsz, t,
                 q_col=0, k_col=(SWA_Q + 4 * DN_W) // SWA_KV, v_col=(SWA_Q + 4 * DN_W) // SWA_KV + 1)
    par = jnp.concatenate([_pad_lanes(a_log, DN_HEADS), _pad_lanes(dt_bias, DN_HEADS)], axis=0)
    out_b = _deltanet(p, tail, conv_w.astype(F32), par, norm_w.reshape(1, DN_HEAD_DIM).astype(F32),
                      bsz, t, cols=(1, 2, 3, 4))
    x = _mix_out([out_a, out_b], w_out.astype(BF16), x, ln_g[0:1], ln_b[0:1])
    return _ffn(x, ffn_wg.astype(BF16), ffn_wu.astype(BF16), ffn_wd.astype(BF16), ln_g[1:2], ln_b[1:2])


def _odd_layer(x, bsz, t, w_in, conv_w, conv_b, dt_bias, a_log, d_skip, norm_w, w_out, ln_g, ln_b,
               w_router, b_router, moe_wg, moe_wu, moe_wd, moe_layer):
    main = SSM_D_INNER + SSM_CONV_DIM
    w_tail = w_in[:, main:]
    wt = jnp.zeros((D_MODEL, LANES), F32).at[:, :SSM_HEADS].set(w_tail)
    p, dt = _proj(x, w_in[:, :main].astype(BF16), wt)
    a = -jnp.exp(a_log.astype(F32))
    pcol = jnp.concatenate([_pad_lanes(dt_bias), _pad_lanes(a)], axis=0)
    prow = jnp.stack([jnp.broadcast_to(dt_bias.astype(F32)[:, None], (SSM_HEADS, SSM_CHUNK)),
                      jnp.broadcast_to(a[:, None], (SSM_HEADS, SSM_CHUNK))])
    dskip = jnp.repeat(d_skip.astype(F32), SSM_HEAD_DIM)[None, :]
    expand = (jnp.arange(LANES)[:, None] == (jnp.arange(SSM_D_INNER) // SSM_HEAD_DIM)[None, :]).astype(BF16)
    y = _ssd(p, dt, conv_w.astype(F32), conv_b.astype(F32)[None, :], pcol, prow, dskip,
             norm_w.astype(F32)[None, :], expand, bsz, t)
    x = _mix_out([y], w_out.astype(BF16), x, ln_g[0:1], ln_b[0:1])
    wr = jnp.zeros((D_MODEL, LANES), F32).at[:, :N_EXPERTS].set(w_router)
    br = jnp.full((1, LANES), -1e30, F32).at[0, :N_EXPERTS].set(b_router.astype(F32))
    return _moe(x, wr, br, moe_wg, moe_wu, moe_wd, moe_layer, ln_g[1:2], ln_b[1:2])


def kernel(x, ln_g, ln_b, even_w_in, swa_sinks, dn_conv_w, dn_a_log, dn_dt_bias, dn_norm_w, even_w_out,
           ssm_w_in, ssm_conv_w, ssm_conv_b, ssm_dt_bias, ssm_a_log, ssm_d, ssm_norm_w, ssm_w_out,
           ffn_w_gate, ffn_w_up, ffn_w_down, moe_w_router, moe_b_router, moe_w_gate, moe_w_up, moe_w_down):
    bsz, t, d = x.shape
    h = x.reshape(bsz * t, d)
    rope_tables = _rope_tables(t)
    for i in range(DEPTH):
        j = i // 2
        if i % 2 == 0:
            h = _even_layer(h, bsz, t, even_w_in[j], swa_sinks[j], dn_conv_w[j], dn_a_log[j], dn_dt_bias[j],
                            dn_norm_w[j], even_w_out[j], ln_g[i], ln_b[i],
                            ffn_w_gate[j], ffn_w_up[j], ffn_w_down[j], rope_tables)
        else:
            h = _odd_layer(h, bsz, t, ssm_w_in[j], ssm_conv_w[j], ssm_conv_b[j], ssm_dt_bias[j], ssm_a_log[j],
                           ssm_d[j], ssm_norm_w[j], ssm_w_out[j], ln_g[i], ln_b[i],
                           moe_w_router[j], moe_b_router[j], moe_w_gate, moe_w_up, moe_w_down, j)
    return h.reshape(bsz, t, d)
```
